```python
import math
import jax
import jax.numpy as jnp
from jax import lax
import numpy as np

D_MODEL = 1024
BATCH = 16
SEQ = 2048
DEPTH = 2

CHUNK = 64
Q_BLOCK = 128
D_FF = 4 * D_MODEL
NORM_EPS = 1e-6

FOX_HEADS = 8
FOX_HEAD_DIM = D_MODEL // 16
FOX_W = FOX_HEADS * FOX_HEAD_DIM
FOX_F_BIAS_MEAN = 2.0
GDN_HEADS = 4
GDN_HEAD_DIM = D_MODEL // 8
GDN_W = GDN_HEADS * GDN_HEAD_DIM
GDN_CONV = 4
EVEN_SPLIT = (FOX_W, FOX_W, FOX_W, FOX_HEADS,
              GDN_W, GDN_W, GDN_W, GDN_HEADS, GDN_HEADS, GDN_W)
EVEN_IN = 3 * FOX_W + FOX_HEADS + 4 * GDN_W + 2 * GDN_HEADS

HGRN_EXPAND = 128
HGRN_HEADS = D_MODEL // HGRN_EXPAND
HGRN_HEAD_DIM = HGRN_EXPAND
ODD_IN = 4 * D_MODEL

kernel_name = "fox_gdn_hgrn2_hybrid_trunk"


def rms_norm(x, gain):
    xf = x.astype(jnp.float32)
    y = xf * lax.rsqrt(jnp.mean(xf * xf, axis=-1, keepdims=True) + NORM_EPS)
    return (y * gain.astype(jnp.float32)).astype(x.dtype)


def l2_normalize(x):
    xf = x.astype(jnp.float32)
    return xf * lax.rsqrt(jnp.sum(xf * xf, axis=-1, keepdims=True) + NORM_EPS)


def split_cols(z, sizes):
    out, start = [], 0
    for s in sizes:
        out.append(z[..., start:start + s])
        start += s
    return out


def causal_depthwise_conv(x, w):
    k_w, c = w.shape
    return lax.conv_general_dilated(
        x, w[:, None, :].astype(x.dtype), window_strides=(1,),
        padding=[(k_w - 1, 0)], dimension_numbers=("NWC", "WIO", "NWC"),
        feature_group_count=c)


def squared_relu_mlp(h, w1, w2):
    a = jax.nn.relu(h @ w1)
    return (a * a) @ w2


def forgetting_attention(q, k, v, f_logit):
    b, t, h, dh = q.shape
    c = jnp.cumsum(jax.nn.log_sigmoid(f_logit.astype(jnp.float32)), axis=1)
    c = c.transpose(0, 2, 1)
    scale = dh ** -0.5
    outs = []
    for i in range(t // Q_BLOCK):
        q0, q1 = i * Q_BLOCK, (i + 1) * Q_BLOCK
        s = jnp.einsum("bqhd,bkhd->bhqk", q[:, q0:q1], k[:, :q1]).astype(jnp.float32) * scale
        s = s + c[:, :, q0:q1, None] - c[:, :, None, :q1]
        mask = (q0 + jnp.arange(Q_BLOCK))[:, None] >= jnp.arange(q1)[None, :]
        s = jnp.where(mask, s, -jnp.inf)
        p = jax.nn.softmax(s, axis=-1).astype(v.dtype)
        outs.append(jnp.einsum("bhqk,bkhd->bqhd", p, v[:, :q1]))
    return jnp.concatenate(outs, axis=1)


def gated_delta_rule(q, k, v, log_alpha, beta):
    b, h, t, dk = q.shape
    dv = v.shape[-1]
    n = t // CHUNK
    rs = lambda a: a.reshape(b, h, n, CHUNK, *a.shape[3:])
    q = rs(q * dk ** -0.5)
    k, v = rs(k), rs(v)
    beta = rs(beta)
    g = jnp.cumsum(rs(log_alpha), axis=-1)
    idx = jnp.arange(CHUNK)
    causal = idx[:, None] >= idx[None, :]
    strict = idx[:, None] > idx[None, :]
    decay = jnp.exp(jnp.where(causal, g[..., :, None] - g[..., None, :], -jnp.inf))
    k_beta = k * beta[..., None]
    a_mat = jnp.where(strict, jnp.einsum("bhnck,bhndk->bhncd", k_beta, k) * decay, 0.0)
    eye = jnp.eye(CHUNK, dtype=jnp.float32)
    rhs = jnp.concatenate([v * beta[..., None], k_beta * jnp.exp(g)[..., None]], axis=-1)
    wu = lax.linalg.triangular_solve(eye + a_mat, rhs, left_side=True, lower=True,
                                     unit_diagonal=True)
    u, w = wu[..., :dv], wu[..., dv:]
    qk = jnp.where(causal, jnp.einsum("bhnck,bhndk->bhncd", q, k) * decay, 0.0)
    q_dec = q * jnp.exp(g)[..., None]
    g_last = g[..., -1]
    k_dec = k * jnp.exp(g_last[..., None] - g)[..., None]

    def step(s_state, xs):
        qk_i, q_dec_i, k_dec_i, u_i, w_i, gl_i = xs
        v_new = u_i - jnp.einsum("bhck,bhkv->bhcv", w_i, s_state)
        o = (jnp.einsum("bhck,bhkv->bhcv", q_dec_i, s_state)
             + jnp.einsum("bhcd,bhdv->bhcv", qk_i, v_new))
        s_state = (s_state * jnp.exp(gl_i)[..., None, None]
                   + jnp.einsum("bhck,bhcv->bhkv", k_dec_i, v_new))
        return s_state, o

    xs = tuple(jnp.moveaxis(a, 2, 0) for a in (qk, q_dec, k_dec, u, w, g_last))
    s0 = jnp.zeros((b, h, dk, dv), jnp.float32)
    _, o = lax.scan(step, s0, xs)
    return jnp.moveaxis(o, 0, 2).reshape(b, h, t, dv)


def hgrn2_recurrence(q, k, i_val, log_f):
    b, h, t, dk = q.shape
    dv = i_val.shape[-1]
    n = t // CHUNK
    rs = lambda a: a.reshape(b, h, n, CHUNK, a.shape[-1])
    q, k, i_val = rs(q), rs(k), rs(i_val)
    bcum = jnp.cumsum(rs(log_f), axis=-2)
    b_last = bcum[..., -1, :]
    q_dec = q * jnp.exp(bcum)
    k_dec = k * jnp.exp(b_last[..., None, :] - bcum)
    idx = jnp.arange(CHUNK)
    causal = (idx[:, None] >= idx[None, :])[:, :, None]

    def step(s_state, xs):
        q_i, k_i, v_i, b_i, q_dec_i, k_dec_i, bl_i = xs
        diff = b_i[:, :, :, None, :] - b_i[:, :, None, :, :]
        dec = jnp.exp(jnp.where(causal, diff, -jnp.inf))
        a_mat = jnp.einsum("bhtk,bhsk,bhtsk->bhts", q_i, k_i, dec)
        o = (jnp.einsum("bhtk,bhkv->bhtv", q_dec_i, s_state)
             + jnp.einsum("bhts,bhsv->bhtv", a_mat, v_i))
        s_state = (s_state * jnp.exp(bl_i)[..., None]
                   + jnp.einsum("bhsk,bhsv->bhkv", k_dec_i, v_i))
        return s_state, o

    xs = tuple(jnp.moveaxis(a, 2, 0) for a in (q, k, i_val, bcum, q_dec, k_dec, b_last))
    s0 = jnp.zeros((b, h, dk, dv), jnp.float32)
    _, o = lax.scan(step, s0, xs)
    return jnp.moveaxis(o, 0, 2).reshape(b, h, t, dv)


def fox_gdn_mixer(h, w_in, fox_q_norm, fox_k_norm, fox_f_bias, gdn_conv, gdn_A_log,
                  gdn_dt_bias, gdn_o_norm, w_out):
    b, t, _ = h.shape
    z = h @ w_in
    fq, fk, fv, ff, gq, gk, gv, gb, ga, gg = split_cols(z, EVEN_SPLIT)
    fq = rms_norm(fq.reshape(b, t, FOX_HEADS, FOX_HEAD_DIM), fox_q_norm)
    fk = rms_norm(fk.reshape(b, t, FOX_HEADS, FOX_HEAD_DIM), fox_k_norm)
    fv = fv.reshape(b, t, FOX_HEADS, FOX_HEAD_DIM)
    fox_o = forgetting_attention(fq, fk, fv, ff + fox_f_bias)
    fox_o = fox_o.reshape(b, t, FOX_W).astype(h.dtype)
    conv = jax.nn.silu(causal_depthwise_conv(jnp.concatenate([gq, gk, gv], axis=-1), gdn_conv))
    cq, ck, cv = split_cols(conv, (GDN_W, GDN_W, GDN_W))
    to_heads = lambda a: a.reshape(b, t, GDN_HEADS, GDN_HEAD_DIM).transpose(0, 2, 1, 3)
    q = to_heads(l2_normalize(cq.reshape(b, t, GDN_HEADS, GDN_HEAD_DIM)).reshape(b, t, GDN_W))
    k = to_heads(l2_normalize(ck.reshape(b, t, GDN_HEADS, GDN_HEAD_DIM)).reshape(b, t, GDN_W))
    v = to_heads(cv.astype(jnp.float32))
    beta = jax.nn.sigmoid(gb.astype(jnp.float32)).transpose(0, 2, 1)
    log_alpha = (-jnp.exp(gdn_A_log.astype(jnp.float32))
                 * jax.nn.softplus(ga.astype(jnp.float32) + gdn_dt_bias.astype(jnp.float32)))
    gdn_o = gated_delta_rule(q, k, v, log_alpha.transpose(0, 2, 1), beta)
    gdn_o = gdn_o.transpose(0, 2, 1, 3)
    gate = jax.nn.silu(gg.astype(jnp.float32)).reshape(b, t, GDN_HEADS, GDN_HEAD_DIM)
    gdn_o = (rms_norm(gdn_o, gdn_o_norm) * gate).reshape(b, t, GDN_W).astype(h.dtype)
    return jnp.concatenate([fox_o, gdn_o], axis=-1) @ w_out


def hgrn2_mixer(h, w_in, lower_bound, o_norm, w_out):
    b, t, _ = h.shape
    z = h @ w_in
    zq, zf, zi, zg = split_cols(z, (D_MODEL, D_MODEL, D_MODEL, D_MODEL))
    q = jax.nn.silu(zq.astype(jnp.float32))
    f = lower_bound + (1.0 - lower_bound) * jax.nn.sigmoid(zf.astype(jnp.float32))
    k = 1.0 - f
    log_f = jnp.log(f)
    to_heads = lambda a: a.reshape(b, t, HGRN_HEADS, HGRN_HEAD_DIM).transpose(0, 2, 1, 3)
    o = hgrn2_recurrence(to_heads(q), to_heads(k), to_heads(zi.astype(jnp.float32)),
                         to_heads(log_f))
    o = o.transpose(0, 2, 1, 3)
    gate = jax.nn.silu(zg.astype(jnp.float32)).reshape(b, t, HGRN_HEADS, HGRN_HEAD_DIM)
    o = (rms_norm(o, o_norm) * gate).reshape(b, t, D_MODEL).astype(h.dtype)
    return o @ w_out


def setup_inputs(seed: int = 0) -> dict:
    key = jax.random.key(seed)
    ks = jax.random.split(key, 24)
    f32 = jnp.float32

    def dense(k, fan_in, fan_out):
        return jax.random.normal(k, (fan_in, fan_out), f32) * fan_in ** -0.5

    def gain(k, n):
        return 1.0 + 0.1 * jax.random.normal(k, (n,), f32)

    dt = jnp.exp(jax.random.uniform(ks[8], (GDN_HEADS,), f32, math.log(1e-3), math.log(1e-1)))
    return {
        "x": jax.random.normal(ks[0], (BATCH, SEQ, D_MODEL), f32),
        "l0_mix_norm": gain(ks[1], D_MODEL),
        "l0_w_in": dense(ks[2], D_MODEL, EVEN_IN),
        "l0_fox_q_norm": gain(ks[3], FOX_HEAD_DIM),
        "l0_fox_k_norm": gain(ks[4], FOX_HEAD_DIM),
        "l0_fox_f_bias": FOX_F_BIAS_MEAN + 0.1 * jax.random.normal(ks[5], (FOX_HEADS,), f32),
        "l0_gdn_conv": jax.random.normal(ks[6], (GDN_CONV, 3 * GDN_W), f32) * GDN_CONV ** -0.5,
        "l0_gdn_A_log": jnp.log(jax.random.uniform(ks[7], (GDN_HEADS,), f32, 1.0, 16.0)),
        "l0_gdn_dt_bias": dt + jnp.log(-jnp.expm1(-dt)),
        "l0_gdn_o_norm": gain(ks[9], GDN_HEAD_DIM),
        "l0_w_out": dense(ks[10], FOX_W + GDN_W, D_MODEL),
        "l0_ffn_norm": gain(ks[11], D_MODEL),
        "l0_w_ff1": dense(ks[12], D_MODEL, D_FF),
        "l0_w_ff2": dense(ks[13], D_FF, D_MODEL),
        "l1_mix_norm": gain(ks[14], D_MODEL),
        "l1_w_in": dense(ks[15], D_MODEL, ODD_IN),
        "l1_hgrn_o_norm": gain(ks[16], HGRN_HEAD_DIM),
        "l1_w_out": dense(ks[17], D_MODEL, D_MODEL),
        "l1_ffn_norm": gain(ks[18], D_MODEL),
        "l1_w_ff1": dense(ks[19], D_MODEL, D_FF),
        "l1_w_ff2": dense(ks[20], D_FF, D_MODEL),
        "hgrn_lb_logits": 0.5 * jax.random.normal(ks[21], (DEPTH, D_MODEL), f32),
    }


def reference(x, l0_mix_norm, l0_w_in, l0_fox_q_norm, l0_fox_k_norm, l0_fox_f_bias,
              l0_gdn_conv, l0_gdn_A_log, l0_gdn_dt_bias, l0_gdn_o_norm, l0_w_out,
              l0_ffn_norm, l0_w_ff1, l0_w_ff2, l1_mix_norm, l1_w_in, l1_hgrn_o_norm,
              l1_w_out, l1_ffn_norm, l1_w_ff1, l1_w_ff2, hgrn_lb_logits):
    lb_soft = jax.nn.softmax(hgrn_lb_logits.astype(jnp.float32), axis=0)
    lower_bounds = jnp.cumsum(lb_soft, axis=0) - lb_soft[0]
    layers = (
        dict(mix_norm=l0_mix_norm, ffn_norm=l0_ffn_norm, w_ff1=l0_w_ff1, w_ff2=l0_w_ff2),
        dict(mix_norm=l1_mix_norm, ffn_norm=l1_ffn_norm, w_ff1=l1_w_ff1, w_ff2=l1_w_ff2),
    )
    for l in range(DEPTH):
        p = layers[l]
        h = rms_norm(x, p["mix_norm"])
        if l % 2 == 0:
            mix = fox_gdn_mixer(h, l0_w_in, l0_fox_q_norm, l0_fox_k_norm, l0_fox_f_bias,
                                l0_gdn_conv, l0_gdn_A_log, l0_gdn_dt_bias, l0_gdn_o_norm,
                                l0_w_out)
        else:
            mix = hgrn2_mixer(h, l1_w_in, lower_bounds[l], l1_hgrn_o_norm, l1_w_out)
        x = x + mix.astype(x.dtype)
        x = x + squared_relu_mlp(rms_norm(x, p["ffn_norm"]), p["w_ff1"], p["w_ff2"]).astype(x.dtype)
    return x
```

```python
import functools

import jax
import jax.numpy as jnp
from jax import lax
from jax.experimental import pallas as pl
from jax.experimental.pallas import tpu as pltpu

BF = jnp.bfloat16
F32 = jnp.float32
NORM_EPS = 1e-6
CHUNK = 64
FOX_HEADS, FOX_HEAD_DIM = 8, 64
GDN_HEADS, GDN_HEAD_DIM = 4, 128
HGRN_HEADS, HGRN_HEAD_DIM = 8, 128
GDN_CONV = 4
LANES = 128
SUBLANES = 8
VMEM_LIMIT = 56 * 1024 * 1024


def _cparams(sem):
    return pltpu.CompilerParams(dimension_semantics=sem, vmem_limit_bytes=VMEM_LIMIT)


def _const_spec(shape):
    nd = len(shape)
    return pl.BlockSpec(shape, lambda *_: (0,) * nd, pipeline_mode=pl.Buffered(1))


def _mm(a, b):
    return jnp.dot(a.astype(BF), b.astype(BF), preferred_element_type=F32)


def _mm_nt(a, b):
    return lax.dot_general(a.astype(BF), b.astype(BF), (((1,), (1,)), ((), ())),
                           preferred_element_type=F32)


def _mm_tn(a, b):
    return lax.dot_general(a.astype(BF), b.astype(BF), (((0,), (0,)), ((), ())),
                           preferred_element_type=F32)


def _split3(x):
    hi = x.astype(BF)
    r1 = x - hi.astype(F32)
    mid = r1.astype(BF)
    lo = (r1 - mid.astype(F32)).astype(BF)
    return hi, mid, lo


def _mm_exact_lhs(m_bf, x):
    hi, mid, lo = _split3(x)
    d = lambda p: jnp.dot(m_bf, p, preferred_element_type=F32)
    return d(hi) + d(mid) + d(lo)


def _mm_exact_rhs(x, m_bf):
    hi, mid, lo = _split3(x)
    d = lambda p: jnp.dot(p, m_bf, preferred_element_type=F32)
    return d(hi) + d(mid) + d(lo)


def _sigmoid(x):
    return 1.0 / (1.0 + jnp.exp(-x))


def _silu(x):
    return x * _sigmoid(x)


def _softplus(x):
    return jnp.maximum(x, 0.0) + jnp.log1p(jnp.exp(-jnp.abs(x)))


def _rms(x, gain):
    return x * lax.rsqrt(jnp.mean(x * x, axis=-1, keepdims=True) + NORM_EPS) * gain


def _in0_kernel(x_ref, g_ref, wb_ref, ws_ref, pm_ref, qkg_ref, zb_ref, zs_ref):
    h = _rms(x_ref[...], g_ref[...]).astype(BF)
    gw = 512
    for c in range(wb_ref.shape[1] // gw):
        z = jnp.dot(h, wb_ref[:, c * gw:(c + 1) * gw], preferred_element_type=F32)
        if c < 2:
            ms = jnp.dot((z * z).astype(BF), pm_ref[...], preferred_element_type=F32)
            z = z * lax.rsqrt(ms + NORM_EPS) * qkg_ref[c:c + 1, :]
        zb_ref[:, c * gw:(c + 1) * gw] = z.astype(BF)
    zs_ref[...] = jnp.dot(h, ws_ref[...], preferred_element_type=F32)


def _in0(x2, gain, wb, ws, pm, qkg, tm):
    n, d = x2.shape
    nb = wb.shape[1]
    return pl.pallas_call(
        _in0_kernel,
        grid=(n // tm,),
        in_specs=[pl.BlockSpec((tm, d), lambda i: (i, 0)),
                  _const_spec((1, d)), _const_spec(wb.shape), _const_spec(ws.shape),
                  _const_spec(pm.shape), _const_spec(qkg.shape)],
        out_specs=[pl.BlockSpec((tm, nb), lambda i: (i, 0)),
                   pl.BlockSpec((tm, LANES), lambda i: (i, 0))],
        out_shape=[jax.ShapeDtypeStruct((n, nb), BF), jax.ShapeDtypeStruct((n, LANES), F32)],
        compiler_params=_cparams(("parallel",)),
        name="l0_in_proj",
    )(x2, gain, wb, ws, pm, qkg)


def _fox_cum_kernel(ff_ref, bias_ref, c_ref, *, blk):
    t = ff_ref.shape[0]
    r = lax.broadcasted_iota(jnp.int32, (blk, blk), 0)
    c = lax.broadcasted_iota(jnp.int32, (blk, blk), 1)
    tril = (r >= c).astype(BF)
    carry = jnp.zeros((1, ff_ref.shape[1]), F32)
    for i in range(t // blk):
        z = ff_ref[i * blk:(i + 1) * blk, :] + bias_ref[...]
        ls = jnp.minimum(z, 0.0) - jnp.log1p(jnp.exp(-jnp.abs(z)))
        cb = _mm_exact_lhs(tril, ls) + carry
        c_ref[i * blk:(i + 1) * blk, :] = cb
        carry = cb[blk - 1:blk, :]


def _fox_cum(ff_t, bias_t):
    t, l = ff_t.shape
    return pl.pallas_call(
        functools.partial(_fox_cum_kernel, blk=min(256, t)),
        out_shape=jax.ShapeDtypeStruct((t, l), F32),
        name="fox_cum_gate",
    )(ff_t, bias_t)


def _fox_kernel(q_ref, k_ref, v_ref, cc_ref, cr_ref, o_ref, *, tq):
    i = pl.program_id(2)
    lane = lax.broadcasted_iota(jnp.int32, (1, LANES), 1)
    q2 = q_ref[...]
    ri = lax.broadcasted_iota(jnp.int32, (tq, tq), 0)
    ci = lax.broadcasted_iota(jnp.int32, (tq, tq), 1)
    causal = ri >= ci
    outs = []
    for h in range(2):
        in_head = (lane >= h * FOX_HEAD_DIM) & (lane < (h + 1) * FOX_HEAD_DIM)
        qh = jnp.where(in_head, q2, jnp.zeros_like(q2))
        cq = cc_ref[0, h]

        def tile(j, carry, masked):
            m, l, acc = carry
            r0 = pl.multiple_of(j * tq, tq)
            kb = k_ref[pl.ds(r0, tq), :]
            vb = v_ref[pl.ds(r0, tq), :]
            ck = cr_ref[0, h, pl.ds(j, 1), :]
            s = lax.dot_general(qh, kb, (((1,), (1,)), ((), ())),
                                preferred_element_type=F32)
            s = s + (cq - ck)
            if masked:
                s = jnp.where(causal, s, -jnp.inf)
            m_new = jnp.maximum(m, jnp.max(s, axis=-1, keepdims=True))
            p = jnp.exp(s - m_new)
            alpha = jnp.exp(m - m_new)
            l = alpha * l + jnp.sum(p, axis=-1, keepdims=True)
            acc = alpha * acc + jnp.dot(p.astype(BF), vb, preferred_element_type=F32)
            return m_new, l, acc

        init = (jnp.full((tq, 1), -1e30, F32), jnp.zeros((tq, 1), F32),
                jnp.zeros((tq, LANES), F32))
        carry = lax.fori_loop(0, i, lambda j, c: tile(j, c, False), init)
        m, l, acc = tile(i, carry, True)
        outs.append(acc / l)
    o_ref[...] = jnp.where(lane < FOX_HEAD_DIM, outs[0], outs[1]).astype(o_ref.dtype)


def _fox_attention(zb, c_col, c_row, b, t, tq):
    n = zb.shape[0]
    nq = t // tq
    hp = FOX_HEADS // 2
    return pl.pallas_call(
        functools.partial(_fox_kernel, tq=tq),
        grid=(b, hp, nq),
        in_specs=[pl.BlockSpec((tq, LANES), lambda bi, p, i: (bi * nq + i, p)),
                  pl.BlockSpec((t, LANES), lambda bi, p, i: (bi, hp + p)),
                  pl.BlockSpec((t, LANES), lambda bi, p, i: (bi, 2 * hp + p)),
                  pl.BlockSpec((1, 2, tq, 1), lambda bi, p, i: (bi, p, i, 0)),
                  pl.BlockSpec((1, 2, nq, tq), lambda bi, p, i: (bi, p, 0, 0))],
        out_specs=pl.BlockSpec((tq, LANES), lambda bi, p, i: (bi * nq + i, p)),
        out_shape=jax.ShapeDtypeStruct((n, FOX_HEADS * FOX_HEAD_DIM), BF),
        compiler_params=_cparams(("parallel", "parallel", "arbitrary")),
        name="fox_attention",
    )(zb, zb, zb, c_col, c_row)


def _neumann_inverse(a, eye):
    p = eye - a
    ap = a
    k = 2
    while k < CHUNK:
        ap = _mm(ap, ap)
        p = p + _mm(p, ap)
        k *= 2
    return p


def _gdn_kernel(alog_ref, dtb_ref, q_ref, k_ref, v_ref, gg_ref, gar_ref, gac_ref, gbc_ref,
                cw_ref, on_ref, o_ref, qs, ks, vs, grow_s, gcol_s, bcol_s, s_s):
    t = q_ref.shape[0]
    nc = t // CHUNK
    hd = GDN_HEAD_DIM
    width = GDN_HEADS * hd

    row = lax.broadcasted_iota(jnp.int32, (t, hd), 0)
    for idx, (src, dst) in enumerate(((q_ref, qs), (k_ref, ks), (v_ref, vs))):
        for h in range(GDN_HEADS):
            col = idx * width + h * hd
            x = src[:, h * hd:(h + 1) * hd].astype(F32)
            y = x * cw_ref[GDN_CONV - 1:GDN_CONV, col:col + hd]
            for s in range(1, GDN_CONV):
                xs = jnp.where(row >= s, pltpu.roll(x, s, 0), 0.0)
                y = y + xs * cw_ref[GDN_CONV - 1 - s:GDN_CONV - s, col:col + hd]
            y = _silu(y)
            if idx < 2:
                y = y * lax.rsqrt(jnp.sum(y * y, axis=-1, keepdims=True) + NORM_EPS)
            if idx == 0:
                y = y * (hd ** -0.5)
            dst[:, h * hd:(h + 1) * hd] = y

    ri = lax.broadcasted_iota(jnp.int32, (CHUNK, CHUNK), 0)
    ci = lax.broadcasted_iota(jnp.int32, (CHUNK, CHUNK), 1)
    lower = ri >= ci
    strict = ri > ci
    tril = lower.astype(BF)
    triu = (ri <= ci).astype(BF)
    eye = (ri == ci).astype(F32)
    for h in range(GDN_HEADS):
        a = -jnp.exp(alog_ref[h])
        grow_s[h] = _mm_exact_rhs(a * _softplus(gar_ref[0, h] + dtb_ref[h]), triu)
        gcol_s[h] = _mm_exact_lhs(tril, a * _softplus(gac_ref[0, h] + dtb_ref[h]))
        bcol_s[h] = _sigmoid(gbc_ref[0, h])
    s_s[...] = jnp.zeros_like(s_s)

    lane_nc = lax.broadcasted_iota(jnp.int32, (CHUNK, nc), 1)

    def chunk_step(n, carry):
        r0 = pl.multiple_of(n * CHUNK, CHUNK)
        sel = lane_nc == n
        for h in range(GDN_HEADS):
            sl = slice(h * hd, (h + 1) * hd)
            q = qs[pl.ds(r0, CHUNK), sl]
            k = ks[pl.ds(r0, CHUNK), sl]
            v = vs[pl.ds(r0, CHUNK), sl]
            g_c = jnp.sum(jnp.where(sel, gcol_s[h], 0.0), axis=1, keepdims=True)
            b_c = jnp.sum(jnp.where(sel, bcol_s[h], 0.0), axis=1, keepdims=True)
            g_r = grow_s[h, pl.ds(n, 1), :]
            g_last = g_r[:, CHUNK - 1:CHUNK]
            decay = jnp.exp(jnp.where(lower, g_c - g_r, -jnp.inf))
            kb = k * b_c
            a_mat = jnp.where(strict, _mm_nt(kb, k) * decay, 0.0)
            qk = _mm_nt(q, k) * decay
            t_inv = _neumann_inverse(a_mat, eye)
            eg = jnp.exp(g_c)
            uw = _mm(t_inv, jnp.concatenate([v * b_c, kb * eg], axis=1))
            u = uw[:, :hd]
            w = uw[:, hd:]
            s_prev = s_s[h]
            v_new = u - _mm(w, s_prev)
            o = _mm(q * eg, s_prev) + _mm(qk, v_new)
            k_dec = k * jnp.exp(g_last - g_c)
            s_s[h] = s_prev * jnp.exp(g_last) + _mm_tn(k_dec, v_new)
            o = _rms(o, on_ref[...])
            gate = _silu(gg_ref[pl.ds(r0, CHUNK), sl].astype(F32))
            o_ref[pl.ds(r0, CHUNK), sl] = (o * gate).astype(o_ref.dtype)
        return carry

    lax.fori_loop(0, nc, chunk_step, 0)


def _gdn(zb, ga_row, ga_col, gb_col, conv_w, a_log, dt_bias, o_norm, b, t):
    n = zb.shape[0]
    nc = t // CHUNK
    width = GDN_HEADS * GDN_HEAD_DIM
    fox_blocks = 3 * FOX_HEADS * FOX_HEAD_DIM // width
    colspec = lambda j: pl.BlockSpec((t, width), lambda bi: (bi, fox_blocks + j))
    gspec = lambda shp: pl.BlockSpec((1,) + shp, lambda bi: (bi, 0, 0, 0))
    return pl.pallas_call(
        _gdn_kernel,
        grid=(b,),
        in_specs=[_const_spec(a_log.shape), _const_spec(dt_bias.shape),
                  colspec(0), colspec(1), colspec(2), colspec(3),
                  gspec((GDN_HEADS, nc, CHUNK)), gspec((GDN_HEADS, CHUNK, nc)),
                  gspec((GDN_HEADS, CHUNK, nc)),
                  _const_spec(conv_w.shape), _const_spec(o_norm.shape)],
        out_specs=pl.BlockSpec((t, width), lambda bi: (bi, 0)),
        out_shape=jax.ShapeDtypeStruct((n, width), BF),
        scratch_shapes=[pltpu.VMEM((t, width), F32), pltpu.VMEM((t, width), F32),
                        pltpu.VMEM((t, width), F32),
                        pltpu.VMEM((GDN_HEADS, nc, CHUNK), F32),
                        pltpu.VMEM((GDN_HEADS, CHUNK, nc), F32),
                        pltpu.VMEM((GDN_HEADS, CHUNK, nc), F32),
                        pltpu.VMEM((GDN_HEADS, GDN_HEAD_DIM, GDN_HEAD_DIM), F32)],
        compiler_params=_cparams(("parallel",)),
        name="gated_deltanet",
    )(a_log, dt_bias, zb, zb, zb, zb, ga_row, ga_col, gb_col, conv_w, o_norm)


def _mlp_kernel(a1_ref, a2_ref, x_ref, wo1_ref, wo2_ref, g_ref, w1_ref, w2_ref, o_ref, acc_ref,
                *, fc):
    mix = (jnp.dot(a1_ref[...], wo1_ref[...], preferred_element_type=F32)
           + jnp.dot(a2_ref[...], wo2_ref[...], preferred_element_type=F32))
    x1 = x_ref[...] + mix
    h = _rms(x1, g_ref[...]).astype(BF)
    acc_ref[...] = x1
    for c in range(w1_ref.shape[1] // fc):
        a = jnp.maximum(jnp.dot(h, w1_ref[:, c * fc:(c + 1) * fc],
                                preferred_element_type=F32), 0.0)
        acc_ref[...] += jnp.dot((a * a).astype(BF), w2_ref[c * fc:(c + 1) * fc, :],
                                preferred_element_type=F32)
    o_ref[...] = acc_ref[...]


def _mlp(a1, a2, a2_block, x2, wo1, wo2, gain, w1, w2, tm):
    n, d = x2.shape
    half = wo1.shape[0]
    return pl.pallas_call(
        functools.partial(_mlp_kernel, fc=1024),
        grid=(n // tm,),
        in_specs=[pl.BlockSpec((tm, half), lambda i: (i, 0)),
                  pl.BlockSpec((tm, half), lambda i: (i, a2_block)),
                  pl.BlockSpec((tm, d), lambda i: (i, 0)),
                  _const_spec(wo1.shape), _const_spec(wo2.shape), _const_spec(gain.shape),
                  _const_spec(w1.shape), _const_spec(w2.shape)],
        out_specs=pl.BlockSpec((tm, d), lambda i: (i, 0)),
        out_shape=jax.ShapeDtypeStruct((n, d), F32),
        scratch_shapes=[pltpu.VMEM((tm, d), F32)],
        compiler_params=_cparams(("parallel",)),
        name="out_proj_mlp",
    )(a1, a2, x2, wo1, wo2, gain, w1, w2)


def _in1_kernel(x_ref, g_ref, wb_ref, wf_ref, zb_ref, zf_ref):
    h = _rms(x_ref[...], g_ref[...]).astype(BF)
    gw = 512
    for c in range(wb_ref.shape[1] // gw):
        zb_ref[:, c * gw:(c + 1) * gw] = jnp.dot(
            h, wb_ref[:, c * gw:(c + 1) * gw], preferred_element_type=F32).astype(BF)
    for c in range(wf_ref.shape[1] // gw):
        zf_ref[:, c * gw:(c + 1) * gw] = jnp.dot(
            h, wf_ref[:, c * gw:(c + 1) * gw], preferred_element_type=F32)


def _in1(x2, gain, wb, wf, tm):
    n, d = x2.shape
    return pl.pallas_call(
        _in1_kernel,
        grid=(n // tm,),
        in_specs=[pl.BlockSpec((tm, d), lambda i: (i, 0)),
                  _const_spec(gain.shape), _const_spec(wb.shape), _const_spec(wf.shape)],
        out_specs=[pl.BlockSpec((tm, wb.shape[1]), lambda i: (i, 0)),
                   pl.BlockSpec((tm, wf.shape[1]), lambda i: (i, 0))],
        out_shape=[jax.ShapeDtypeStruct((n, wb.shape[1]), BF),
                   jax.ShapeDtypeStruct((n, wf.shape[1]), F32)],
        compiler_params=_cparams(("parallel",)),
        name="l1_in_proj",
    )(x2, gain, wb, wf)


HG_SUB = SUBLANES


def _hgrn_intra(q, k, b):
    nb = CHUNK // HG_SUB
    d = q.shape[1]
    q3 = q.reshape(nb, HG_SUB, d)
    k3 = k.reshape(nb, HG_SUB, d)
    b3 = b.reshape(nb, HG_SUB, d)
    row = lax.broadcasted_iota(jnp.int32, (CHUNK, CHUNK), 0)
    lane = lax.broadcasted_iota(jnp.int32, (CHUNK, CHUNK), 1)
    blk0 = (row // HG_SUB) * HG_SUB
    rin = row - blk0
    dmat = jnp.zeros((CHUNK, CHUNK), F32)
    for s in range(HG_SUB):
        e = jnp.exp(jnp.minimum(b3 - b3[:, s:s + 1, :], 0.0))
        r = jnp.sum(q3 * e * k3[:, s:s + 1, :], axis=-1, keepdims=True)
        r = r.reshape(CHUNK, 1)
        dmat = jnp.where((lane == blk0 + s) & (rin >= s), r, dmat)

    blk = lambda x, i: x[HG_SUB * i:HG_SUB * (i + 1)]
    zero = jnp.zeros((HG_SUB, d), F32)
    lhs, rhs = [], []
    for j in range(1, nb // 2):
        bref = b[2 * HG_SUB * j:2 * HG_SUB * j + 1, :]
        lg, rg = [zero] * nb, [zero] * nb
        for i in (2 * j, 2 * j + 1):
            lg[i] = blk(q, i) * jnp.exp(blk(b, i) - bref)
        for i in range(2 * j):
            rg[i] = blk(k, i) * jnp.exp(bref - blk(b, i))
        lhs.append(jnp.concatenate(lg, axis=0))
        rhs.append(jnp.concatenate(rg, axis=0))
    for j in range(nb // 2):
        bref = b[HG_SUB * (2 * j + 1):HG_SUB * (2 * j + 1) + 1, :]
        lg, rg = [zero] * nb, [zero] * nb
        lg[2 * j + 1] = blk(q, 2 * j + 1) * jnp.exp(blk(b, 2 * j + 1) - bref)
        rg[2 * j] = blk(k, 2 * j) * jnp.exp(bref - blk(b, 2 * j))
        lhs.append(jnp.concatenate(lg, axis=0))
        rhs.append(jnp.concatenate(rg, axis=0))
    off = _mm_nt(jnp.concatenate(lhs, axis=1), jnp.concatenate(rhs, axis=1))
    return dmat + off


def _hgrn_kernel(zq_ref, zi_ref, zg_ref, zf_ref, lbl_ref, on_ref, o_ref, st_ref):
    t = zq_ref.shape[0]
    nc = t // CHUNK
    ri = lax.broadcasted_iota(jnp.int32, (CHUNK, CHUNK), 0)
    ci = lax.broadcasted_iota(jnp.int32, (CHUNK, CHUNK), 1)
    tril = (ri >= ci).astype(BF)

    l0 = lbl_ref[0:1, :]
    l1 = lbl_ref[1:2, :]
    mx = jnp.maximum(l0, l1)
    e0 = jnp.exp(l0 - mx)
    e1 = jnp.exp(l1 - mx)
    s0 = e0 / (e0 + e1)
    s1 = e1 / (e0 + e1)
    lb = (s0 + s1) - s0

    st_ref[...] = jnp.zeros_like(st_ref)

    def chunk_step(n, carry):
        r0 = pl.multiple_of(n * CHUNK, CHUNK)
        rows = pl.ds(r0, CHUNK)
        q = _silu(zq_ref[rows, :].astype(F32))
        f = lb + (1.0 - lb) * _sigmoid(zf_ref[rows, :])
        k = 1.0 - f
        v = zi_ref[rows, :]
        b = _mm_exact_lhs(tril, jnp.log(f))
        b_last = b[CHUNK - 1:CHUNK, :]
        st = st_ref[...]
        o = _mm_nt(q * jnp.exp(b), st) + _mm(_hgrn_intra(q, k, b), v)
        st_ref[...] = st * jnp.exp(b_last) + _mm_tn(v, k * jnp.exp(b_last - b))
        o = _rms(o, on_ref[...]) * _silu(zg_ref[rows, :].astype(F32))
        o_ref[rows, :] = o.astype(o_ref.dtype)
        return carry

    lax.fori_loop(0, nc, chunk_step, 0)


def _hgrn(zb, zf, lb_logits, o_norm, b, t):
    n = zb.shape[0]
    hd = HGRN_HEAD_DIM
    nh = HGRN_HEADS
    return pl.pallas_call(
        _hgrn_kernel,
        grid=(b, nh),
        in_specs=[pl.BlockSpec((t, hd), lambda bi, h: (bi, h)),
                  pl.BlockSpec((t, hd), lambda bi, h: (bi, nh + h)),
                  pl.BlockSpec((t, hd), lambda bi, h: (bi, 2 * nh + h)),
                  pl.BlockSpec((t, hd), lambda bi, h: (bi, h)),
                  pl.BlockSpec((lb_logits.shape[0], hd), lambda bi, h: (0, h)),
                  _const_spec(o_norm.shape)],
        out_specs=pl.BlockSpec((t, hd), lambda bi, h: (bi, h)),
        out_shape=jax.ShapeDtypeStruct((n, nh * hd), BF),
        scratch_shapes=[pltpu.VMEM((hd, hd), F32)],
        compiler_params=_cparams(("parallel", "parallel")),
        name="hgrn2",
    )(zb, zb, zb, zf, lb_logits, o_norm)


def kernel(x, l0_mix_norm, l0_w_in, l0_fox_q_norm, l0_fox_k_norm, l0_fox_f_bias, l0_gdn_conv,
           l0_gdn_A_log, l0_gdn_dt_bias, l0_gdn_o_norm, l0_w_out, l0_ffn_norm, l0_w_ff1,
           l0_w_ff2, l1_mix_norm, l1_w_in, l1_hgrn_o_norm, l1_w_out, l1_ffn_norm, l1_w_ff1,
           l1_w_ff2, hgrn_lb_logits):
    b, t, d = x.shape
    n = b * t
    fw = FOX_HEADS * FOX_HEAD_DIM
    gw = GDN_HEADS * GDN_HEAD_DIM
    assert hgrn_lb_logits.shape[0] == 2 and t % 256 == 0 and d == HGRN_HEADS * HGRN_HEAD_DIM
    tm = 512 if n % 512 == 0 else 256
    tq = 256
    nc = t // CHUNK
    row = lambda p: p.reshape(1, -1).astype(F32)
    x2 = x.reshape(n, d)

    o_ff = 3 * fw
    o_gq = o_ff + FOX_HEADS
    o_gb = o_gq + 3 * gw
    o_gg = o_gb + 2 * GDN_HEADS
    wb0 = jnp.concatenate([l0_w_in[:, :o_ff], l0_w_in[:, o_gq:o_gb], l0_w_in[:, o_gg:]],
                          axis=1).astype(BF)
    n_small = FOX_HEADS + 2 * GDN_HEADS
    ws0 = jnp.concatenate([l0_w_in[:, o_ff:o_gq], l0_w_in[:, o_gb:o_gg],
                           jnp.zeros((d, LANES - n_small), F32)], axis=1).astype(BF)
    hid = jnp.arange(fw) // FOX_HEAD_DIM
    pm = jnp.where(hid[:, None] == hid[None, :], 1.0 / FOX_HEAD_DIM, 0.0).astype(BF)
    qkg = jnp.stack([jnp.tile(l0_fox_q_norm, FOX_HEADS) * FOX_HEAD_DIM ** -0.5,
                     jnp.tile(l0_fox_k_norm, FOX_HEADS)]).astype(F32)

    zb0, zs0 = _in0(x2, row(l0_mix_norm), wb0, ws0, pm, qkg, tm)

    ff_t = zs0[:, :FOX_HEADS].reshape(b, t, FOX_HEADS).transpose(1, 0, 2).reshape(t, b * FOX_HEADS)
    c = _fox_cum(ff_t, jnp.tile(l0_fox_f_bias, b).reshape(1, -1).astype(F32))
    c = c.reshape(t, b, FOX_HEADS).transpose(1, 2, 0)
    fox_o = _fox_attention(zb0, c[..., None], c.reshape(b, FOX_HEADS, t // tq, tq), b, t, tq)

    def gate_rows(lo):
        g = zs0[:, lo:lo + GDN_HEADS].reshape(b, t, GDN_HEADS).transpose(0, 2, 1)
        return g.reshape(b, GDN_HEADS, nc, CHUNK)
    gb_row = gate_rows(FOX_HEADS)
    ga_row = gate_rows(FOX_HEADS + GDN_HEADS)
    gdn_o = _gdn(zb0, ga_row, ga_row.swapaxes(-1, -2), gb_row.swapaxes(-1, -2),
                 l0_gdn_conv.astype(F32), l0_gdn_A_log.reshape(GDN_HEADS, 1, 1).astype(F32),
                 l0_gdn_dt_bias.reshape(GDN_HEADS, 1, 1).astype(F32), row(l0_gdn_o_norm), b, t)

    wo0 = l0_w_out.astype(BF)
    x1 = _mlp(fox_o, gdn_o, 0, x2, wo0[:fw], wo0[fw:], row(l0_ffn_norm),
              l0_w_ff1.astype(BF), l0_w_ff2.astype(BF), tm)

    wb1 = jnp.concatenate([l1_w_in[:, :d], l1_w_in[:, 2 * d:]], axis=1).astype(BF)
    wf1 = l1_w_in[:, d:2 * d].astype(BF)
    zb1, zf1 = _in1(x1, row(l1_mix_norm), wb1, wf1, tm)
    ho = _hgrn(zb1, zf1, hgrn_lb_logits.astype(F32), row(l1_hgrn_o_norm), b, t)
    wo1 = l1_w_out.astype(BF)
    half = d // 2
    x3 = _mlp(ho, ho, 1, x1, wo1[:half], wo1[half:], row(l1_ffn_norm),
              l1_w_ff1.astype(BF), l1_w_ff2.astype(BF), tm)
    return x3.reshape(b, t, d)
```

```python
import functools

import jax
import jax.numpy as jnp
from jax import lax
from jax.experimental import pallas as pl
from jax.experimental.pallas import tpu as pltpu

BF = jnp.bfloat16
F32 = jnp.float32
NORM_EPS = 1e-6
CHUNK = 64
FOX_HEADS, FOX_HEAD_DIM = 8, 64
GDN_HEADS, GDN_HEAD_DIM = 4, 128
HGRN_HEADS, HGRN_HEAD_DIM = 8, 128
GDN_CONV = 4
LANES = 128
SUBLANES = 8
VMEM_LIMIT = 56 * 1024 * 1024


def _cparams(sem):
    return pltpu.CompilerParams(dimension_semantics=sem, vmem_limit_bytes=VMEM_LIMIT)


def _const_spec(shape):
    nd = len(shape)
    return pl.BlockSpec(shape, lambda *_: (0,) * nd, pipeline_mode=pl.Buffered(1))


def _mm(a, b):
    return jnp.dot(a.astype(BF), b.astype(BF), preferred_element_type=F32)


def _mm_nt(a, b):
    return lax.dot_general(a.astype(BF), b.astype(BF), (((1,), (1,)), ((), ())),
                           preferred_element_type=F32)


def _mm_tn(a, b):
    return lax.dot_general(a.astype(BF), b.astype(BF), (((0,), (0,)), ((), ())),
                           preferred_element_type=F32)


def _split3(x):
    hi = x.astype(BF)
    r1 = x - hi.astype(F32)
    mid = r1.astype(BF)
    lo = (r1 - mid.astype(F32)).astype(BF)
    return hi, mid, lo


def _mm_exact_lhs(m_bf, x):
    hi, mid, lo = _split3(x)
    d = lambda p: jnp.dot(m_bf, p, preferred_element_type=F32)
    return d(hi) + d(mid) + d(lo)


def _mm_exact_rhs(x, m_bf):
    hi, mid, lo = _split3(x)
    d = lambda p: jnp.dot(p, m_bf, preferred_element_type=F32)
    return d(hi) + d(mid) + d(lo)


def _sigmoid(x):
    return 1.0 / (1.0 + jnp.exp(-x))


def _silu(x):
    return x * _sigmoid(x)


def _softplus(x):
    return jnp.maximum(x, 0.0) + jnp.log1p(jnp.exp(-jnp.abs(x)))


def _rms(x, gain):
    return x * lax.rsqrt(jnp.mean(x * x, axis=-1, keepdims=True) + NORM_EPS) * gain


def _in0_kernel(x_ref, g_ref, wb_ref, ws_ref, pm_ref, qkg_ref, zb_ref, zs_ref):
    h = _rms(x_ref[...], g_ref[...]).astype(BF)
    gw = 512
    for c in range(wb_ref.shape[1] // gw):
        z = jnp.dot(h, wb_ref[:, c * gw:(c + 1) * gw], preferred_element_type=F32)
        if c < 2:
            ms = jnp.dot((z * z).astype(BF), pm_ref[...], preferred_element_type=F32)
            z = z * lax.rsqrt(ms + NORM_EPS) * qkg_ref[c:c + 1, :]
        zb_ref[:, c * gw:(c + 1) * gw] = z.astype(BF)
    zs_ref[...] = jnp.dot(h, ws_ref[...], preferred_element_type=F32)


def _in0(x2, gain, wb, ws, pm, qkg, tm):
    n, d = x2.shape
    nb = wb.shape[1]
    return pl.pallas_call(
        _in0_kernel,
        grid=(n // tm,),
        in_specs=[pl.BlockSpec((tm, d), lambda i: (i, 0)),
                  _const_spec((1, d)), _const_spec(wb.shape), _const_spec(ws.shape),
                  _const_spec(pm.shape), _const_spec(qkg.shape)],
        out_specs=[pl.BlockSpec((tm, nb), lambda i: (i, 0)),
                   pl.BlockSpec((tm, LANES), lambda i: (i, 0))],
        out_shape=[jax.ShapeDtypeStruct((n, nb), BF), jax.ShapeDtypeStruct((n, LANES), F32)],
        compiler_params=_cparams(("parallel",)),
        name="l0_in_proj",
    )(x2, gain, wb, ws, pm, qkg)


def _fox_cum_kernel(ff_ref, bias_ref, c_ref, *, blk):
    t = ff_ref.shape[0]
    r = lax.broadcasted_iota(jnp.int32, (blk, blk), 0)
    c = lax.broadcasted_iota(jnp.int32, (blk, blk), 1)
    tril = (r >= c).astype(BF)
    carry = jnp.zeros((1, ff_ref.shape[1]), F32)
    for i in range(t // blk):
        z = ff_ref[i * blk:(i + 1) * blk, :] + bias_ref[...]
        ls = jnp.minimum(z, 0.0) - jnp.log1p(jnp.exp(-jnp.abs(z)))
        cb = _mm_exact_lhs(tril, ls) + carry
        c_ref[i * blk:(i + 1) * blk, :] = cb
        carry = cb[blk - 1:blk, :]


def _fox_cum(ff_t, bias_t):
    t, l = ff_t.shape
    return pl.pallas_call(
        functools.partial(_fox_cum_kernel, blk=min(256, t)),
        out_shape=jax.ShapeDtypeStruct((t, l), F32),
        name="fox_cum_gate",
    )(ff_t, bias_t)


def _fox_kernel(q_ref, k_ref, v_ref, cc_ref, cr_ref, o_ref, *, tq, tk):
    i = pl.program_id(2)
    lane = lax.broadcasted_iota(jnp.int32, (1, LANES), 1)
    q2 = q_ref[...]
    zero = jnp.zeros_like(q2)
    first = lane < FOX_HEAD_DIM
    qs = jnp.concatenate([jnp.where(first, q2, zero), jnp.where(first, zero, q2)], axis=0)
    cq = (cc_ref[0, 0], cc_ref[0, 1])
    ones = jnp.ones((tk, LANES), BF)
    row_g = i * tq + lax.broadcasted_iota(jnp.int32, (tq, tk), 0)
    col_l = lax.broadcasted_iota(jnp.int32, (tq, tk), 1)

    def tile(j, carry, masked):
        m, acc = carry
        r0 = pl.multiple_of(j * tk, tk)
        kb = k_ref[pl.ds(r0, tk), :]
        va = jnp.concatenate([v_ref[pl.ds(r0, tk), :], ones], axis=1)
        s = lax.dot_general(qs, kb, (((1,), (1,)), ((), ())), preferred_element_type=F32)
        parts = []
        for h in range(2):
            sh = s[h * tq:(h + 1) * tq] + (cq[h] - cr_ref[0, h, pl.ds(j, 1), :])
            if masked:
                sh = jnp.where(j * tk + col_l <= row_g, sh, -jnp.inf)
            parts.append(sh)
        s = jnp.concatenate(parts, axis=0)
        m_new = jnp.maximum(m, jnp.max(s, axis=-1, keepdims=True))
        p = jnp.exp(s - m_new)
        acc = jnp.exp(m - m_new) * acc + jnp.dot(p.astype(BF), va, preferred_element_type=F32)
        return m_new, acc

    init = (jnp.full((2 * tq, 1), -1e30, F32), jnp.zeros((2 * tq, 2 * LANES), F32))
    n_full = (i * tq) // tk
    carry = lax.fori_loop(0, n_full, lambda j, c: tile(j, c, False), init)
    _, acc = tile(n_full, carry, True)
    o = acc[:, :LANES] / acc[:, LANES:]
    o_ref[...] = jnp.where(first, o[:tq], o[tq:]).astype(o_ref.dtype)


def _fox_attention(zb, c_col, c_row, b, t, tq, tk):
    n = zb.shape[0]
    nq = t // tq
    hp = FOX_HEADS // 2
    return pl.pallas_call(
        functools.partial(_fox_kernel, tq=tq, tk=tk),
        grid=(b, hp, nq),
        in_specs=[pl.BlockSpec((tq, LANES), lambda bi, p, i: (bi * nq + i, p)),
                  pl.BlockSpec((t, LANES), lambda bi, p, i: (bi, hp + p)),
                  pl.BlockSpec((t, LANES), lambda bi, p, i: (bi, 2 * hp + p)),
                  pl.BlockSpec((1, 2, tq, 1), lambda bi, p, i: (bi, p, i, 0)),
                  pl.BlockSpec((1, 2, t // tk, tk), lambda bi, p, i: (bi, p, 0, 0))],
        out_specs=pl.BlockSpec((tq, LANES), lambda bi, p, i: (bi * nq + i, p)),
        out_shape=jax.ShapeDtypeStruct((n, FOX_HEADS * FOX_HEAD_DIM), BF),
        compiler_params=_cparams(("parallel", "parallel", "arbitrary")),
        name="fox_attention",
    )(zb, zb, zb, c_col, c_row)


def _neumann_inverse(mats, eye):
    ps = [eye - a for a in mats]
    aps = list(mats)
    k = 2
    while k < CHUNK:
        aps = [_mm(ap, ap) for ap in aps]
        ps = [p + _mm(p, ap) for p, ap in zip(ps, aps)]
        k *= 2
    return ps


def _gdn_kernel(alog_ref, dtb_ref, q_ref, k_ref, v_ref, gg_ref, gar_ref, gac_ref, gbc_ref,
                cw_ref, on_ref, o_ref, qs, ks, vs, kd_s, qk_s, grow_s, gcol_s, bcol_s, s_s):
    t = q_ref.shape[0]
    nc = t // CHUNK
    hd = GDN_HEAD_DIM
    width = GDN_HEADS * hd

    row = lax.broadcasted_iota(jnp.int32, (t, hd), 0)
    for idx, (src, dst) in enumerate(((q_ref, qs), (k_ref, ks), (v_ref, vs))):
        for h in range(GDN_HEADS):
            col = idx * width + h * hd
            x = src[:, h * hd:(h + 1) * hd].astype(F32)
            y = x * cw_ref[GDN_CONV - 1:GDN_CONV, col:col + hd]
            for s in range(1, GDN_CONV):
                xs = jnp.where(row >= s, pltpu.roll(x, s, 0), 0.0)
                y = y + xs * cw_ref[GDN_CONV - 1 - s:GDN_CONV - s, col:col + hd]
            y = _silu(y)
            if idx < 2:
                y = y * lax.rsqrt(jnp.sum(y * y, axis=-1, keepdims=True) + NORM_EPS)
            if idx == 0:
                y = y * (hd ** -0.5)
            dst[:, h * hd:(h + 1) * hd] = y

    ri = lax.broadcasted_iota(jnp.int32, (CHUNK, CHUNK), 0)
    ci = lax.broadcasted_iota(jnp.int32, (CHUNK, CHUNK), 1)
    lower = ri >= ci
    strict = ri > ci
    tril = lower.astype(BF)
    triu = (ri <= ci).astype(BF)
    eye = (ri == ci).astype(F32)
    for h in range(GDN_HEADS):
        a = -jnp.exp(alog_ref[h])
        grow_s[h] = _mm_exact_rhs(a * _softplus(gar_ref[0, h] + dtb_ref[h]), triu)
        gcol_s[h] = _mm_exact_lhs(tril, a * _softplus(gac_ref[0, h] + dtb_ref[h]))
        bcol_s[h] = _sigmoid(gbc_ref[0, h])
    s_s[...] = jnp.zeros_like(s_s)

    lane_nc = lax.broadcasted_iota(jnp.int32, (CHUNK, nc), 1)
    heads = range(GDN_HEADS)
    cat = lambda parts: jnp.concatenate(parts, axis=1)

    per_iter = 2

    def local_step(it, carry):
        r0 = pl.multiple_of(it * (per_iter * CHUNK), per_iter * CHUNK)
        rows = pl.ds(r0, per_iter * CHUNK)
        chains = [(c, h) for c in range(per_iter) for h in heads]
        csl = lambda c: slice(c * CHUNK, (c + 1) * CHUNK)
        hsl = lambda h: slice(h * hd, (h + 1) * hd)
        q_all, k_all, v_all = qs[rows, :], ks[rows, :], vs[rows, :]
        q = [q_all[csl(c), hsl(h)] for c, h in chains]
        k = [k_all[csl(c), hsl(h)] for c, h in chains]
        v = [v_all[csl(c), hsl(h)] for c, h in chains]
        sel = [lane_nc == it * per_iter + c for c in range(per_iter)]
        g_c = [jnp.sum(jnp.where(sel[c], gcol_s[h], 0.0), axis=1, keepdims=True)
               for c, h in chains]
        b_c = [jnp.sum(jnp.where(sel[c], bcol_s[h], 0.0), axis=1, keepdims=True)
               for c, h in chains]
        g_r = [grow_s[h, pl.ds(it * per_iter + c, 1), :] for c, h in chains]
        decay = [jnp.exp(jnp.where(lower, gc - gr, -jnp.inf)) for gc, gr in zip(g_c, g_r)]
        kb = [x * bc for x, bc in zip(k, b_c)]
        kk = [_mm_nt(x, y) for x, y in zip(kb, k)]
        qk = [_mm_nt(x, y) for x, y in zip(q, k)]
        t_inv = _neumann_inverse([jnp.where(strict, x * d, 0.0) for x, d in zip(kk, decay)], eye)
        eg = [jnp.exp(gc) for gc in g_c]
        uw = [_mm(ti, jnp.concatenate([x * bc, y * e], axis=1))
              for ti, x, bc, y, e in zip(t_inv, v, b_c, kb, eg)]
        kd = [x * jnp.exp(gr[:, CHUNK - 1:CHUNK] - gc) for x, gr, gc in zip(k, g_r, g_c)]
        grid = lambda parts: jnp.concatenate(
            [cat(parts[c * GDN_HEADS:(c + 1) * GDN_HEADS]) for c in range(per_iter)], axis=0)
        qs[rows, :] = grid([x * e for x, e in zip(q, eg)])
        ks[rows, :] = grid([x[:, hd:] for x in uw])
        vs[rows, :] = grid([x[:, :hd] for x in uw])
        kd_s[rows, :] = grid(kd).astype(BF)
        qk_s[rows, :] = grid([x * d for x, d in zip(qk, decay)]).astype(BF)
        return carry

    lax.fori_loop(0, nc // per_iter, local_step, 0)

    def state_step(n, carry):
        r0 = pl.multiple_of(n * CHUNK, CHUNK)
        rows = pl.ds(r0, CHUNK)
        hsl = lambda h: slice(h * hd, (h + 1) * hd)
        s_prev = [s_s[h] for h in heads]
        sb = [x.astype(BF) for x in s_prev]
        ws_ = [_mm(ks[rows, hsl(h)], sb[h]) for h in heads]
        qs_ = [_mm(qs[rows, hsl(h)], sb[h]) for h in heads]
        v_new = [vs[rows, hsl(h)] - ws_[h] for h in heads]
        upd = [_mm_tn(kd_s[rows, hsl(h)], v_new[h]) for h in heads]
        qv = [_mm(qk_s[rows, h * CHUNK:(h + 1) * CHUNK], v_new[h]) for h in heads]
        g_last = [grow_s[h, pl.ds(n, 1), :][:, CHUNK - 1:CHUNK] for h in heads]
        s_s[...] = jnp.stack([s_prev[h] * jnp.exp(g_last[h]) + upd[h] for h in heads])
        outs = [_rms(qs_[h] + qv[h], on_ref[...]) for h in heads]
        gate = _silu(gg_ref[rows, :].astype(F32))
        o_ref[rows, :] = (cat(outs) * gate).astype(o_ref.dtype)
        return carry

    lax.fori_loop(0, nc, state_step, 0)


def _gdn(zb, ga_row, ga_col, gb_col, conv_w, a_log, dt_bias, o_norm, b, t):
    n = zb.shape[0]
    nc = t // CHUNK
    width = GDN_HEADS * GDN_HEAD_DIM
    fox_blocks = 3 * FOX_HEADS * FOX_HEAD_DIM // width
    colspec = lambda j: pl.BlockSpec((t, width), lambda bi: (bi, fox_blocks + j))
    gspec = lambda shp: pl.BlockSpec((1,) + shp, lambda bi: (bi, 0, 0, 0))
    return pl.pallas_call(
        _gdn_kernel,
        grid=(b,),
        in_specs=[_const_spec(a_log.shape), _const_spec(dt_bias.shape),
                  colspec(0), colspec(1), colspec(2), colspec(3),
                  gspec((GDN_HEADS, nc, CHUNK)), gspec((GDN_HEADS, CHUNK, nc)),
                  gspec((GDN_HEADS, CHUNK, nc)),
                  _const_spec(conv_w.shape), _const_spec(o_norm.shape)],
        out_specs=pl.BlockSpec((t, width), lambda bi: (bi, 0)),
        out_shape=jax.ShapeDtypeStruct((n, width), BF),
        scratch_shapes=[pltpu.VMEM((t, width), F32), pltpu.VMEM((t, width), F32),
                        pltpu.VMEM((t, width), F32),
                        pltpu.VMEM((t, width), BF), pltpu.VMEM((t, GDN_HEADS * CHUNK), BF),
                        pltpu.VMEM((GDN_HEADS, nc, CHUNK), F32),
                        pltpu.VMEM((GDN_HEADS, CHUNK, nc), F32),
                        pltpu.VMEM((GDN_HEADS, CHUNK, nc), F32),
                        pltpu.VMEM((GDN_HEADS, GDN_HEAD_DIM, GDN_HEAD_DIM), F32)],
        compiler_params=_cparams(("parallel",)),
        name="gated_deltanet",
    )(a_log, dt_bias, zb, zb, zb, zb, ga_row, ga_col, gb_col, conv_w, o_norm)


def _mlp_kernel(a1_ref, a2_ref, x_ref, wo1_ref, wo2_ref, g_ref, w1_ref, w2_ref, o_ref, acc_ref,
                *, fc):
    mix = (jnp.dot(a1_ref[...], wo1_ref[...], preferred_element_type=F32)
           + jnp.dot(a2_ref[...], wo2_ref[...], preferred_element_type=F32))
    x1 = x_ref[...] + mix
    h = _rms(x1, g_ref[...]).astype(BF)
    acc_ref[...] = x1
    for c in range(w1_ref.shape[1] // fc):
        a = jnp.maximum(jnp.dot(h, w1_ref[:, c * fc:(c + 1) * fc],
                                preferred_element_type=F32), 0.0)
        acc_ref[...] += jnp.dot((a * a).astype(BF), w2_ref[c * fc:(c + 1) * fc, :],
                                preferred_element_type=F32)
    o_ref[...] = acc_ref[...]


def _mlp(a1, a2, a2_block, x2, wo1, wo2, gain, w1, w2, tm):
    n, d = x2.shape
    half = wo1.shape[0]
    return pl.pallas_call(
        functools.partial(_mlp_kernel, fc=1024),
        grid=(n // tm,),
        in_specs=[pl.BlockSpec((tm, half), lambda i: (i, 0)),
                  pl.BlockSpec((tm, half), lambda i: (i, a2_block)),
                  pl.BlockSpec((tm, d), lambda i: (i, 0)),
                  _const_spec(wo1.shape), _const_spec(wo2.shape), _const_spec(gain.shape),
                  _const_spec(w1.shape), _const_spec(w2.shape)],
        out_specs=pl.BlockSpec((tm, d), lambda i: (i, 0)),
        out_shape=jax.ShapeDtypeStruct((n, d), F32),
        scratch_shapes=[pltpu.VMEM((tm, d), F32)],
        compiler_params=_cparams(("parallel",)),
        name="out_proj_mlp",
    )(a1, a2, x2, wo1, wo2, gain, w1, w2)


def _in1_kernel(x_ref, g_ref, wb_ref, wf_ref, zb_ref, zf_ref):
    h = _rms(x_ref[...], g_ref[...]).astype(BF)
    gw = 512
    for c in range(wb_ref.shape[1] // gw):
        zb_ref[:, c * gw:(c + 1) * gw] = jnp.dot(
            h, wb_ref[:, c * gw:(c + 1) * gw], preferred_element_type=F32).astype(BF)
    for c in range(wf_ref.shape[1] // gw):
        zf_ref[:, c * gw:(c + 1) * gw] = jnp.dot(
            h, wf_ref[:, c * gw:(c + 1) * gw], preferred_element_type=F32)


def _in1(x2, gain, wb, wf, tm):
    n, d = x2.shape
    return pl.pallas_call(
        _in1_kernel,
        grid=(n // tm,),
        in_specs=[pl.BlockSpec((tm, d), lambda i: (i, 0)),
                  _const_spec(gain.shape), _const_spec(wb.shape), _const_spec(wf.shape)],
        out_specs=[pl.BlockSpec((tm, wb.shape[1]), lambda i: (i, 0)),
                   pl.BlockSpec((tm, wf.shape[1]), lambda i: (i, 0))],
        out_shape=[jax.ShapeDtypeStruct((n, wb.shape[1]), BF),
                   jax.ShapeDtypeStruct((n, wf.shape[1]), F32)],
        compiler_params=_cparams(("parallel",)),
        name="l1_in_proj",
    )(x2, gain, wb, wf)


HG_SUB = SUBLANES


def _hgrn_intra(qs, ks, bs):
    nb = CHUNK // HG_SUB
    d = qs[0].shape[1]
    row = lax.broadcasted_iota(jnp.int32, (CHUNK, CHUNK), 0)
    lane = lax.broadcasted_iota(jnp.int32, (CHUNK, CHUNK), 1)
    blk0 = (row // HG_SUB) * HG_SUB
    rin = row - blk0
    r3 = lambda x: x.reshape(nb, HG_SUB, d)
    q3, k3, b3 = [r3(x) for x in qs], [r3(x) for x in ks], [r3(x) for x in bs]
    dmats = [jnp.zeros((CHUNK, CHUNK), F32) for _ in qs]
    for s in range(HG_SUB):
        take = (lane == blk0 + s) & (rin >= s)
        for c in range(len(qs)):
            e = jnp.exp(b3[c] - b3[c][:, s:s + 1, :])
            r = jnp.sum(q3[c] * e * k3[c][:, s:s + 1, :], axis=-1, keepdims=True)
            dmats[c] = jnp.where(take, r.reshape(CHUNK, 1), dmats[c])

    blk = lambda x, i: x[HG_SUB * i:HG_SUB * (i + 1)]
    zero = jnp.zeros((HG_SUB, d), F32)
    offs = []
    for q, k, b in zip(qs, ks, bs):
        lhs, rhs = [], []
        for j in range(1, nb // 2):
            bref = b[2 * HG_SUB * j:2 * HG_SUB * j + 1, :]
            lg, rg = [zero] * nb, [zero] * nb
            for i in (2 * j, 2 * j + 1):
                lg[i] = blk(q, i) * jnp.exp(blk(b, i) - bref)
            for i in range(2 * j):
                rg[i] = blk(k, i) * jnp.exp(bref - blk(b, i))
            lhs.append(jnp.concatenate(lg, axis=0))
            rhs.append(jnp.concatenate(rg, axis=0))
        for j in range(nb // 2):
            bref = b[HG_SUB * (2 * j + 1):HG_SUB * (2 * j + 1) + 1, :]
            lg, rg = [zero] * nb, [zero] * nb
            lg[2 * j + 1] = blk(q, 2 * j + 1) * jnp.exp(blk(b, 2 * j + 1) - bref)
            rg[2 * j] = blk(k, 2 * j) * jnp.exp(bref - blk(b, 2 * j))
            lhs.append(jnp.concatenate(lg, axis=0))
            rhs.append(jnp.concatenate(rg, axis=0))
        offs.append((jnp.concatenate(lhs, axis=1).astype(BF),
                     jnp.concatenate(rhs, axis=1).astype(BF)))
    return [dm + _mm_nt(l, r) for dm, (l, r) in zip(dmats, offs)]


def _hgrn_kernel(zq_ref, zi_ref, zg_ref, zf_ref, lbl_ref, on_ref, o_ref,
                 oi_s, qd_s, kv_s, dec_s, *, group):
    t = zq_ref.shape[0]
    gr = group * CHUNK
    ri = lax.broadcasted_iota(jnp.int32, (gr, gr), 0)
    ci = lax.broadcasted_iota(jnp.int32, (gr, gr), 1)
    tril = ((ri >= ci) & (ri // CHUNK == ci // CHUNK)).astype(BF)

    l0 = lbl_ref[0:1, :]
    l1 = lbl_ref[1:2, :]
    mx = jnp.maximum(l0, l1)
    e0 = jnp.exp(l0 - mx)
    e1 = jnp.exp(l1 - mx)
    s0 = e0 / (e0 + e1)
    s1 = e1 / (e0 + e1)
    lb = (s0 + s1) - s0

    def local_step(g, carry):
        rows = pl.ds(pl.multiple_of(g * gr, gr), gr)
        q = _silu(zq_ref[rows, :].astype(F32))
        f = lb + (1.0 - lb) * _sigmoid(zf_ref[rows, :])
        k = 1.0 - f
        v = zi_ref[rows, :]
        b = _mm_exact_lhs(tril, jnp.log(f))
        sls = [slice(c * CHUNK, (c + 1) * CHUNK) for c in range(group)]
        bc = [b[sl] for sl in sls]
        b_last = [x[CHUNK - 1:CHUNK, :] for x in bc]
        kdec = [k[sl] * jnp.exp(bl - x) for sl, bl, x in zip(sls, b_last, bc)]
        kv = [_mm_tn(v[sl], kd) for sl, kd in zip(sls, kdec)]
        dmat = _hgrn_intra([q[sl] for sl in sls], [k[sl] for sl in sls], bc)
        oi = [_mm(dm, v[sl]) for dm, sl in zip(dmat, sls)]
        dec = [jnp.exp(bl) for bl in b_last]
        oi_s[rows, :] = jnp.concatenate(oi, axis=0)
        qd_s[rows, :] = (q * jnp.exp(b)).astype(BF)
        kv_s[pl.ds(g * group, group)] = jnp.stack(kv)
        dec_s[pl.ds(g * group, group)] = jnp.stack(dec)
        return carry

    lax.fori_loop(0, t // gr, local_step, 0)

    def state_step(g, st):
        r0 = pl.multiple_of(g * gr, gr)
        outs = []
        for c in range(group):
            n = g * group + c
            rows = pl.ds(r0 + c * CHUNK, CHUNK)
            outs.append(oi_s[rows, :] + _mm_nt(qd_s[rows, :], st))
            st = st * dec_s[n] + kv_s[n]
        rows = pl.ds(r0, gr)
        o = _rms(jnp.concatenate(outs, axis=0), on_ref[...])
        o_ref[rows, :] = (o * _silu(zg_ref[rows, :].astype(F32))).astype(o_ref.dtype)
        return st

    d = zq_ref.shape[1]
    lax.fori_loop(0, t // gr, state_step, jnp.zeros((d, d), F32))


def _hgrn(zb, zf, lb_logits, o_norm, b, t):
    n = zb.shape[0]
    hd = HGRN_HEAD_DIM
    nh = HGRN_HEADS
    nc = t // CHUNK
    return pl.pallas_call(
        functools.partial(_hgrn_kernel, group=4),
        grid=(b, nh),
        in_specs=[pl.BlockSpec((t, hd), lambda bi, h: (bi, h)),
                  pl.BlockSpec((t, hd), lambda bi, h: (bi, nh + h)),
                  pl.BlockSpec((t, hd), lambda bi, h: (bi, 2 * nh + h)),
                  pl.BlockSpec((t, hd), lambda bi, h: (bi, h)),
                  pl.BlockSpec((lb_logits.shape[0], hd), lambda bi, h: (0, h)),
                  _const_spec(o_norm.shape)],
        out_specs=pl.BlockSpec((t, hd), lambda bi, h: (bi, h)),
        out_shape=jax.ShapeDtypeStruct((n, nh * hd), BF),
        scratch_shapes=[pltpu.VMEM((t, hd), F32), pltpu.VMEM((t, hd), BF),
                        pltpu.VMEM((nc, hd, hd), F32), pltpu.VMEM((nc, 1, hd), F32)],
        compiler_params=_cparams(("parallel", "parallel")),
        name="hgrn2",
    )(zb, zb, zb, zf, lb_logits, o_norm)


def kernel(x, l0_mix_norm, l0_w_in, l0_fox_q_norm, l0_fox_k_norm, l0_fox_f_bias, l0_gdn_conv,
           l0_gdn_A_log, l0_gdn_dt_bias, l0_gdn_o_norm, l0_w_out, l0_ffn_norm, l0_w_ff1,
           l0_w_ff2, l1_mix_norm, l1_w_in, l1_hgrn_o_norm, l1_w_out, l1_ffn_norm, l1_w_ff1,
           l1_w_ff2, hgrn_lb_logits):
    b, t, d = x.shape
    n = b * t
    fw = FOX_HEADS * FOX_HEAD_DIM
    gw = GDN_HEADS * GDN_HEAD_DIM
    assert hgrn_lb_logits.shape[0] == 2 and t % 256 == 0 and d == HGRN_HEADS * HGRN_HEAD_DIM
    tm = 512 if n % 512 == 0 else 256
    tq = 256
    tk = 512 if t % 512 == 0 else 256
    nc = t // CHUNK
    row = lambda p: p.reshape(1, -1).astype(F32)
    x2 = x.reshape(n, d)

    o_ff = 3 * fw
    o_gq = o_ff + FOX_HEADS
    o_gb = o_gq + 3 * gw
    o_gg = o_gb + 2 * GDN_HEADS
    wb0 = jnp.concatenate([l0_w_in[:, :o_ff], l0_w_in[:, o_gq:o_gb], l0_w_in[:, o_gg:]],
                          axis=1).astype(BF)
    n_small = FOX_HEADS + 2 * GDN_HEADS
    ws0 = jnp.concatenate([l0_w_in[:, o_ff:o_gq], l0_w_in[:, o_gb:o_gg],
                           jnp.zeros((d, LANES - n_small), F32)], axis=1).astype(BF)
    hid = jnp.arange(fw) // FOX_HEAD_DIM
    pm = jnp.where(hid[:, None] == hid[None, :], 1.0 / FOX_HEAD_DIM, 0.0).astype(BF)
    qkg = jnp.stack([jnp.tile(l0_fox_q_norm, FOX_HEADS) * FOX_HEAD_DIM ** -0.5,
                     jnp.tile(l0_fox_k_norm, FOX_HEADS)]).astype(F32)

    zb0, zs0 = _in0(x2, row(l0_mix_norm), wb0, ws0, pm, qkg, tm)

    ff_t = zs0[:, :FOX_HEADS].reshape(b, t, FOX_HEADS).transpose(1, 0, 2).reshape(t, b * FOX_HEADS)
    c = _fox_cum(ff_t, jnp.tile(l0_fox_f_bias, b).reshape(1, -1).astype(F32))
    c = c.reshape(t, b, FOX_HEADS).transpose(1, 2, 0)
    fox_o = _fox_attention(zb0, c[..., None], c.reshape(b, FOX_HEADS, t // tk, tk), b, t, tq, tk)

    def gate_rows(lo):
        g = zs0[:, lo:lo + GDN_HEADS].reshape(b, t, GDN_HEADS).transpose(0, 2, 1)
        return g.reshape(b, GDN_HEADS, nc, CHUNK)
    gb_row = gate_rows(FOX_HEADS)
    ga_row = gate_rows(FOX_HEADS + GDN_HEADS)
    gdn_o = _gdn(zb0, ga_row, ga_row.swapaxes(-1, -2), gb_row.swapaxes(-1, -2),
                 l0_gdn_conv.astype(F32), l0_gdn_A_log.reshape(GDN_HEADS, 1, 1).astype(F32),
                 l0_gdn_dt_bias.reshape(GDN_HEADS, 1, 1).astype(F32), row(l0_gdn_o_norm), b, t)

    wo0 = l0_w_out.astype(BF)
    x1 = _mlp(fox_o, gdn_o, 0, x2, wo0[:fw], wo0[fw:], row(l0_ffn_norm),
              l0_w_ff1.astype(BF), l0_w_ff2.astype(BF), tm)

    wb1 = jnp.concatenate([l1_w_in[:, :d], l1_w_in[:, 2 * d:]], axis=1).astype(BF)
    wf1 = l1_w_in[:, d:2 * d].astype(BF)
    zb1, zf1 = _in1(x1, row(l1_mix_norm), wb1, wf1, tm)
    ho = _hgrn(zb1, zf1, hgrn_lb_logits.astype(F32), row(l1_hgrn_o_norm), b, t)
    wo1 = l1_w_out.astype(BF)
    half = d // 2
    x3 = _mlp(ho, ho, 1, x1, wo1[:half], wo1[half:], row(l1_ffn_norm),
              l1_w_ff1.astype(BF), l1_w_ff2.astype(BF), tm)
    return x3.reshape(b, t, d)
```

```python
import functools

import jax
import jax.numpy as jnp
from jax import lax
from jax.experimental import pallas as pl
from jax.experimental.pallas import tpu as pltpu

BF = jnp.bfloat16
F32 = jnp.float32
NORM_EPS = 1e-6
LOG2E = 1.4426950408889634
CHUNK = 64
FOX_HEADS, FOX_HEAD_DIM = 8, 64
GDN_HEADS, GDN_HEAD_DIM = 4, 128
HGRN_HEADS, HGRN_HEAD_DIM = 8, 128
GDN_CONV = 4
LANES = 128
SUBLANES = 8
VMEM_LIMIT = 56 * 1024 * 1024


def _cparams(sem):
    return pltpu.CompilerParams(dimension_semantics=sem, vmem_limit_bytes=VMEM_LIMIT)


def _const_spec(shape):
    nd = len(shape)
    return pl.BlockSpec(shape, lambda *_: (0,) * nd, pipeline_mode=pl.Buffered(1))


def _mm(a, b):
    return jnp.dot(a.astype(BF), b.astype(BF), preferred_element_type=F32)


def _mm_nt(a, b):
    return lax.dot_general(a.astype(BF), b.astype(BF), (((1,), (1,)), ((), ())),
                           preferred_element_type=F32)


def _mm_tn(a, b):
    return lax.dot_general(a.astype(BF), b.astype(BF), (((0,), (0,)), ((), ())),
                           preferred_element_type=F32)


def _split3(x):
    hi = x.astype(BF)
    r1 = x - hi.astype(F32)
    mid = r1.astype(BF)
    lo = (r1 - mid.astype(F32)).astype(BF)
    return hi, mid, lo


def _mm_exact_lhs(m_bf, x):
    hi, mid, lo = _split3(x)
    d = lambda p: jnp.dot(m_bf, p, preferred_element_type=F32)
    return d(hi) + d(mid) + d(lo)


def _mm_exact_rhs(x, m_bf):
    hi, mid, lo = _split3(x)
    d = lambda p: jnp.dot(p, m_bf, preferred_element_type=F32)
    return d(hi) + d(mid) + d(lo)


def _sigmoid(x):
    return 1.0 / (1.0 + jnp.exp(-x))


def _silu(x):
    return x * _sigmoid(x)


def _softplus(x):
    return jnp.maximum(x, 0.0) + jnp.log1p(jnp.exp(-jnp.abs(x)))


def _rms(x, gain):
    return x * lax.rsqrt(jnp.mean(x * x, axis=-1, keepdims=True) + NORM_EPS) * gain


def _in0_kernel(x_ref, g_ref, wb_ref, ws_ref, pm_ref, qkg_ref, cw_ref, zb_ref, zs_ref, hist_ref,
                *, tiles_per_seq):
    tm = x_ref.shape[0]
    hd = GDN_HEAD_DIM
    hrows = hist_ref.shape[1]

    @pl.when(pl.program_id(0) % tiles_per_seq == 0)
    def _():
        hist_ref[...] = jnp.zeros_like(hist_ref)

    h = _rms(x_ref[...], g_ref[...]).astype(BF)
    gw = 512
    for c in range(wb_ref.shape[1] // gw):
        z = jnp.dot(h, wb_ref[:, c * gw:(c + 1) * gw], preferred_element_type=F32)
        if c < 2:
            ms = jnp.dot((z * z).astype(BF), pm_ref[...], preferred_element_type=F32)
            z = z * lax.rsqrt(ms + NORM_EPS) * qkg_ref[c:c + 1, :]
        elif 3 <= c < 6:
            j = c - 3
            w = cw_ref[:, j * gw:(j + 1) * gw]
            ext = jnp.concatenate([hist_ref[j], z], axis=0)
            hist_ref[j] = z[tm - hrows:, :]
            y = z * w[GDN_CONV - 1:GDN_CONV]
            for s in range(1, GDN_CONV):
                y = y + ext[hrows - s:hrows - s + tm] * w[GDN_CONV - 1 - s:GDN_CONV - s]
            z = _silu(y)
            if j < 2:
                scale = 1.0 if j else hd ** -0.5
                z = jnp.concatenate(
                    [zh * (lax.rsqrt(jnp.sum(zh * zh, axis=-1, keepdims=True) + NORM_EPS) * scale)
                     for zh in (z[:, k * hd:(k + 1) * hd] for k in range(gw // hd))], axis=1)
        zb_ref[:, c * gw:(c + 1) * gw] = z.astype(BF)
    zs_ref[...] = jnp.dot(h, ws_ref[...], preferred_element_type=F32)


def _in0(x2, gain, wb, ws, pm, qkg, conv_w, tm, t):
    n, d = x2.shape
    nb = wb.shape[1]
    return pl.pallas_call(
        functools.partial(_in0_kernel, tiles_per_seq=t // tm),
        grid=(n // tm,),
        in_specs=[pl.BlockSpec((tm, d), lambda i: (i, 0)),
                  _const_spec((1, d)), _const_spec(wb.shape), _const_spec(ws.shape),
                  _const_spec(pm.shape), _const_spec(qkg.shape), _const_spec(conv_w.shape)],
        out_specs=[pl.BlockSpec((tm, nb), lambda i: (i, 0)),
                   pl.BlockSpec((tm, LANES), lambda i: (i, 0))],
        out_shape=[jax.ShapeDtypeStruct((n, nb), BF), jax.ShapeDtypeStruct((n, LANES), F32)],
        scratch_shapes=[pltpu.VMEM((3, SUBLANES, 512), F32)],
        compiler_params=_cparams(("arbitrary",)),
        name="l0_in_proj",
    )(x2, gain, wb, ws, pm, qkg, conv_w)


def _fox_cum_kernel(ff_ref, bias_ref, hi_ref, mid_ref, lo_ref, *, blk):
    t = ff_ref.shape[0]
    r = lax.broadcasted_iota(jnp.int32, (blk, blk), 0)
    c = lax.broadcasted_iota(jnp.int32, (blk, blk), 1)
    tril = (r >= c).astype(BF)
    carry = jnp.zeros((1, ff_ref.shape[1]), F32)
    for i in range(t // blk):
        z = ff_ref[i * blk:(i + 1) * blk, :] + bias_ref[...]
        ls = jnp.minimum(z, 0.0) - jnp.log1p(jnp.exp(-jnp.abs(z)))
        cb = _mm_exact_lhs(tril, ls) + carry
        carry = cb[blk - 1:blk, :]
        hi, mid, lo = _split3(cb * LOG2E)
        hi_ref[i * blk:(i + 1) * blk, :] = hi
        mid_ref[i * blk:(i + 1) * blk, :] = mid
        lo_ref[i * blk:(i + 1) * blk, :] = lo


def _fox_cum(ff_t, bias_t):
    t, l = ff_t.shape
    return pl.pallas_call(
        functools.partial(_fox_cum_kernel, blk=min(256, t)),
        out_shape=[jax.ShapeDtypeStruct((t, l), BF)] * 3,
        name="fox_cum_gate",
    )(ff_t, bias_t)


FOX_BIAS_LANES = 8


def _fox_kernel(q_ref, k_ref, v_ref, qb_ref, kb_ref, o_ref, *, tq):
    pair = pl.program_id(1)
    t = q_ref.shape[0]
    lane = lax.broadcasted_iota(jnp.int32, (1, LANES), 1)
    zero = jnp.zeros((tq, LANES), BF)
    first = lane < FOX_HEAD_DIM
    bias_head = lane // FOX_BIAS_LANES
    ones = jnp.ones((tq, LANES), BF)
    on_diag = (lax.broadcasted_iota(jnp.int32, (2 * tq, tq), 1)
               <= lax.broadcasted_iota(jnp.int32, (2 * tq, tq), 0) % tq)

    def scores(i):
        rows_i = slice(i * tq, (i + 1) * tq)
        kv = (i + 1) * tq
        q2, qb = q_ref[rows_i, :], qb_ref[rows_i, :]
        rows = []
        for h in range(2):
            qh = jnp.where(first if h == 0 else jnp.logical_not(first), q2, zero)
            qbh = jnp.where(bias_head == 2 * pair + h, qb, zero)
            rows.append(jnp.concatenate([qh, qbh], axis=1))
        ka = jnp.concatenate([k_ref[:kv, :], kb_ref[:kv, :]], axis=1)
        s = lax.dot_general(jnp.concatenate(rows, axis=0), ka, (((1,), (1,)), ((), ())),
                            preferred_element_type=F32)
        diag = jnp.where(on_diag, s[:, kv - tq:], -jnp.inf)
        return diag if i == 0 else jnp.concatenate([s[:, :kv - tq], diag], axis=1)

    def finish(i, s):
        kv = (i + 1) * tq
        p = jnp.exp2(s - jnp.max(s, axis=-1, keepdims=True)).astype(BF)
        va = jnp.concatenate([v_ref[:kv, :], jnp.concatenate([ones] * (i + 1), axis=0)], axis=1)
        acc = jnp.dot(p, va, preferred_element_type=F32)
        o = acc[:, :LANES] / acc[:, LANES:]
        o_ref[i * tq:(i + 1) * tq, :] = jnp.where(first, o[:tq], o[tq:]).astype(o_ref.dtype)

    s = scores(0)
    for i in range(t // tq):
        s_next = scores(i + 1) if (i + 1) * tq < t else None
        finish(i, s)
        s = s_next


def _fox_attention(zb, qbias, kbias, b, t, tq):
    n = zb.shape[0]
    hp = FOX_HEADS // 2
    return pl.pallas_call(
        functools.partial(_fox_kernel, tq=tq),
        grid=(b, hp),
        in_specs=[pl.BlockSpec((t, LANES), lambda bi, p: (bi, p)),
                  pl.BlockSpec((t, LANES), lambda bi, p: (bi, hp + p)),
                  pl.BlockSpec((t, LANES), lambda bi, p: (bi, 2 * hp + p)),
                  pl.BlockSpec((t, LANES), lambda bi, p: (bi, 0)),
                  pl.BlockSpec((t, LANES), lambda bi, p: (bi, 0))],
        out_specs=pl.BlockSpec((t, LANES), lambda bi, p: (bi, p)),
        out_shape=jax.ShapeDtypeStruct((n, FOX_HEADS * FOX_HEAD_DIM), BF),
        compiler_params=_cparams(("parallel", "parallel")),
        name="fox_attention",
    )(zb, zb, zb, qbias, kbias)


def _neumann_inverse(mats, eye):
    ps = [eye - a for a in mats]
    aps = list(mats)
    k = 2
    while k < CHUNK:
        aps = [_mm(ap, ap) for ap in aps]
        ps = [p + _mm(p, ap) for p, ap in zip(ps, aps)]
        k *= 2
    return ps


def _gdn_kernel(alog_ref, dtb_ref, q_ref, k_ref, v_ref, gg_ref, gar_ref, gac_ref, gbc_ref,
                on_ref, o_ref, qd_s, w_s, u_s, kd_s, qk_s, grow_s, gcol_s, bcol_s, s_s):
    t = q_ref.shape[0]
    nc = t // CHUNK
    hd = GDN_HEAD_DIM

    ri = lax.broadcasted_iota(jnp.int32, (CHUNK, CHUNK), 0)
    ci = lax.broadcasted_iota(jnp.int32, (CHUNK, CHUNK), 1)
    lower = ri >= ci
    strict = ri > ci
    tril = lower.astype(BF)
    triu = (ri <= ci).astype(BF)
    eye = (ri == ci).astype(F32)
    for h in range(GDN_HEADS):
        a = -jnp.exp(alog_ref[h])
        grow_s[h] = _mm_exact_rhs(a * _softplus(gar_ref[0, h] + dtb_ref[h]), triu)
        gcol_s[h] = _mm_exact_lhs(tril, a * _softplus(gac_ref[0, h] + dtb_ref[h]))
        bcol_s[h] = _sigmoid(gbc_ref[0, h])
    s_s[...] = jnp.zeros_like(s_s)

    lane_nc = lax.broadcasted_iota(jnp.int32, (CHUNK, nc), 1)
    heads = range(GDN_HEADS)
    cat = lambda parts: jnp.concatenate(parts, axis=1)

    per_iter = 4

    def local_step(it, carry):
        r0 = pl.multiple_of(it * (per_iter * CHUNK), per_iter * CHUNK)
        rows = pl.ds(r0, per_iter * CHUNK)
        chains = [(c, h) for c in range(per_iter) for h in heads]
        csl = lambda c: slice(c * CHUNK, (c + 1) * CHUNK)
        hsl = lambda h: slice(h * hd, (h + 1) * hd)
        q_all, k_all, v_all = (r[rows, :].astype(F32) for r in (q_ref, k_ref, v_ref))
        q = [q_all[csl(c), hsl(h)] for c, h in chains]
        k = [k_all[csl(c), hsl(h)] for c, h in chains]
        v = [v_all[csl(c), hsl(h)] for c, h in chains]
        sel = [lane_nc == it * per_iter + c for c in range(per_iter)]
        g_c = [jnp.sum(jnp.where(sel[c], gcol_s[h], 0.0), axis=1, keepdims=True)
               for c, h in chains]
        b_c = [jnp.sum(jnp.where(sel[c], bcol_s[h], 0.0), axis=1, keepdims=True)
               for c, h in chains]
        g_r = [grow_s[h, pl.ds(it * per_iter + c, 1), :] for c, h in chains]
        decay = [jnp.exp(jnp.where(lower, gc - gr, -jnp.inf)) for gc, gr in zip(g_c, g_r)]
        kb = [x * bc for x, bc in zip(k, b_c)]
        kk = [_mm_nt(x, y) for x, y in zip(kb, k)]
        qk = [_mm_nt(x, y) for x, y in zip(q, k)]
        t_inv = _neumann_inverse([jnp.where(strict, x * d, 0.0) for x, d in zip(kk, decay)], eye)
        eg = [jnp.exp(gc) for gc in g_c]
        uw = [_mm(ti, jnp.concatenate([x * bc, y * e], axis=1))
              for ti, x, bc, y, e in zip(t_inv, v, b_c, kb, eg)]
        kd = [x * jnp.exp(gr[:, CHUNK - 1:CHUNK] - gc) for x, gr, gc in zip(k, g_r, g_c)]
        grid = lambda parts: jnp.concatenate(
            [cat(parts[c * GDN_HEADS:(c + 1) * GDN_HEADS]) for c in range(per_iter)], axis=0)
        qd_s[rows, :] = grid([x * e for x, e in zip(q, eg)]).astype(BF)
        w_s[rows, :] = grid([x[:, hd:] for x in uw]).astype(BF)
        u_s[rows, :] = grid([x[:, :hd] for x in uw])
        kd_s[rows, :] = grid(kd).astype(BF)
        qk_s[rows, :] = grid([x * d for x, d in zip(qk, decay)]).astype(BF)
        return carry

    lax.fori_loop(0, nc // per_iter, local_step, 0)

    def state_step(n, carry):
        r0 = pl.multiple_of(n * CHUNK, CHUNK)
        rows = pl.ds(r0, CHUNK)
        hsl = lambda h: slice(h * hd, (h + 1) * hd)
        s_prev = [s_s[h] for h in heads]
        sb = [x.astype(BF) for x in s_prev]
        ws_ = [_mm(w_s[rows, hsl(h)], sb[h]) for h in heads]
        qs_ = [_mm(qd_s[rows, hsl(h)], sb[h]) for h in heads]
        v_new = [u_s[rows, hsl(h)] - ws_[h] for h in heads]
        upd = [_mm_tn(kd_s[rows, hsl(h)], v_new[h]) for h in heads]
        qv = [_mm(qk_s[rows, h * CHUNK:(h + 1) * CHUNK], v_new[h]) for h in heads]
        g_last = [grow_s[h, pl.ds(n, 1), :][:, CHUNK - 1:CHUNK] for h in heads]
        s_s[...] = jnp.stack([s_prev[h] * jnp.exp(g_last[h]) + upd[h] for h in heads])
        outs = [_rms(qs_[h] + qv[h], on_ref[...]) for h in heads]
        gate = _silu(gg_ref[rows, :].astype(F32))
        o_ref[rows, :] = (cat(outs) * gate).astype(o_ref.dtype)
        return carry

    lax.fori_loop(0, nc, state_step, 0)


def _gdn(zb, ga_row, ga_col, gb_col, a_log, dt_bias, o_norm, b, t):
    n = zb.shape[0]
    nc = t // CHUNK
    width = GDN_HEADS * GDN_HEAD_DIM
    fox_blocks = 3 * FOX_HEADS * FOX_HEAD_DIM // width
    colspec = lambda j: pl.BlockSpec((t, width), lambda bi: (bi, fox_blocks + j))
    gspec = lambda shp: pl.BlockSpec((1,) + shp, lambda bi: (bi, 0, 0, 0))
    return pl.pallas_call(
        _gdn_kernel,
        grid=(b,),
        in_specs=[_const_spec(a_log.shape), _const_spec(dt_bias.shape),
                  colspec(0), colspec(1), colspec(2), colspec(3),
                  gspec((GDN_HEADS, nc, CHUNK)), gspec((GDN_HEADS, CHUNK, nc)),
                  gspec((GDN_HEADS, CHUNK, nc)), _const_spec(o_norm.shape)],
        out_specs=pl.BlockSpec((t, width), lambda bi: (bi, 0)),
        out_shape=jax.ShapeDtypeStruct((n, width), BF),
        scratch_shapes=[pltpu.VMEM((t, width), BF), pltpu.VMEM((t, width), BF),
                        pltpu.VMEM((t, width), F32),
                        pltpu.VMEM((t, width), BF), pltpu.VMEM((t, GDN_HEADS * CHUNK), BF),
                        pltpu.VMEM((GDN_HEADS, nc, CHUNK), F32),
                        pltpu.VMEM((GDN_HEADS, CHUNK, nc), F32),
                        pltpu.VMEM((GDN_HEADS, CHUNK, nc), F32),
                        pltpu.VMEM((GDN_HEADS, GDN_HEAD_DIM, GDN_HEAD_DIM), F32)],
        compiler_params=_cparams(("parallel",)),
        name="gated_deltanet",
    )(a_log, dt_bias, zb, zb, zb, zb, ga_row, ga_col, gb_col, o_norm)


def _mlp_kernel(a1_ref, a2_ref, x_ref, wo1_ref, wo2_ref, g_ref, w1_ref, w2_ref, o_ref, acc_ref,
                *, fc):
    mix = (jnp.dot(a1_ref[...], wo1_ref[...], preferred_element_type=F32)
           + jnp.dot(a2_ref[...], wo2_ref[...], preferred_element_type=F32))
    x1 = x_ref[...] + mix
    h = _rms(x1, g_ref[...]).astype(BF)
    acc_ref[...] = x1
    for c in range(w1_ref.shape[1] // fc):
        a = jnp.maximum(jnp.dot(h, w1_ref[:, c * fc:(c + 1) * fc],
                                preferred_element_type=F32), 0.0)
        acc_ref[...] += jnp.dot((a * a).astype(BF), w2_ref[c * fc:(c + 1) * fc, :],
                                preferred_element_type=F32)
    o_ref[...] = acc_ref[...]


def _mlp(a1, a2, a2_block, x2, wo1, wo2, gain, w1, w2, tm):
    n, d = x2.shape
    half = wo1.shape[0]
    return pl.pallas_call(
        functools.partial(_mlp_kernel, fc=1024),
        grid=(n // tm,),
        in_specs=[pl.BlockSpec((tm, half), lambda i: (i, 0)),
                  pl.BlockSpec((tm, half), lambda i: (i, a2_block)),
                  pl.BlockSpec((tm, d), lambda i: (i, 0)),
                  _const_spec(wo1.shape), _const_spec(wo2.shape), _const_spec(gain.shape),
                  _const_spec(w1.shape), _const_spec(w2.shape)],
        out_specs=pl.BlockSpec((tm, d), lambda i: (i, 0)),
        out_shape=jax.ShapeDtypeStruct((n, d), F32),
        scratch_shapes=[pltpu.VMEM((tm, d), F32)],
        compiler_params=_cparams(("parallel",)),
        name="out_proj_mlp",
    )(a1, a2, x2, wo1, wo2, gain, w1, w2)


def _in1_kernel(x_ref, g_ref, wb_ref, wf_ref, zb_ref, zf_ref):
    h = _rms(x_ref[...], g_ref[...]).astype(BF)
    gw = 512
    for c in range(wb_ref.shape[1] // gw):
        zb_ref[:, c * gw:(c + 1) * gw] = jnp.dot(
            h, wb_ref[:, c * gw:(c + 1) * gw], preferred_element_type=F32).astype(BF)
    for c in range(wf_ref.shape[1] // gw):
        zf_ref[:, c * gw:(c + 1) * gw] = jnp.dot(
            h, wf_ref[:, c * gw:(c + 1) * gw], preferred_element_type=F32)


def _in1(x2, gain, wb, wf, tm):
    n, d = x2.shape
    return pl.pallas_call(
        _in1_kernel,
        grid=(n // tm,),
        in_specs=[pl.BlockSpec((tm, d), lambda i: (i, 0)),
                  _const_spec(gain.shape), _const_spec(wb.shape), _const_spec(wf.shape)],
        out_specs=[pl.BlockSpec((tm, wb.shape[1]), lambda i: (i, 0)),
                   pl.BlockSpec((tm, wf.shape[1]), lambda i: (i, 0))],
        out_shape=[jax.ShapeDtypeStruct((n, wb.shape[1]), BF),
                   jax.ShapeDtypeStruct((n, wf.shape[1]), F32)],
        compiler_params=_cparams(("parallel",)),
        name="l1_in_proj",
    )(x2, gain, wb, wf)


HG_SUB = SUBLANES
HG_GROUP = 4


def _hgrn_intra(qs, ks, bs, key_row):
    nb = CHUNK // HG_SUB
    d = qs[0].shape[1]
    blk = lambda x, i: x[HG_SUB * i:HG_SUB * (i + 1)]
    row = lax.broadcasted_iota(jnp.int32, (CHUNK, CHUNK), 0)
    col = lax.broadcasted_iota(jnp.int32, (CHUNK, CHUNK), 1)
    own_block = (row // HG_SUB == col // HG_SUB) & (col % HG_SUB <= row % HG_SUB)
    gather = (lax.broadcasted_iota(jnp.int32, (CHUNK, HG_SUB * d), 0) % HG_SUB
              == lax.broadcasted_iota(jnp.int32, (CHUNK, HG_SUB * d), 1) // d).astype(BF)

    wide = []
    for c, (q, b) in enumerate(zip(qs, bs)):
        cols = []
        for s in range(HG_SUB):
            parts = []
            for i in range(nb):
                r = HG_SUB * i + s
                e = jnp.exp2(jnp.minimum(blk(b, i) - key_row(c, r, 0), 0.0))
                parts.append(blk(q, i) * e * key_row(c, r, 1))
            cols.append(jnp.concatenate(parts, axis=0))
        wide.append(jnp.concatenate(cols, axis=1).astype(BF))

    zero = jnp.zeros((HG_SUB, d), F32)
    offs = []
    for q, k, b in zip(qs, ks, bs):
        lhs, rhs = [], []
        for j in range(1, nb // 2):
            bref = b[2 * HG_SUB * j:2 * HG_SUB * j + 1, :]
            lg, rg = [zero] * nb, [zero] * nb
            for i in (2 * j, 2 * j + 1):
                lg[i] = blk(q, i) * jnp.exp2(blk(b, i) - bref)
            for i in range(2 * j):
                rg[i] = blk(k, i) * jnp.exp2(bref - blk(b, i))
            lhs.append(jnp.concatenate(lg, axis=0))
            rhs.append(jnp.concatenate(rg, axis=0))
        for j in range(nb // 2):
            bref = b[HG_SUB * (2 * j + 1):HG_SUB * (2 * j + 1) + 1, :]
            lg, rg = [zero] * nb, [zero] * nb
            lg[2 * j + 1] = blk(q, 2 * j + 1) * jnp.exp2(blk(b, 2 * j + 1) - bref)
            rg[2 * j] = blk(k, 2 * j) * jnp.exp2(bref - blk(b, 2 * j))
            lhs.append(jnp.concatenate(lg, axis=0))
            rhs.append(jnp.concatenate(rg, axis=0))
        offs.append((jnp.concatenate(lhs, axis=1).astype(BF),
                     jnp.concatenate(rhs, axis=1).astype(BF)))
    diag = [jnp.where(own_block, _mm_nt(w, gather), 0.0) for w in wide]
    return [dg + _mm_nt(l, r) for dg, (l, r) in zip(diag, offs)]


def _hgrn_kernel(zq_ref, zi_ref, zg_ref, zf_ref, lbl_ref, on_ref, o_ref,
                 oi_s, qd_s, kv_s, dec_s, key_s, *, group):
    t = zq_ref.shape[0]
    gr = group * CHUNK
    ri = lax.broadcasted_iota(jnp.int32, (gr, gr), 0)
    ci = lax.broadcasted_iota(jnp.int32, (gr, gr), 1)
    tril = ((ri >= ci) & (ri // CHUNK == ci // CHUNK)).astype(BF)

    l0 = lbl_ref[0:1, :]
    l1 = lbl_ref[1:2, :]
    mx = jnp.maximum(l0, l1)
    e0 = jnp.exp(l0 - mx)
    e1 = jnp.exp(l1 - mx)
    s0 = e0 / (e0 + e1)
    s1 = e1 / (e0 + e1)
    lb = (s0 + s1) - s0

    def local_step(g, carry):
        rows = pl.ds(pl.multiple_of(g * gr, gr), gr)
        q = _silu(zq_ref[rows, :].astype(F32))
        f = lb + (1.0 - lb) * _sigmoid(zf_ref[rows, :])
        k = 1.0 - f
        v = zi_ref[rows, :]
        b = _mm_exact_lhs(tril, jnp.log2(f))
        key_s[0] = b
        key_s[1] = k
        key_row = lambda c, r, which: key_s[which, c * CHUNK + r:c * CHUNK + r + 1, :]
        sls = [slice(c * CHUNK, (c + 1) * CHUNK) for c in range(group)]
        bc = [b[sl] for sl in sls]
        b_last = [x[CHUNK - 1:CHUNK, :] for x in bc]
        kdec = [k[sl] * jnp.exp2(bl - x) for sl, bl, x in zip(sls, b_last, bc)]
        kv = [_mm_tn(v[sl], kd) for sl, kd in zip(sls, kdec)]
        dmat = _hgrn_intra([q[sl] for sl in sls], [k[sl] for sl in sls], bc, key_row)
        oi = [_mm(dm, v[sl]) for dm, sl in zip(dmat, sls)]
        dec = [jnp.exp2(bl) for bl in b_last]
        oi_s[rows, :] = jnp.concatenate(oi, axis=0)
        qd_s[rows, :] = (q * jnp.exp2(b)).astype(BF)
        kv_s[pl.ds(g * group, group)] = jnp.stack(kv)
        dec_s[pl.ds(g * group, group)] = jnp.stack(dec)
        return carry

    lax.fori_loop(0, t // gr, local_step, 0)

    def state_step(g, st):
        r0 = pl.multiple_of(g * gr, gr)
        outs = []
        for c in range(group):
            n = g * group + c
            rows = pl.ds(r0 + c * CHUNK, CHUNK)
            outs.append(oi_s[rows, :] + _mm_nt(qd_s[rows, :], st))
            st = st * dec_s[n] + kv_s[n]
        rows = pl.ds(r0, gr)
        o = _rms(jnp.concatenate(outs, axis=0), on_ref[...])
        o_ref[rows, :] = (o * _silu(zg_ref[rows, :].astype(F32))).astype(o_ref.dtype)
        return st

    d = zq_ref.shape[1]
    lax.fori_loop(0, t // gr, state_step, jnp.zeros((d, d), F32))


def _hgrn(zb, zf, lb_logits, o_norm, b, t):
    n = zb.shape[0]
    hd = HGRN_HEAD_DIM
    nh = HGRN_HEADS
    nc = t // CHUNK
    return pl.pallas_call(
        functools.partial(_hgrn_kernel, group=HG_GROUP),
        grid=(b, nh),
        in_specs=[pl.BlockSpec((t, hd), lambda bi, h: (bi, h)),
                  pl.BlockSpec((t, hd), lambda bi, h: (bi, nh + h)),
                  pl.BlockSpec((t, hd), lambda bi, h: (bi, 2 * nh + h)),
                  pl.BlockSpec((t, hd), lambda bi, h: (bi, h)),
                  pl.BlockSpec((lb_logits.shape[0], hd), lambda bi, h: (0, h)),
                  _const_spec(o_norm.shape)],
        out_specs=pl.BlockSpec((t, hd), lambda bi, h: (bi, h)),
        out_shape=jax.ShapeDtypeStruct((n, nh * hd), BF),
        scratch_shapes=[pltpu.VMEM((t, hd), F32), pltpu.VMEM((t, hd), BF),
                        pltpu.VMEM((nc, hd, hd), F32), pltpu.VMEM((nc, 1, hd), F32),
                        pltpu.VMEM((2, HG_GROUP * CHUNK, hd), F32)],
        compiler_params=_cparams(("parallel", "parallel")),
        name="hgrn2",
    )(zb, zb, zb, zf, lb_logits, o_norm)


def kernel(x, l0_mix_norm, l0_w_in, l0_fox_q_norm, l0_fox_k_norm, l0_fox_f_bias, l0_gdn_conv,
           l0_gdn_A_log, l0_gdn_dt_bias, l0_gdn_o_norm, l0_w_out, l0_ffn_norm, l0_w_ff1,
           l0_w_ff2, l1_mix_norm, l1_w_in, l1_hgrn_o_norm, l1_w_out, l1_ffn_norm, l1_w_ff1,
           l1_w_ff2, hgrn_lb_logits):
    b, t, d = x.shape
    n = b * t
    fw = FOX_HEADS * FOX_HEAD_DIM
    gw = GDN_HEADS * GDN_HEAD_DIM
    assert hgrn_lb_logits.shape[0] == 2 and t % 256 == 0 and d == HGRN_HEADS * HGRN_HEAD_DIM
    tm = 512 if t % 512 == 0 else 256
    tq = 256
    nc = t // CHUNK
    row = lambda p: p.reshape(1, -1).astype(F32)
    x2 = x.reshape(n, d)

    o_ff = 3 * fw
    o_gq = o_ff + FOX_HEADS
    o_gb = o_gq + 3 * gw
    o_gg = o_gb + 2 * GDN_HEADS
    wb0 = jnp.concatenate([l0_w_in[:, :o_ff], l0_w_in[:, o_gq:o_gb], l0_w_in[:, o_gg:]],
                          axis=1).astype(BF)
    n_small = FOX_HEADS + 2 * GDN_HEADS
    ws0 = jnp.concatenate([l0_w_in[:, o_ff:o_gq], l0_w_in[:, o_gb:o_gg],
                           jnp.zeros((d, LANES - n_small), F32)], axis=1).astype(BF)
    hid = jnp.arange(fw) // FOX_HEAD_DIM
    pm = jnp.where(hid[:, None] == hid[None, :], 1.0 / FOX_HEAD_DIM, 0.0).astype(BF)
    qkg = jnp.stack([jnp.tile(l0_fox_q_norm, FOX_HEADS) * (FOX_HEAD_DIM ** -0.5 * LOG2E),
                     jnp.tile(l0_fox_k_norm, FOX_HEADS)]).astype(F32)

    zb0, zs0 = _in0(x2, row(l0_mix_norm), wb0, ws0, pm, qkg, l0_gdn_conv.astype(F32), tm, t)

    ff_t = zs0[:, :FOX_HEADS].reshape(b, t, FOX_HEADS).transpose(1, 0, 2).reshape(t, b * FOX_HEADS)
    c3 = _fox_cum(ff_t, jnp.tile(l0_fox_f_bias, b).reshape(1, -1).astype(F32))
    one, nil = jnp.ones_like(c3[0]), jnp.zeros_like(c3[0])

    def bias_features(parts):
        f = jnp.stack(parts + [nil] * (FOX_BIAS_LANES - len(parts)), axis=-1)
        f = f.reshape(t, b, FOX_HEADS * FOX_BIAS_LANES).transpose(1, 0, 2).reshape(n, -1)
        return jnp.pad(f, ((0, 0), (0, LANES - f.shape[1])))
    qbias = bias_features([c3[0], c3[1], c3[2], one, one, one])
    kbias = bias_features([one, one, one, -c3[0], -c3[1], -c3[2]])
    fox_o = _fox_attention(zb0, qbias, kbias, b, t, tq)

    def gate_rows(lo):
        g = zs0[:, lo:lo + GDN_HEADS].reshape(b, t, GDN_HEADS).transpose(0, 2, 1)
        return g.reshape(b, GDN_HEADS, nc, CHUNK)
    gb_row = gate_rows(FOX_HEADS)
    ga_row = gate_rows(FOX_HEADS + GDN_HEADS)
    gdn_o = _gdn(zb0, ga_row, ga_row.swapaxes(-1, -2), gb_row.swapaxes(-1, -2),
                 l0_gdn_A_log.reshape(GDN_HEADS, 1, 1).astype(F32),
                 l0_gdn_dt_bias.reshape(GDN_HEADS, 1, 1).astype(F32), row(l0_gdn_o_norm), b, t)

    wo0 = l0_w_out.astype(BF)
    x1 = _mlp(fox_o, gdn_o, 0, x2, wo0[:fw], wo0[fw:], row(l0_ffn_norm),
              l0_w_ff1.astype(BF), l0_w_ff2.astype(BF), tm)

    wb1 = jnp.concatenate([l1_w_in[:, :d], l1_w_in[:, 2 * d:]], axis=1).astype(BF)
    wf1 = l1_w_in[:, d:2 * d].astype(BF)
    zb1, zf1 = _in1(x1, row(l1_mix_norm), wb1, wf1, tm)
    ho = _hgrn(zb1, zf1, hgrn_lb_logits.astype(F32), row(l1_hgrn_o_norm), b, t)
    wo1 = l1_w_out.astype(BF)
    half = d // 2
    x3 = _mlp(ho, ho, 1, x1, wo1[:half], wo1[half:], row(l1_ffn_norm),
              l1_w_ff1.astype(BF), l1_w_ff2.astype(BF), tm)
    return x3.reshape(b, t, d)
```

```python
import functools

import jax
import jax.numpy as jnp
from jax import lax
from jax.experimental import pallas as pl
from jax.experimental.pallas import tpu as pltpu

BF = jnp.bfloat16
F32 = jnp.float32
NORM_EPS = 1e-6
LOG2E = 1.4426950408889634
CHUNK = 64
FOX_HEADS, FOX_HEAD_DIM = 8, 64
GDN_HEADS, GDN_HEAD_DIM = 4, 128
HGRN_HEADS, HGRN_HEAD_DIM = 8, 128
GDN_CONV = 4
LANES = 128
SUBLANES = 8
VMEM_LIMIT = 56 * 1024 * 1024


def _cparams(sem):
    return pltpu.CompilerParams(dimension_semantics=sem, vmem_limit_bytes=VMEM_LIMIT)


def _const_spec(shape):
    nd = len(shape)
    return pl.BlockSpec(shape, lambda *_: (0,) * nd, pipeline_mode=pl.Buffered(1))


def _mm(a, b):
    return jnp.dot(a.astype(BF), b.astype(BF), preferred_element_type=F32)


def _mm_nt(a, b):
    return lax.dot_general(a.astype(BF), b.astype(BF), (((1,), (1,)), ((), ())),
                           preferred_element_type=F32)


def _mm_tn(a, b):
    return lax.dot_general(a.astype(BF), b.astype(BF), (((0,), (0,)), ((), ())),
                           preferred_element_type=F32)


def _split3(x):
    hi = x.astype(BF)
    r1 = x - hi.astype(F32)
    mid = r1.astype(BF)
    lo = (r1 - mid.astype(F32)).astype(BF)
    return hi, mid, lo


def _mm_exact_lhs(m_bf, x):
    n = x.shape[1]
    if n % LANES:
        hi, mid, lo = _split3(x)
        d = lambda p: jnp.dot(m_bf, p, preferred_element_type=F32)
        return d(hi) + d(mid) + d(lo)
    y = jnp.dot(m_bf, jnp.concatenate(_split3(x), axis=1), preferred_element_type=F32)
    return y[:, :n] + y[:, n:2 * n] + y[:, 2 * n:]


def _mm_exact_rhs(x, m_bf):
    hi, mid, lo = _split3(x)
    d = lambda p: jnp.dot(p, m_bf, preferred_element_type=F32)
    return d(hi) + d(mid) + d(lo)


def _sigmoid(x):
    return 1.0 / (1.0 + jnp.exp(-x))


def _silu(x):
    return x * _sigmoid(x)


def _softplus(x):
    return jnp.maximum(x, 0.0) + jnp.log1p(jnp.exp(-jnp.abs(x)))


def _rms(x, gain):
    return x * lax.rsqrt(jnp.mean(x * x, axis=-1, keepdims=True) + NORM_EPS) * gain


def _in0_kernel(x_ref, g_ref, wb_ref, ws_ref, pm_ref, qkg_ref, cw_ref, zb_ref, zs_ref, hist_ref,
                *, tiles_per_seq):
    tm = x_ref.shape[0]
    hd = GDN_HEAD_DIM
    hrows = hist_ref.shape[1]

    @pl.when(pl.program_id(0) % tiles_per_seq == 0)
    def _():
        hist_ref[...] = jnp.zeros_like(hist_ref)

    h = _rms(x_ref[...], g_ref[...]).astype(BF)
    gw = 512
    for c in range(wb_ref.shape[1] // gw):
        z = jnp.dot(h, wb_ref[:, c * gw:(c + 1) * gw], preferred_element_type=F32)
        if c < 2:
            ms = jnp.dot((z * z).astype(BF), pm_ref[...], preferred_element_type=F32)
            z = z * lax.rsqrt(ms + NORM_EPS) * qkg_ref[c:c + 1, :]
        elif 3 <= c < 6:
            j = c - 3
            w = cw_ref[:, j * gw:(j + 1) * gw]
            ext = jnp.concatenate([hist_ref[j], z], axis=0)
            hist_ref[j] = z[tm - hrows:, :]
            y = z * w[GDN_CONV - 1:GDN_CONV]
            for s in range(1, GDN_CONV):
                y = y + ext[hrows - s:hrows - s + tm] * w[GDN_CONV - 1 - s:GDN_CONV - s]
            z = _silu(y)
            if j < 2:
                scale = 1.0 if j else hd ** -0.5
                z = jnp.concatenate(
                    [zh * (lax.rsqrt(jnp.sum(zh * zh, axis=-1, keepdims=True) + NORM_EPS) * scale)
                     for zh in (z[:, k * hd:(k + 1) * hd] for k in range(gw // hd))], axis=1)
        zb_ref[:, c * gw:(c + 1) * gw] = z.astype(BF)
    zs_ref[...] = jnp.dot(h, ws_ref[...], preferred_element_type=F32)


def _in0(x2, gain, wb, ws, pm, qkg, conv_w, tm, t):
    n, d = x2.shape
    nb = wb.shape[1]
    return pl.pallas_call(
        functools.partial(_in0_kernel, tiles_per_seq=t // tm),
        grid=(n // tm,),
        in_specs=[pl.BlockSpec((tm, d), lambda i: (i, 0)),
                  _const_spec((1, d)), _const_spec(wb.shape), _const_spec(ws.shape),
                  _const_spec(pm.shape), _const_spec(qkg.shape), _const_spec(conv_w.shape)],
        out_specs=[pl.BlockSpec((tm, nb), lambda i: (i, 0)),
                   pl.BlockSpec((tm, LANES), lambda i: (i, 0))],
        out_shape=[jax.ShapeDtypeStruct((n, nb), BF), jax.ShapeDtypeStruct((n, LANES), F32)],
        scratch_shapes=[pltpu.VMEM((3, SUBLANES, 512), F32)],
        compiler_params=_cparams(("arbitrary",)),
        name="l0_in_proj",
    )(x2, gain, wb, ws, pm, qkg, conv_w)


FOX_BIAS_LANES = 8


def _fox_cum_kernel(zs_ref, bias_ref, qb_ref, kb_ref, *, blk):
    t = zs_ref.shape[0]
    r = lax.broadcasted_iota(jnp.int32, (blk, blk), 0)
    c = lax.broadcasted_iota(jnp.int32, (blk, blk), 1)
    tril = (r >= c).astype(BF)
    pr = lax.broadcasted_iota(jnp.int32, (3 * LANES, 2 * LANES), 0)
    pc = lax.broadcasted_iota(jnp.int32, (3 * LANES, 2 * LANES), 1)
    place = ((pr % LANES < FOX_HEADS)
             & (pc % LANES == (pr % LANES) * FOX_BIAS_LANES + pr // LANES + 3 * (pc // LANES))
             ).astype(BF)
    lane = lax.broadcasted_iota(jnp.int32, (1, LANES), 1)
    slot = lane % FOX_BIAS_LANES
    used = lane < FOX_HEADS * FOX_BIAS_LANES
    ones_q = jnp.where(used & (slot >= 3) & (slot < 6), 1.0, 0.0)
    ones_k = jnp.where(used & (slot < 3), 1.0, 0.0)
    blocks = [slice(i * blk, (i + 1) * blk) for i in range(t // blk)]
    local = []
    for rows in blocks:
        z = zs_ref[rows, :] + bias_ref[...]
        ls = jnp.minimum(z, 0.0) - jnp.log1p(jnp.exp(-jnp.abs(z)))
        local.append(_mm_exact_lhs(tril, ls))
    carry = jnp.zeros((1, LANES), F32)
    for rows, loc in zip(blocks, local):
        cb = loc + carry
        carry = cb[blk - 1:blk, :]
        feat = jnp.dot(jnp.concatenate(_split3(cb * LOG2E), axis=1), place,
                       preferred_element_type=F32)
        qb_ref[rows, :] = (feat[:, :LANES] + ones_q).astype(BF)
        kb_ref[rows, :] = (ones_k - feat[:, LANES:]).astype(BF)


def _fox_cum(zs, bias_row, b, t):
    n = zs.shape[0]
    spec = pl.BlockSpec((t, LANES), lambda bi: (bi, 0))
    return pl.pallas_call(
        functools.partial(_fox_cum_kernel, blk=256),
        grid=(b,),
        in_specs=[spec, _const_spec(bias_row.shape)],
        out_specs=[spec, spec],
        out_shape=[jax.ShapeDtypeStruct((n, LANES), BF)] * 2,
        compiler_params=_cparams(("parallel",)),
        name="fox_cum_gate",
    )(zs, bias_row)


def _fox_kernel(q_ref, k_ref, v_ref, qb_ref, kb_ref, o_ref, *, tq):
    pair = pl.program_id(1)
    t = q_ref.shape[0]
    lane = lax.broadcasted_iota(jnp.int32, (1, LANES), 1)
    zero = jnp.zeros((tq, LANES), BF)
    first = lane < FOX_HEAD_DIM
    bias_head = lane // FOX_BIAS_LANES
    ones = jnp.ones((tq, LANES), BF)
    on_diag = (lax.broadcasted_iota(jnp.int32, (2 * tq, tq), 1)
               <= lax.broadcasted_iota(jnp.int32, (2 * tq, tq), 0) % tq)

    def scores(i):
        rows_i = slice(i * tq, (i + 1) * tq)
        kv = (i + 1) * tq
        q2, qb = q_ref[rows_i, :], qb_ref[rows_i, :]
        rows = []
        for h in range(2):
            qh = jnp.where(first if h == 0 else jnp.logical_not(first), q2, zero)
            qbh = jnp.where(bias_head == 2 * pair + h, qb, zero)
            rows.append(jnp.concatenate([qh, qbh], axis=1))
        ka = jnp.concatenate([k_ref[:kv, :], kb_ref[:kv, :]], axis=1)
        s = lax.dot_general(jnp.concatenate(rows, axis=0), ka, (((1,), (1,)), ((), ())),
                            preferred_element_type=F32)
        diag = jnp.where(on_diag, s[:, kv - tq:], -jnp.inf)
        return diag if i == 0 else jnp.concatenate([s[:, :kv - tq], diag], axis=1)

    def finish(i, s):
        kv = (i + 1) * tq
        p = jnp.exp2(s - jnp.max(s, axis=-1, keepdims=True)).astype(BF)
        va = jnp.concatenate([v_ref[:kv, :], jnp.concatenate([ones] * (i + 1), axis=0)], axis=1)
        acc = jnp.dot(p, va, preferred_element_type=F32)
        o = acc[:, :LANES] / acc[:, LANES:]
        o_ref[i * tq:(i + 1) * tq, :] = jnp.where(first, o[:tq], o[tq:]).astype(o_ref.dtype)

    s = scores(0)
    for i in range(t // tq):
        s_next = scores(i + 1) if (i + 1) * tq < t else None
        finish(i, s)
        s = s_next


def _fox_attention(zb, qbias, kbias, b, t, tq):
    n = zb.shape[0]
    hp = FOX_HEADS // 2
    return pl.pallas_call(
        functools.partial(_fox_kernel, tq=tq),
        grid=(b, hp),
        in_specs=[pl.BlockSpec((t, LANES), lambda bi, p: (bi, p)),
                  pl.BlockSpec((t, LANES), lambda bi, p: (bi, hp + p)),
                  pl.BlockSpec((t, LANES), lambda bi, p: (bi, 2 * hp + p)),
                  pl.BlockSpec((t, LANES), lambda bi, p: (bi, 0)),
                  pl.BlockSpec((t, LANES), lambda bi, p: (bi, 0))],
        out_specs=pl.BlockSpec((t, LANES), lambda bi, p: (bi, p)),
        out_shape=jax.ShapeDtypeStruct((n, FOX_HEADS * FOX_HEAD_DIM), BF),
        compiler_params=_cparams(("parallel", "parallel")),
        name="fox_attention",
    )(zb, zb, zb, qbias, kbias)


def _neumann_inverse(mats, eye):
    ps = [eye - a for a in mats]
    aps = list(mats)
    k = 2
    while k < CHUNK:
        aps = [_mm(ap, ap) for ap in aps]
        ps = [p + _mm(p, ap) for p, ap in zip(ps, aps)]
        k *= 2
    return ps


def _gdn_kernel(alog_ref, dtb_ref, q_ref, k_ref, v_ref, gg_ref, gar_ref, gac_ref, gbc_ref,
                on_ref, o_ref, lhs_s, ku_s, u_s, qk_s, pend_s, grow_s, gcol_s, bcol_s, s_s):
    t = q_ref.shape[0]
    nc = t // CHUNK
    hd = GDN_HEAD_DIM
    lrows = hd + 2 * CHUNK

    ri = lax.broadcasted_iota(jnp.int32, (CHUNK, CHUNK), 0)
    ci = lax.broadcasted_iota(jnp.int32, (CHUNK, CHUNK), 1)
    lower = ri >= ci
    strict = ri > ci
    tril = lower.astype(BF)
    triu = (ri <= ci).astype(BF)
    eye = (ri == ci).astype(F32)
    for h in range(GDN_HEADS):
        a = -jnp.exp(alog_ref[h])
        grow_s[h] = _mm_exact_rhs(a * _softplus(gar_ref[0, h] + dtb_ref[h]), triu)
        gcol_s[h] = _mm_exact_lhs(tril, a * _softplus(gac_ref[0, h] + dtb_ref[h]))
        bcol_s[h] = _sigmoid(gbc_ref[0, h])
    s_s[...] = jnp.zeros_like(s_s)

    lane_nc = lax.broadcasted_iota(jnp.int32, (CHUNK, nc), 1)
    heads = range(GDN_HEADS)
    cat = lambda parts: jnp.concatenate(parts, axis=1)

    per_iter = 4

    def local_step(it, carry):
        r0 = pl.multiple_of(it * (per_iter * CHUNK), per_iter * CHUNK)
        rows = pl.ds(r0, per_iter * CHUNK)
        chains = [(c, h) for c in range(per_iter) for h in heads]
        csl = lambda c: slice(c * CHUNK, (c + 1) * CHUNK)
        hsl = lambda h: slice(h * hd, (h + 1) * hd)
        q_all, k_all, v_all = (r[rows, :].astype(F32) for r in (q_ref, k_ref, v_ref))
        q = [q_all[csl(c), hsl(h)] for c, h in chains]
        k = [k_all[csl(c), hsl(h)] for c, h in chains]
        v = [v_all[csl(c), hsl(h)] for c, h in chains]
        sel = [lane_nc == it * per_iter + c for c in range(per_iter)]
        g_c = [jnp.sum(jnp.where(sel[c], gcol_s[h], 0.0), axis=1, keepdims=True)
               for c, h in chains]
        b_c = [jnp.sum(jnp.where(sel[c], bcol_s[h], 0.0), axis=1, keepdims=True)
               for c, h in chains]
        g_r = [grow_s[h, pl.ds(it * per_iter + c, 1), :] for c, h in chains]
        decay = [jnp.exp(jnp.where(lower, gc - gr, -jnp.inf)) for gc, gr in zip(g_c, g_r)]
        kb = [x * bc for x, bc in zip(k, b_c)]
        kk = [_mm_nt(x, y) for x, y in zip(kb, k)]
        qk = [_mm_nt(x, y) for x, y in zip(q, k)]
        t_inv = _neumann_inverse([jnp.where(strict, x * d, 0.0) for x, d in zip(kk, decay)], eye)
        eg = [jnp.exp(gc) for gc in g_c]
        uw = [_mm(ti, jnp.concatenate([x * bc, y * e], axis=1))
              for ti, x, bc, y, e in zip(t_inv, v, b_c, kb, eg)]
        kd = [x * jnp.exp(gr[:, CHUNK - 1:CHUNK] - gc) for x, gr, gc in zip(k, g_r, g_c)]
        kuw = [_mm_tn(x, y) for x, y in zip(kd, uw)]
        grid = lambda parts: jnp.concatenate(
            [cat(parts[c * GDN_HEADS:(c + 1) * GDN_HEADS]) for c in range(per_iter)], axis=0)
        lhs = [jnp.concatenate([m[:, hd:], x[:, hd:], y * e], axis=0)
               for m, x, y, e in zip(kuw, uw, q, eg)]
        lhs_s[pl.ds(pl.multiple_of(it * (per_iter * lrows), per_iter * lrows),
                    per_iter * lrows), :] = grid(lhs).astype(BF)
        ku_s[pl.ds(pl.multiple_of(it * (per_iter * hd), per_iter * hd), per_iter * hd), :] = (
            grid([m[:, :hd] for m in kuw]))
        u_s[rows, :] = grid([x[:, :hd] for x in uw])
        qk_s[rows, :] = grid([x * d for x, d in zip(qk, decay)]).astype(BF)
        return carry

    lax.fori_loop(0, nc // per_iter, local_step, 0)

    hsl = lambda h: slice(h * hd, (h + 1) * hd)

    def chunk_rows(n, size):
        return pl.ds(n * size if isinstance(n, int) else pl.multiple_of(n * size, size), size)

    def advance(n):
        s_prev = [s_s[h] for h in heads]
        prod = [_mm(lhs_s[chunk_rows(n, lrows), hsl(h)], s_prev[h]) for h in heads]
        g_last = [grow_s[h, pl.ds(n, 1), :][:, CHUNK - 1:CHUNK] for h in heads]
        s_s[...] = jnp.stack([s_prev[h] * jnp.exp(g_last[h]) - prod[h][:hd]
                              + ku_s[chunk_rows(n, hd), hsl(h)] for h in heads])
        pend_s[...] = cat([prod[h][hd:] for h in heads])

    def emit(n):
        rows = chunk_rows(n, CHUNK)
        pend = pend_s[...]
        v_new = [u_s[rows, hsl(h)] - pend[:CHUNK, hsl(h)] for h in heads]
        qv = [_mm(qk_s[rows, h * CHUNK:(h + 1) * CHUNK], v_new[h]) for h in heads]
        outs = [_rms(pend[CHUNK:, hsl(h)] + qv[h], on_ref[...]) for h in heads]
        gate = _silu(gg_ref[rows, :].astype(F32))
        o_ref[rows, :] = (cat(outs) * gate).astype(o_ref.dtype)

    def state_step(n, carry):
        emit(n - 1)
        advance(n)
        return carry

    advance(0)
    lax.fori_loop(1, nc, state_step, 0)
    emit(nc - 1)


def _gdn(zb, ga_row, ga_col, gb_col, a_log, dt_bias, o_norm, b, t):
    n = zb.shape[0]
    nc = t // CHUNK
    width = GDN_HEADS * GDN_HEAD_DIM
    fox_blocks = 3 * FOX_HEADS * FOX_HEAD_DIM // width
    colspec = lambda j: pl.BlockSpec((t, width), lambda bi: (bi, fox_blocks + j))
    gspec = lambda shp: pl.BlockSpec((1,) + shp, lambda bi: (bi, 0, 0, 0))
    return pl.pallas_call(
        _gdn_kernel,
        grid=(b,),
        in_specs=[_const_spec(a_log.shape), _const_spec(dt_bias.shape),
                  colspec(0), colspec(1), colspec(2), colspec(3),
                  gspec((GDN_HEADS, nc, CHUNK)), gspec((GDN_HEADS, CHUNK, nc)),
                  gspec((GDN_HEADS, CHUNK, nc)), _const_spec(o_norm.shape)],
        out_specs=pl.BlockSpec((t, width), lambda bi: (bi, 0)),
        out_shape=jax.ShapeDtypeStruct((n, width), BF),
        scratch_shapes=[pltpu.VMEM((nc * (GDN_HEAD_DIM + 2 * CHUNK), width), BF),
                        pltpu.VMEM((nc * GDN_HEAD_DIM, width), F32),
                        pltpu.VMEM((t, width), F32),
                        pltpu.VMEM((t, GDN_HEADS * CHUNK), BF),
                        pltpu.VMEM((2 * CHUNK, width), F32),
                        pltpu.VMEM((GDN_HEADS, nc, CHUNK), F32),
                        pltpu.VMEM((GDN_HEADS, CHUNK, nc), F32),
                        pltpu.VMEM((GDN_HEADS, CHUNK, nc), F32),
                        pltpu.VMEM((GDN_HEADS, GDN_HEAD_DIM, GDN_HEAD_DIM), F32)],
        compiler_params=_cparams(("parallel",)),
        name="gated_deltanet",
    )(a_log, dt_bias, zb, zb, zb, zb, ga_row, ga_col, gb_col, o_norm)


def _mlp_kernel(a1_ref, a2_ref, x_ref, wo1_ref, wo2_ref, g_ref, w1_ref, w2_ref, o_ref, acc_ref,
                *, fc):
    mix = (jnp.dot(a1_ref[...], wo1_ref[...], preferred_element_type=F32)
           + jnp.dot(a2_ref[...], wo2_ref[...], preferred_element_type=F32))
    x1 = x_ref[...] + mix
    h = _rms(x1, g_ref[...]).astype(BF)
    acc_ref[...] = x1
    for c in range(w1_ref.shape[1] // fc):
        a = jnp.maximum(jnp.dot(h, w1_ref[:, c * fc:(c + 1) * fc],
                                preferred_element_type=F32), 0.0)
        acc_ref[...] += jnp.dot((a * a).astype(BF), w2_ref[c * fc:(c + 1) * fc, :],
                                preferred_element_type=F32)
    o_ref[...] = acc_ref[...]


def _mlp(a1, a2, a2_block, x2, wo1, wo2, gain, w1, w2, tm):
    n, d = x2.shape
    half = wo1.shape[0]
    return pl.pallas_call(
        functools.partial(_mlp_kernel, fc=1024),
        grid=(n // tm,),
        in_specs=[pl.BlockSpec((tm, half), lambda i: (i, 0)),
                  pl.BlockSpec((tm, half), lambda i: (i, a2_block)),
                  pl.BlockSpec((tm, d), lambda i: (i, 0)),
                  _const_spec(wo1.shape), _const_spec(wo2.shape), _const_spec(gain.shape),
                  _const_spec(w1.shape), _const_spec(w2.shape)],
        out_specs=pl.BlockSpec((tm, d), lambda i: (i, 0)),
        out_shape=jax.ShapeDtypeStruct((n, d), F32),
        scratch_shapes=[pltpu.VMEM((tm, d), F32)],
        compiler_params=_cparams(("parallel",)),
        name="out_proj_mlp",
    )(a1, a2, x2, wo1, wo2, gain, w1, w2)


def _in1_kernel(x_ref, g_ref, wb_ref, wf_ref, zb_ref, zf_ref):
    h = _rms(x_ref[...], g_ref[...]).astype(BF)
    gw = 512
    for c in range(wb_ref.shape[1] // gw):
        zb_ref[:, c * gw:(c + 1) * gw] = jnp.dot(
            h, wb_ref[:, c * gw:(c + 1) * gw], preferred_element_type=F32).astype(BF)
    for c in range(wf_ref.shape[1] // gw):
        zf_ref[:, c * gw:(c + 1) * gw] = jnp.dot(
            h, wf_ref[:, c * gw:(c + 1) * gw], preferred_element_type=F32)


def _in1(x2, gain, wb, wf, tm):
    n, d = x2.shape
    return pl.pallas_call(
        _in1_kernel,
        grid=(n // tm,),
        in_specs=[pl.BlockSpec((tm, d), lambda i: (i, 0)),
                  _const_spec(gain.shape), _const_spec(wb.shape), _const_spec(wf.shape)],
        out_specs=[pl.BlockSpec((tm, wb.shape[1]), lambda i: (i, 0)),
                   pl.BlockSpec((tm, wf.shape[1]), lambda i: (i, 0))],
        out_shape=[jax.ShapeDtypeStruct((n, wb.shape[1]), BF),
                   jax.ShapeDtypeStruct((n, wf.shape[1]), F32)],
        compiler_params=_cparams(("parallel",)),
        name="l1_in_proj",
    )(x2, gain, wb, wf)


HG_SUB = SUBLANES
HG_GROUP = 4


def _hgrn_intra_operands(qs, ks, bs, key_row):
    nb = CHUNK // HG_SUB
    d = qs[0].shape[1]
    blk = lambda x, i: x[HG_SUB * i:HG_SUB * (i + 1)]

    wide = []
    for c, (q, b) in enumerate(zip(qs, bs)):
        cols = []
        for s in range(HG_SUB):
            parts = []
            for i in range(nb):
                r = HG_SUB * i + s
                e = jnp.exp2(jnp.minimum(blk(b, i) - key_row(c, r, 0), 0.0))
                parts.append(blk(q, i) * e * key_row(c, r, 1))
            cols.append(jnp.concatenate(parts, axis=0))
        wide.append(jnp.concatenate(cols, axis=1).astype(BF))

    zero = jnp.zeros((HG_SUB, d), F32)
    lhs_all, rhs_all = [], []
    for q, k, b in zip(qs, ks, bs):
        lhs, rhs = [], []
        for j in range(1, nb // 2):
            bref = b[2 * HG_SUB * j:2 * HG_SUB * j + 1, :]
            lg, rg = [zero] * nb, [zero] * nb
            for i in (2 * j, 2 * j + 1):
                lg[i] = blk(q, i) * jnp.exp2(blk(b, i) - bref)
            for i in range(2 * j):
                rg[i] = blk(k, i) * jnp.exp2(bref - blk(b, i))
            lhs.append(jnp.concatenate(lg, axis=0))
            rhs.append(jnp.concatenate(rg, axis=0))
        for j in range(nb // 2):
            bref = b[HG_SUB * (2 * j + 1):HG_SUB * (2 * j + 1) + 1, :]
            lg, rg = [zero] * nb, [zero] * nb
            lg[2 * j + 1] = blk(q, 2 * j + 1) * jnp.exp2(blk(b, 2 * j + 1) - bref)
            rg[2 * j] = blk(k, 2 * j) * jnp.exp2(bref - blk(b, 2 * j))
            lhs.append(jnp.concatenate(lg, axis=0))
            rhs.append(jnp.concatenate(rg, axis=0))
        lhs_all.append(jnp.concatenate(lhs, axis=1).astype(BF))
        rhs_all.append(jnp.concatenate(rhs, axis=1).astype(BF))
    return wide, lhs_all, rhs_all


def _hgrn_kernel(zq_ref, zi_ref, zg_ref, zf_ref, lbl_ref, on_ref, o_ref,
                 oi_s, qd_s, kv_s, dec_s, h1_s, wide_s, lhs_s, rhs_s, kdec_s, *, group):
    t = zq_ref.shape[0]
    d = zq_ref.shape[1]
    gr = group * CHUNK
    ng = t // gr
    ri = lax.broadcasted_iota(jnp.int32, (gr, gr), 0)
    ci = lax.broadcasted_iota(jnp.int32, (gr, gr), 1)
    tril = ((ri >= ci) & (ri // CHUNK == ci // CHUNK)).astype(BF)
    row = lax.broadcasted_iota(jnp.int32, (CHUNK, CHUNK), 0)
    col = lax.broadcasted_iota(jnp.int32, (CHUNK, CHUNK), 1)
    own_block = (row // HG_SUB == col // HG_SUB) & (col % HG_SUB <= row % HG_SUB)
    gather = (lax.broadcasted_iota(jnp.int32, (CHUNK, HG_SUB * d), 0) % HG_SUB
              == lax.broadcasted_iota(jnp.int32, (CHUNK, HG_SUB * d), 1) // d).astype(BF)
    sls = [slice(c * CHUNK, (c + 1) * CHUNK) for c in range(group)]

    l0 = lbl_ref[0:1, :]
    l1 = lbl_ref[1:2, :]
    mx = jnp.maximum(l0, l1)
    e0 = jnp.exp(l0 - mx)
    e1 = jnp.exp(l1 - mx)
    s0 = e0 / (e0 + e1)
    s1 = e1 / (e0 + e1)
    lb = (s0 + s1) - s0

    def group_rows(g):
        start = g * gr if isinstance(g, int) else pl.multiple_of(g * gr, gr)
        return pl.ds(start, gr)

    def prelude(g, slot):
        rows = group_rows(g)
        q = _silu(zq_ref[rows, :].astype(F32))
        f = lb + (1.0 - lb) * _sigmoid(zf_ref[rows, :])
        k = 1.0 - f
        b = _mm_exact_lhs(tril, jnp.log2(f))
        h1_s[slot, 0] = b
        h1_s[slot, 1] = k
        h1_s[slot, 2] = q
        qd_s[rows, :] = (q * jnp.exp2(b)).astype(BF)

    def operands(g, slot):
        b, k, q = h1_s[slot, 0], h1_s[slot, 1], h1_s[slot, 2]
        key_row = lambda c, r, which: h1_s[slot, which, c * CHUNK + r:c * CHUNK + r + 1, :]
        bc = [b[sl] for sl in sls]
        b_last = [x[CHUNK - 1:CHUNK, :] for x in bc]
        wide, lhs, rhs = _hgrn_intra_operands([q[sl] for sl in sls], [k[sl] for sl in sls],
                                              bc, key_row)
        kdec = [k[sl] * jnp.exp2(bl - x) for sl, bl, x in zip(sls, b_last, bc)]
        wide_s[slot] = jnp.concatenate(wide, axis=0)
        lhs_s[slot] = jnp.concatenate(lhs, axis=0)
        rhs_s[slot] = jnp.concatenate(rhs, axis=0)
        kdec_s[slot] = jnp.concatenate(kdec, axis=0).astype(BF)
        dec_s[pl.ds(g * group, group)] = jnp.stack([jnp.exp2(bl) for bl in b_last])

    def products(g, slot):
        rows = group_rows(g)
        v = zi_ref[rows, :]
        diag = [_mm_nt(wide_s[slot, sl, :], gather) for sl in sls]
        off = [_mm_nt(lhs_s[slot, sl, :], rhs_s[slot, sl, :]) for sl in sls]
        kv = [_mm_tn(v[sl], kdec_s[slot, sl, :]) for sl in sls]
        dmat = [jnp.where(own_block, dg, 0.0) + of for dg, of in zip(diag, off)]
        oi_s[rows, :] = jnp.concatenate([_mm(dm, v[sl]) for dm, sl in zip(dmat, sls)], axis=0)
        kv_s[pl.ds(g * group, group)] = jnp.stack(kv)

    def step(i, slot, stages):
        if "products" in stages:
            products(i - 2, slot)
        if "operands" in stages:
            operands(i - 1, 1 - slot)
        if "prelude" in stages:
            prelude(i, slot)

    assert ng % 2 == 0
    step(0, 0, ("prelude",))
    step(1, 1, ("operands", "prelude"))

    def steady(j, carry):
        i = 2 + 2 * j
        step(i, 0, ("products", "operands", "prelude"))
        step(i + 1, 1, ("products", "operands", "prelude"))
        return carry

    lax.fori_loop(0, (ng - 2) // 2, steady, 0)
    step(ng, 0, ("products", "operands"))
    step(ng + 1, 1, ("products",))

    per_step = 4 * group
    sr = per_step * CHUNK

    def state_step(g, st):
        r0 = pl.multiple_of(g * sr, sr)
        states = []
        for c in range(per_step):
            n = g * per_step + c
            states.append(st)
            st = st * dec_s[n] + kv_s[n]
        outs = [oi_s[pl.ds(r0 + c * CHUNK, CHUNK), :]
                + _mm_nt(qd_s[pl.ds(r0 + c * CHUNK, CHUNK), :], states[c])
                for c in range(per_step)]
        rows = pl.ds(r0, sr)
        o = _rms(jnp.concatenate(outs, axis=0), on_ref[...])
        o_ref[rows, :] = (o * _silu(zg_ref[rows, :].astype(F32))).astype(o_ref.dtype)
        return st

    lax.fori_loop(0, t // sr, state_step, jnp.zeros((d, d), F32))


def _hgrn(zb, zf, lb_logits, o_norm, b, t):
    n = zb.shape[0]
    hd = HGRN_HEAD_DIM
    nh = HGRN_HEADS
    nc = t // CHUNK
    gr = HG_GROUP * CHUNK
    return pl.pallas_call(
        functools.partial(_hgrn_kernel, group=HG_GROUP),
        grid=(b, nh),
        in_specs=[pl.BlockSpec((t, hd), lambda bi, h: (bi, h)),
                  pl.BlockSpec((t, hd), lambda bi, h: (bi, nh + h)),
                  pl.BlockSpec((t, hd), lambda bi, h: (bi, 2 * nh + h)),
                  pl.BlockSpec((t, hd), lambda bi, h: (bi, h)),
                  pl.BlockSpec((lb_logits.shape[0], hd), lambda bi, h: (0, h)),
                  _const_spec(o_norm.shape)],
        out_specs=pl.BlockSpec((t, hd), lambda bi, h: (bi, h)),
        out_shape=jax.ShapeDtypeStruct((n, nh * hd), BF),
        scratch_shapes=[pltpu.VMEM((t, hd), F32), pltpu.VMEM((t, hd), BF),
                        pltpu.VMEM((nc, hd, hd), F32), pltpu.VMEM((nc, 1, hd), F32),
                        pltpu.VMEM((2, 3, gr, hd), F32),
                        pltpu.VMEM((2, gr, HG_SUB * hd), BF),
                        pltpu.VMEM((2, gr, (HG_SUB - 1) * hd), BF),
                        pltpu.VMEM((2, gr, (HG_SUB - 1) * hd), BF),
                        pltpu.VMEM((2, gr, hd), BF)],
        compiler_params=_cparams(("parallel", "parallel")),
        name="hgrn2",
    )(zb, zb, zb, zf, lb_logits, o_norm)


def kernel(x, l0_mix_norm, l0_w_in, l0_fox_q_norm, l0_fox_k_norm, l0_fox_f_bias, l0_gdn_conv,
           l0_gdn_A_log, l0_gdn_dt_bias, l0_gdn_o_norm, l0_w_out, l0_ffn_norm, l0_w_ff1,
           l0_w_ff2, l1_mix_norm, l1_w_in, l1_hgrn_o_norm, l1_w_out, l1_ffn_norm, l1_w_ff1,
           l1_w_ff2, hgrn_lb_logits):
    b, t, d = x.shape
    n = b * t
    fw = FOX_HEADS * FOX_HEAD_DIM
    gw = GDN_HEADS * GDN_HEAD_DIM
    assert hgrn_lb_logits.shape[0] == 2 and d == HGRN_HEADS * HGRN_HEAD_DIM
    assert t % (2 * HG_GROUP * CHUNK) == 0
    tm = 512 if t % 512 == 0 else 256
    tq = 256
    nc = t // CHUNK
    row = lambda p: p.reshape(1, -1).astype(F32)
    x2 = x.reshape(n, d)

    o_ff = 3 * fw
    o_gq = o_ff + FOX_HEADS
    o_gb = o_gq + 3 * gw
    o_gg = o_gb + 2 * GDN_HEADS
    wb0 = jnp.concatenate([l0_w_in[:, :o_ff], l0_w_in[:, o_gq:o_gb], l0_w_in[:, o_gg:]],
                          axis=1).astype(BF)
    n_small = FOX_HEADS + 2 * GDN_HEADS
    ws0 = jnp.concatenate([l0_w_in[:, o_ff:o_gq], l0_w_in[:, o_gb:o_gg],
                           jnp.zeros((d, LANES - n_small), F32)], axis=1).astype(BF)
    hid = jnp.arange(fw) // FOX_HEAD_DIM
    pm = jnp.where(hid[:, None] == hid[None, :], 1.0 / FOX_HEAD_DIM, 0.0).astype(BF)
    qkg = jnp.stack([jnp.tile(l0_fox_q_norm, FOX_HEADS) * (FOX_HEAD_DIM ** -0.5 * LOG2E),
                     jnp.tile(l0_fox_k_norm, FOX_HEADS)]).astype(F32)

    zb0, zs0 = _in0(x2, row(l0_mix_norm), wb0, ws0, pm, qkg, l0_gdn_conv.astype(F32), tm, t)

    f_bias = jnp.pad(l0_fox_f_bias.astype(F32), (0, LANES - FOX_HEADS)).reshape(1, LANES)
    qbias, kbias = _fox_cum(zs0, f_bias, b, t)
    fox_o = _fox_attention(zb0, qbias, kbias, b, t, tq)

    def gate_rows(lo):
        g = zs0[:, lo:lo + GDN_HEADS].reshape(b, t, GDN_HEADS).transpose(0, 2, 1)
        return g.reshape(b, GDN_HEADS, nc, CHUNK)
    gb_row = gate_rows(FOX_HEADS)
    ga_row = gate_rows(FOX_HEADS + GDN_HEADS)
    gdn_o = _gdn(zb0, ga_row, ga_row.swapaxes(-1, -2), gb_row.swapaxes(-1, -2),
                 l0_gdn_A_log.reshape(GDN_HEADS, 1, 1).astype(F32),
                 l0_gdn_dt_bias.reshape(GDN_HEADS, 1, 1).astype(F32), row(l0_gdn_o_norm), b, t)

    wo0 = l0_w_out.astype(BF)
    x1 = _mlp(fox_o, gdn_o, 0, x2, wo0[:fw], wo0[fw:], row(l0_ffn_norm),
              l0_w_ff1.astype(BF), l0_w_ff2.astype(BF), tm)

    wb1 = jnp.concatenate([l1_w_in[:, :d], l1_w_in[:, 2 * d:]], axis=1).astype(BF)
    wf1 = l1_w_in[:, d:2 * d].astype(BF)
    zb1, zf1 = _in1(x1, row(l1_mix_norm), wb1, wf1, tm)
    ho = _hgrn(zb1, zf1, hgrn_lb_logits.astype(F32), row(l1_hgrn_o_norm), b, t)
    wo1 = l1_w_out.astype(BF)
    half = d // 2
    x3 = _mlp(ho, ho, 1, x1, wo1[:half], wo1[half:], row(l1_ffn_norm),
              l1_w_ff1.astype(BF), l1_w_ff2.astype(BF), tm)
    return x3.reshape(b, t, d)
```

```python
import functools

import jax
import jax.numpy as jnp
from jax import lax
from jax.experimental import pallas as pl
from jax.experimental.pallas import tpu as pltpu

BF = jnp.bfloat16
F32 = jnp.float32
NORM_EPS = 1e-6
LOG2E = 1.4426950408889634
CHUNK = 64
FOX_HEADS, FOX_HEAD_DIM = 8, 64
GDN_HEADS, GDN_HEAD_DIM = 4, 128
HGRN_HEADS, HGRN_HEAD_DIM = 8, 128
GDN_CONV = 4
IN0_ROWS = 128
LANES = 128
SUBLANES = 8
VMEM_LIMIT = 56 * 1024 * 1024


def _cparams(sem):
    return pltpu.CompilerParams(dimension_semantics=sem, vmem_limit_bytes=VMEM_LIMIT)


def _const_spec(shape):
    nd = len(shape)
    return pl.BlockSpec(shape, lambda *_: (0,) * nd, pipeline_mode=pl.Buffered(1))


def _mm(a, b):
    return jnp.dot(a.astype(BF), b.astype(BF), preferred_element_type=F32)


def _mm_nt(a, b):
    return lax.dot_general(a.astype(BF), b.astype(BF), (((1,), (1,)), ((), ())),
                           preferred_element_type=F32)


def _mm_tn(a, b):
    return lax.dot_general(a.astype(BF), b.astype(BF), (((0,), (0,)), ((), ())),
                           preferred_element_type=F32)


def _split3(x):
    hi = x.astype(BF)
    r1 = x - hi.astype(F32)
    mid = r1.astype(BF)
    lo = (r1 - mid.astype(F32)).astype(BF)
    return hi, mid, lo


def _mm_exact_lhs(m_bf, x):
    n = x.shape[1]
    if n % LANES:
        hi, mid, lo = _split3(x)
        d = lambda p: jnp.dot(m_bf, p, preferred_element_type=F32)
        return d(hi) + d(mid) + d(lo)
    y = jnp.dot(m_bf, jnp.concatenate(_split3(x), axis=1), preferred_element_type=F32)
    return y[:, :n] + y[:, n:2 * n] + y[:, 2 * n:]


def _mm_exact_rhs(x, m_bf):
    hi, mid, lo = _split3(x)
    d = lambda p: jnp.dot(p, m_bf, preferred_element_type=F32)
    return d(hi) + d(mid) + d(lo)


def _sigmoid(x):
    return 1.0 / (1.0 + jnp.exp(-x))


def _silu(x):
    return x * _sigmoid(x)


def _softplus(x):
    return jnp.maximum(x, 0.0) + jnp.log1p(jnp.exp(-jnp.abs(x)))


def _rms(x, gain):
    return x * lax.rsqrt(jnp.mean(x * x, axis=-1, keepdims=True) + NORM_EPS) * gain


def _in0_kernel(x_ref, g_ref, wb_ref, ws_ref, pm_ref, qkg_ref, cw_ref, zb_ref, zs_ref, hist_ref,
                *, tiles_per_seq):
    tm = x_ref.shape[0]
    hd = GDN_HEAD_DIM
    hrows = hist_ref.shape[1]

    @pl.when(pl.program_id(0) % tiles_per_seq == 0)
    def _():
        hist_ref[...] = jnp.zeros_like(hist_ref)

    h = _rms(x_ref[...], g_ref[...]).astype(BF)
    gw = 512
    rb = IN0_ROWS

    def project(c, r):
        return jnp.dot(h[r * rb:(r + 1) * rb], wb_ref[:, c * gw:(c + 1) * gw],
                       preferred_element_type=F32)

    def finish(c, r, z, above):
        if c < 2:
            ms = jnp.dot((z * z).astype(BF), pm_ref[...], preferred_element_type=F32)
            z = z * lax.rsqrt(ms + NORM_EPS) * qkg_ref[c:c + 1, :]
        elif 3 <= c < 6:
            j = c - 3
            w = cw_ref[:, j * gw:(j + 1) * gw]
            ext = jnp.concatenate([above, z], axis=0)
            y = z * w[GDN_CONV - 1:GDN_CONV]
            for s in range(1, GDN_CONV):
                y = y + ext[hrows - s:hrows - s + rb] * w[GDN_CONV - 1 - s:GDN_CONV - s]
            z = _silu(y)
            if j < 2:
                scale = 1.0 if j else hd ** -0.5
                z = jnp.concatenate(
                    [zh * (lax.rsqrt(jnp.sum(zh * zh, axis=-1, keepdims=True) + NORM_EPS) * scale)
                     for zh in (z[:, k * hd:(k + 1) * hd] for k in range(gw // hd))], axis=1)
        zb_ref[r * rb:(r + 1) * rb, c * gw:(c + 1) * gw] = z.astype(BF)

    pending = None
    for c in range(wb_ref.shape[1] // gw):
        conv = 3 <= c < 6
        above = hist_ref[c - 3] if conv else None
        for r in range(tm // rb):
            z = project(c, r)
            if pending is not None:
                finish(*pending)
            pending = (c, r, z, above)
            above = z[rb - hrows:, :] if conv else None
        if conv:
            hist_ref[c - 3] = above
    finish(*pending)
    zs_ref[...] = jnp.dot(h, ws_ref[...], preferred_element_type=F32)


def _in0(x2, gain, wb, ws, pm, qkg, conv_w, tm, t):
    n, d = x2.shape
    nb = wb.shape[1]
    return pl.pallas_call(
        functools.partial(_in0_kernel, tiles_per_seq=t // tm),
        grid=(n // tm,),
        in_specs=[pl.BlockSpec((tm, d), lambda i: (i, 0)),
                  _const_spec((1, d)), _const_spec(wb.shape), _const_spec(ws.shape),
                  _const_spec(pm.shape), _const_spec(qkg.shape), _const_spec(conv_w.shape)],
        out_specs=[pl.BlockSpec((tm, nb), lambda i: (i, 0)),
                   pl.BlockSpec((tm, LANES), lambda i: (i, 0))],
        out_shape=[jax.ShapeDtypeStruct((n, nb), BF), jax.ShapeDtypeStruct((n, LANES), F32)],
        scratch_shapes=[pltpu.VMEM((3, SUBLANES, 512), F32)],
        compiler_params=_cparams(("arbitrary",)),
        name="l0_in_proj",
    )(x2, gain, wb, ws, pm, qkg, conv_w)


FOX_BIAS_LANES = 8


def _fox_cum_kernel(zs_ref, bias_ref, qb_ref, kb_ref, *, blk):
    t = zs_ref.shape[0]
    r = lax.broadcasted_iota(jnp.int32, (blk, blk), 0)
    c = lax.broadcasted_iota(jnp.int32, (blk, blk), 1)
    tril = (r >= c).astype(BF)
    pr = lax.broadcasted_iota(jnp.int32, (3 * LANES, 2 * LANES), 0)
    pc = lax.broadcasted_iota(jnp.int32, (3 * LANES, 2 * LANES), 1)
    place = ((pr % LANES < FOX_HEADS)
             & (pc % LANES == (pr % LANES) * FOX_BIAS_LANES + pr // LANES + 3 * (pc // LANES))
             ).astype(BF)
    lane = lax.broadcasted_iota(jnp.int32, (1, LANES), 1)
    slot = lane % FOX_BIAS_LANES
    used = lane < FOX_HEADS * FOX_BIAS_LANES
    ones_q = jnp.where(used & (slot >= 3) & (slot < 6), 1.0, 0.0)
    ones_k = jnp.where(used & (slot < 3), 1.0, 0.0)
    blocks = [slice(i * blk, (i + 1) * blk) for i in range(t // blk)]
    local = []
    for rows in blocks:
        z = zs_ref[rows, :] + bias_ref[...]
        ls = jnp.minimum(z, 0.0) - jnp.log1p(jnp.exp(-jnp.abs(z)))
        local.append(_mm_exact_lhs(tril, ls))
    carry = jnp.zeros((1, LANES), F32)
    for rows, loc in zip(blocks, local):
        cb = loc + carry
        carry = cb[blk - 1:blk, :]
        feat = jnp.dot(jnp.concatenate(_split3(cb * LOG2E), axis=1), place,
                       preferred_element_type=F32)
        qb_ref[rows, :] = (feat[:, :LANES] + ones_q).astype(BF)
        kb_ref[rows, :] = (ones_k - feat[:, LANES:]).astype(BF)


def _fox_cum(zs, bias_row, b, t):
    n = zs.shape[0]
    spec = pl.BlockSpec((t, LANES), lambda bi: (bi, 0))
    return pl.pallas_call(
        functools.partial(_fox_cum_kernel, blk=256),
        grid=(b,),
        in_specs=[spec, _const_spec(bias_row.shape)],
        out_specs=[spec, spec],
        out_shape=[jax.ShapeDtypeStruct((n, LANES), BF)] * 2,
        compiler_params=_cparams(("parallel",)),
        name="fox_cum_gate",
    )(zs, bias_row)


def _fox_kernel(q_ref, k_ref, v_ref, qb_ref, kb_ref, o_ref, *, tq):
    pair = pl.program_id(1)
    t = q_ref.shape[0]
    lane = lax.broadcasted_iota(jnp.int32, (1, LANES), 1)
    zero = jnp.zeros((tq, LANES), BF)
    first = lane < FOX_HEAD_DIM
    bias_head = lane // FOX_BIAS_LANES
    ones = jnp.ones((tq, LANES), BF)
    on_diag = (lax.broadcasted_iota(jnp.int32, (2 * tq, tq), 1)
               <= lax.broadcasted_iota(jnp.int32, (2 * tq, tq), 0) % tq)

    def scores(i):
        rows_i = slice(i * tq, (i + 1) * tq)
        kv = (i + 1) * tq
        q2, qb = q_ref[rows_i, :], qb_ref[rows_i, :]
        rows = []
        for h in range(2):
            qh = jnp.where(first if h == 0 else jnp.logical_not(first), q2, zero)
            qbh = jnp.where(bias_head == 2 * pair + h, qb, zero)
            rows.append(jnp.concatenate([qh, qbh], axis=1))
        ka = jnp.concatenate([k_ref[:kv, :], kb_ref[:kv, :]], axis=1)
        s = lax.dot_general(jnp.concatenate(rows, axis=0), ka, (((1,), (1,)), ((), ())),
                            preferred_element_type=F32)
        diag = jnp.where(on_diag, s[:, kv - tq:], -jnp.inf)
        return diag if i == 0 else jnp.concatenate([s[:, :kv - tq], diag], axis=1)

    def finish(i, s):
        kv = (i + 1) * tq
        p = jnp.exp2(s - jnp.max(s, axis=-1, keepdims=True)).astype(BF)
        va = jnp.concatenate([v_ref[:kv, :], jnp.concatenate([ones] * (i + 1), axis=0)], axis=1)
        acc = jnp.dot(p, va, preferred_element_type=F32)
        o = acc[:, :LANES] / acc[:, LANES:]
        o_ref[i * tq:(i + 1) * tq, :] = jnp.where(first, o[:tq], o[tq:]).astype(o_ref.dtype)

    s = scores(0)
    for i in range(t // tq):
        s_next = scores(i + 1) if (i + 1) * tq < t else None
        finish(i, s)
        s = s_next


def _fox_attention(zb, qbias, kbias, b, t, tq):
    n = zb.shape[0]
    hp = FOX_HEADS // 2
    return pl.pallas_call(
        functools.partial(_fox_kernel, tq=tq),
        grid=(b, hp),
        in_specs=[pl.BlockSpec((t, LANES), lambda bi, p: (bi, p)),
                  pl.BlockSpec((t, LANES), lambda bi, p: (bi, hp + p)),
                  pl.BlockSpec((t, LANES), lambda bi, p: (bi, 2 * hp + p)),
                  pl.BlockSpec((t, LANES), lambda bi, p: (bi, 0)),
                  pl.BlockSpec((t, LANES), lambda bi, p: (bi, 0))],
        out_specs=pl.BlockSpec((t, LANES), lambda bi, p: (bi, p)),
        out_shape=jax.ShapeDtypeStruct((n, FOX_HEADS * FOX_HEAD_DIM), BF),
        compiler_params=_cparams(("parallel", "parallel")),
        name="fox_attention",
    )(zb, zb, zb, qbias, kbias)


def _neumann_inverse(mats, eye):
    ps = [eye - a for a in mats]
    aps = list(mats)
    k = 2
    while k < CHUNK:
        aps = [_mm(ap, ap) for ap in aps]
        ps = [p + _mm(p, ap) for p, ap in zip(ps, aps)]
        k *= 2
    return ps


def _gdn_kernel(alog_ref, dtb_ref, q_ref, k_ref, v_ref, gg_ref, gar_ref, gac_ref, gbc_ref,
                on_ref, o_ref, lhs_s, ku_s, u_s, qk_s, pend_s, grow_s, gcol_s, bcol_s, s_s):
    t = q_ref.shape[0]
    nc = t // CHUNK
    hd = GDN_HEAD_DIM
    lrows = hd + 2 * CHUNK

    ri = lax.broadcasted_iota(jnp.int32, (CHUNK, CHUNK), 0)
    ci = lax.broadcasted_iota(jnp.int32, (CHUNK, CHUNK), 1)
    lower = ri >= ci
    strict = ri > ci
    tril = lower.astype(BF)
    triu = (ri <= ci).astype(BF)
    eye = (ri == ci).astype(F32)
    for h in range(GDN_HEADS):
        a = -jnp.exp(alog_ref[h])
        grow_s[h] = _mm_exact_rhs(a * _softplus(gar_ref[0, h] + dtb_ref[h]), triu)
        gcol_s[h] = _mm_exact_lhs(tril, a * _softplus(gac_ref[0, h] + dtb_ref[h]))
        bcol_s[h] = _sigmoid(gbc_ref[0, h])
    s_s[...] = jnp.zeros_like(s_s)

    lane_nc = lax.broadcasted_iota(jnp.int32, (CHUNK, nc), 1)
    heads = range(GDN_HEADS)
    cat = lambda parts: jnp.concatenate(parts, axis=1)

    per_iter = 4

    def local_step(it, carry):
        r0 = pl.multiple_of(it * (per_iter * CHUNK), per_iter * CHUNK)
        rows = pl.ds(r0, per_iter * CHUNK)
        chains = [(c, h) for c in range(per_iter) for h in heads]
        csl = lambda c: slice(c * CHUNK, (c + 1) * CHUNK)
        hsl = lambda h: slice(h * hd, (h + 1) * hd)
        q_all, k_all, v_all = (r[rows, :].astype(F32) for r in (q_ref, k_ref, v_ref))
        q = [q_all[csl(c), hsl(h)] for c, h in chains]
        k = [k_all[csl(c), hsl(h)] for c, h in chains]
        v = [v_all[csl(c), hsl(h)] for c, h in chains]
        sel = [lane_nc == it * per_iter + c for c in range(per_iter)]
        g_c = [jnp.sum(jnp.where(sel[c], gcol_s[h], 0.0), axis=1, keepdims=True)
               for c, h in chains]
        b_c = [jnp.sum(jnp.where(sel[c], bcol_s[h], 0.0), axis=1, keepdims=True)
               for c, h in chains]
        g_r = [grow_s[h, pl.ds(it * per_iter + c, 1), :] for c, h in chains]
        decay = [jnp.exp(jnp.where(lower, gc - gr, -jnp.inf)) for gc, gr in zip(g_c, g_r)]
        kb = [x * bc for x, bc in zip(k, b_c)]
        kk = [_mm_nt(x, y) for x, y in zip(kb, k)]
        qk = [_mm_nt(x, y) for x, y in zip(q, k)]
        t_inv = _neumann_inverse([jnp.where(strict, x * d, 0.0) for x, d in zip(kk, decay)], eye)
        eg = [jnp.exp(gc) for gc in g_c]
        uw = [_mm(ti, jnp.concatenate([x * bc, y * e], axis=1))
              for ti, x, bc, y, e in zip(t_inv, v, b_c, kb, eg)]
        kd = [x * jnp.exp(gr[:, CHUNK - 1:CHUNK] - gc) for x, gr, gc in zip(k, g_r, g_c)]
        kuw = [_mm_tn(x, y) for x, y in zip(kd, uw)]
        grid = lambda parts: jnp.concatenate(
            [cat(parts[c * GDN_HEADS:(c + 1) * GDN_HEADS]) for c in range(per_iter)], axis=0)
        lhs = [jnp.concatenate([m[:, hd:], x[:, hd:], y * e], axis=0)
               for m, x, y, e in zip(kuw, uw, q, eg)]
        lhs_s[pl.ds(pl.multiple_of(it * (per_iter * lrows), per_iter * lrows),
                    per_iter * lrows), :] = grid(lhs).astype(BF)
        ku_s[pl.ds(pl.multiple_of(it * (per_iter * hd), per_iter * hd), per_iter * hd), :] = (
            grid([m[:, :hd] for m in kuw]))
        u_s[rows, :] = grid([x[:, :hd] for x in uw])
        qk_s[rows, :] = grid([x * d for x, d in zip(qk, decay)]).astype(BF)
        return carry

    lax.fori_loop(0, nc // per_iter, local_step, 0)

    hsl = lambda h: slice(h * hd, (h + 1) * hd)

    def chunk_rows(n, size):
        return pl.ds(n * size if isinstance(n, int) else pl.multiple_of(n * size, size), size)

    def advance(n):
        s_prev = [s_s[h] for h in heads]
        prod = [_mm(lhs_s[chunk_rows(n, lrows), hsl(h)], s_prev[h]) for h in heads]
        g_last = [grow_s[h, pl.ds(n, 1), :][:, CHUNK - 1:CHUNK] for h in heads]
        s_s[...] = jnp.stack([s_prev[h] * jnp.exp(g_last[h]) - prod[h][:hd]
                              + ku_s[chunk_rows(n, hd), hsl(h)] for h in heads])
        pend_s[...] = cat([prod[h][hd:] for h in heads])

    def emit(n):
        rows = chunk_rows(n, CHUNK)
        pend = pend_s[...]
        v_new = [u_s[rows, hsl(h)] - pend[:CHUNK, hsl(h)] for h in heads]
        qv = [_mm(qk_s[rows, h * CHUNK:(h + 1) * CHUNK], v_new[h]) for h in heads]
        outs = [_rms(pend[CHUNK:, hsl(h)] + qv[h], on_ref[...]) for h in heads]
        gate = _silu(gg_ref[rows, :].astype(F32))
        o_ref[rows, :] = (cat(outs) * gate).astype(o_ref.dtype)

    def state_step(n, carry):
        emit(n - 1)
        advance(n)
        return carry

    advance(0)
    lax.fori_loop(1, nc, state_step, 0)
    emit(nc - 1)


def _gdn(zb, ga_row, ga_col, gb_col, a_log, dt_bias, o_norm, b, t):
    n = zb.shape[0]
    nc = t // CHUNK
    width = GDN_HEADS * GDN_HEAD_DIM
    fox_blocks = 3 * FOX_HEADS * FOX_HEAD_DIM // width
    colspec = lambda j: pl.BlockSpec((t, width), lambda bi: (bi, fox_blocks + j))
    gspec = lambda shp: pl.BlockSpec((1,) + shp, lambda bi: (bi, 0, 0, 0))
    return pl.pallas_call(
        _gdn_kernel,
        grid=(b,),
        in_specs=[_const_spec(a_log.shape), _const_spec(dt_bias.shape),
                  colspec(0), colspec(1), colspec(2), colspec(3),
                  gspec((GDN_HEADS, nc, CHUNK)), gspec((GDN_HEADS, CHUNK, nc)),
                  gspec((GDN_HEADS, CHUNK, nc)), _const_spec(o_norm.shape)],
        out_specs=pl.BlockSpec((t, width), lambda bi: (bi, 0)),
        out_shape=jax.ShapeDtypeStruct((n, width), BF),
        scratch_shapes=[pltpu.VMEM((nc * (GDN_HEAD_DIM + 2 * CHUNK), width), BF),
                        pltpu.VMEM((nc * GDN_HEAD_DIM, width), F32),
                        pltpu.VMEM((t, width), F32),
                        pltpu.VMEM((t, GDN_HEADS * CHUNK), BF),
                        pltpu.VMEM((2 * CHUNK, width), F32),
                        pltpu.VMEM((GDN_HEADS, nc, CHUNK), F32),
                        pltpu.VMEM((GDN_HEADS, CHUNK, nc), F32),
                        pltpu.VMEM((GDN_HEADS, CHUNK, nc), F32),
                        pltpu.VMEM((GDN_HEADS, GDN_HEAD_DIM, GDN_HEAD_DIM), F32)],
        compiler_params=_cparams(("parallel",)),
        name="gated_deltanet",
    )(a_log, dt_bias, zb, zb, zb, zb, ga_row, ga_col, gb_col, o_norm)


def _mlp_kernel(a1_ref, a2_ref, x_ref, wo1_ref, wo2_ref, g_ref, w1_ref, w2_ref, o_ref, acc_ref,
                *, fc):
    mix = (jnp.dot(a1_ref[...], wo1_ref[...], preferred_element_type=F32)
           + jnp.dot(a2_ref[...], wo2_ref[...], preferred_element_type=F32))
    x1 = x_ref[...] + mix
    h = _rms(x1, g_ref[...]).astype(BF)
    acc_ref[...] = x1
    for c in range(w1_ref.shape[1] // fc):
        a = jnp.maximum(jnp.dot(h, w1_ref[:, c * fc:(c + 1) * fc],
                                preferred_element_type=F32), 0.0)
        acc_ref[...] += jnp.dot((a * a).astype(BF), w2_ref[c * fc:(c + 1) * fc, :],
                                preferred_element_type=F32)
    o_ref[...] = acc_ref[...]


def _mlp(a1, a2, a2_block, x2, wo1, wo2, gain, w1, w2, tm):
    n, d = x2.shape
    half = wo1.shape[0]
    return pl.pallas_call(
        functools.partial(_mlp_kernel, fc=1024),
        grid=(n // tm,),
        in_specs=[pl.BlockSpec((tm, half), lambda i: (i, 0)),
                  pl.BlockSpec((tm, half), lambda i: (i, a2_block)),
                  pl.BlockSpec((tm, d), lambda i: (i, 0)),
                  _const_spec(wo1.shape), _const_spec(wo2.shape), _const_spec(gain.shape),
                  _const_spec(w1.shape), _const_spec(w2.shape)],
        out_specs=pl.BlockSpec((tm, d), lambda i: (i, 0)),
        out_shape=jax.ShapeDtypeStruct((n, d), F32),
        scratch_shapes=[pltpu.VMEM((tm, d), F32)],
        compiler_params=_cparams(("parallel",)),
        name="out_proj_mlp",
    )(a1, a2, x2, wo1, wo2, gain, w1, w2)


def _in1_kernel(x_ref, g_ref, wb_ref, wf_ref, zb_ref, zf_ref):
    h = _rms(x_ref[...], g_ref[...]).astype(BF)
    gw = 512
    for c in range(wb_ref.shape[1] // gw):
        zb_ref[:, c * gw:(c + 1) * gw] = jnp.dot(
            h, wb_ref[:, c * gw:(c + 1) * gw], preferred_element_type=F32).astype(BF)
    for c in range(wf_ref.shape[1] // gw):
        zf_ref[:, c * gw:(c + 1) * gw] = jnp.dot(
            h, wf_ref[:, c * gw:(c + 1) * gw], preferred_element_type=F32)


def _in1(x2, gain, wb, wf, tm):
    n, d = x2.shape
    return pl.pallas_call(
        _in1_kernel,
        grid=(n // tm,),
        in_specs=[pl.BlockSpec((tm, d), lambda i: (i, 0)),
                  _const_spec(gain.shape), _const_spec(wb.shape), _const_spec(wf.shape)],
        out_specs=[pl.BlockSpec((tm, wb.shape[1]), lambda i: (i, 0)),
                   pl.BlockSpec((tm, wf.shape[1]), lambda i: (i, 0))],
        out_shape=[jax.ShapeDtypeStruct((n, wb.shape[1]), BF),
                   jax.ShapeDtypeStruct((n, wf.shape[1]), F32)],
        compiler_params=_cparams(("parallel",)),
        name="l1_in_proj",
    )(x2, gain, wb, wf)


HG_SUB = SUBLANES
HG_GROUP = 4
HG_HEADS_PER_STEP = 1


def _hgrn_intra_operands(qs, ks, bs, key_row):
    nb = CHUNK // HG_SUB
    d = qs[0].shape[1]
    blk = lambda x, i: x[HG_SUB * i:HG_SUB * (i + 1)]

    wide = []
    for c, (q, b) in enumerate(zip(qs, bs)):
        cols = []
        for s in range(HG_SUB):
            parts = []
            for i in range(nb):
                r = HG_SUB * i + s
                e = jnp.exp2(jnp.minimum(blk(b, i) - key_row(c, r, 0), 0.0))
                parts.append(blk(q, i) * e * key_row(c, r, 1))
            cols.append(jnp.concatenate(parts, axis=0))
            yield
        wide.append(jnp.concatenate(cols, axis=1).astype(BF))

    zero = jnp.zeros((HG_SUB, d), F32)
    lhs_all, rhs_all = [], []
    for q, k, b in zip(qs, ks, bs):
        lhs, rhs = [], []
        for j in range(1, nb // 2):
            bref = b[2 * HG_SUB * j:2 * HG_SUB * j + 1, :]
            lg, rg = [zero] * nb, [zero] * nb
            for i in (2 * j, 2 * j + 1):
                lg[i] = blk(q, i) * jnp.exp2(blk(b, i) - bref)
            for i in range(2 * j):
                rg[i] = blk(k, i) * jnp.exp2(bref - blk(b, i))
            lhs.append(jnp.concatenate(lg, axis=0))
            rhs.append(jnp.concatenate(rg, axis=0))
            yield
        for j in range(nb // 2):
            bref = b[HG_SUB * (2 * j + 1):HG_SUB * (2 * j + 1) + 1, :]
            lg, rg = [zero] * nb, [zero] * nb
            lg[2 * j + 1] = blk(q, 2 * j + 1) * jnp.exp2(blk(b, 2 * j + 1) - bref)
            rg[2 * j] = blk(k, 2 * j) * jnp.exp2(bref - blk(b, 2 * j))
            lhs.append(jnp.concatenate(lg, axis=0))
            rhs.append(jnp.concatenate(rg, axis=0))
            yield
        lhs_all.append(jnp.concatenate(lhs, axis=1).astype(BF))
        rhs_all.append(jnp.concatenate(rhs, axis=1).astype(BF))
    return wide, lhs_all, rhs_all


def _hgrn_kernel(zq_ref, zi_ref, zg_ref, zf_ref, lbl_ref, on_ref, o_ref,
                 oi_s, qd_s, kv_s, dec_s, h1_s, wide_s, lhs_s, rhs_s, kdec_s, *, group):
    t = zq_ref.shape[0]
    d = HGRN_HEAD_DIM
    heads = range(zq_ref.shape[1] // d)
    hs = lambda hh: slice(hh * d, (hh + 1) * d)
    gr = group * CHUNK
    ng = t // gr
    ri = lax.broadcasted_iota(jnp.int32, (gr, gr), 0)
    ci = lax.broadcasted_iota(jnp.int32, (gr, gr), 1)
    tril = ((ri >= ci) & (ri // CHUNK == ci // CHUNK)).astype(BF)
    row = lax.broadcasted_iota(jnp.int32, (CHUNK, CHUNK), 0)
    col = lax.broadcasted_iota(jnp.int32, (CHUNK, CHUNK), 1)
    own_block = (row // HG_SUB == col // HG_SUB) & (col % HG_SUB <= row % HG_SUB)
    gather = (lax.broadcasted_iota(jnp.int32, (CHUNK, HG_SUB * d), 0) % HG_SUB
              == lax.broadcasted_iota(jnp.int32, (CHUNK, HG_SUB * d), 1) // d).astype(BF)
    sls = [slice(c * CHUNK, (c + 1) * CHUNK) for c in range(group)]

    l0 = lbl_ref[0:1, :]
    l1 = lbl_ref[1:2, :]
    mx = jnp.maximum(l0, l1)
    e0 = jnp.exp(l0 - mx)
    e1 = jnp.exp(l1 - mx)
    s0 = e0 / (e0 + e1)
    s1 = e1 / (e0 + e1)
    lb = (s0 + s1) - s0

    def group_rows(g):
        start = g * gr if isinstance(g, int) else pl.multiple_of(g * gr, gr)
        return pl.ds(start, gr)

    def prelude(hh, g, slot):
        rows = group_rows(g)
        qs, ks, ls = [], [], []
        for sl in sls:
            qs.append(_silu(zq_ref[rows, hs(hh)][sl].astype(F32)))
            f = lb[:, hs(hh)] + (1.0 - lb[:, hs(hh)]) * _sigmoid(zf_ref[rows, hs(hh)][sl])
            ks.append(1.0 - f)
            ls.append(jnp.log2(f))
            yield
        q, k = jnp.concatenate(qs, axis=0), jnp.concatenate(ks, axis=0)
        b = _mm_exact_lhs(tril, jnp.concatenate(ls, axis=0))
        yield
        return [(h1_s, (hh, slot, 0), b), (h1_s, (hh, slot, 1), k), (h1_s, (hh, slot, 2), q),
                (qd_s, (hh, rows, slice(None)), (q * jnp.exp2(b)).astype(BF))]

    def operands(hh, g, slot):
        b, k, q = h1_s[hh, slot, 0], h1_s[hh, slot, 1], h1_s[hh, slot, 2]
        key_row = lambda c, r, which: h1_s[hh, slot, which, c * CHUNK + r:c * CHUNK + r + 1, :]
        bc = [b[sl] for sl in sls]
        b_last = [x[CHUNK - 1:CHUNK, :] for x in bc]
        wide, lhs, rhs = yield from _hgrn_intra_operands(
            [q[sl] for sl in sls], [k[sl] for sl in sls], bc, key_row)
        kdec = [k[sl] * jnp.exp2(bl - x) for sl, bl, x in zip(sls, b_last, bc)]
        yield
        return [(wide_s, (hh, slot), jnp.concatenate(wide, axis=0)),
                (lhs_s, (hh, slot), jnp.concatenate(lhs, axis=0)),
                (rhs_s, (hh, slot), jnp.concatenate(rhs, axis=0)),
                (kdec_s, (hh, slot), jnp.concatenate(kdec, axis=0).astype(BF)),
                (dec_s, (hh, pl.ds(g * group, group)),
                 jnp.stack([jnp.exp2(bl) for bl in b_last]))]

    def products(hh, g, slot):
        rows = group_rows(g)
        v = zi_ref[rows, hs(hh)]
        dmat, kv, oi = [], [], []
        for sl in sls:
            diag = _mm_nt(wide_s[hh, slot, sl, :], gather)
            yield
            off = _mm_nt(lhs_s[hh, slot, sl, :], rhs_s[hh, slot, sl, :])
            yield
            kv.append(_mm_tn(v[sl], kdec_s[hh, slot, sl, :]))
            dmat.append(jnp.where(own_block, diag, 0.0) + off)
            yield
        for dm, sl in zip(dmat, sls):
            oi.append(_mm(dm, v[sl]))
            yield
        return [(oi_s, (hh, rows, slice(None)), jnp.concatenate(oi, axis=0)),
                (kv_s, (hh, pl.ds(g * group, group)), jnp.stack(kv))]

    def step(i, slot, stages):
        active = []
        for hh in heads:
            if "products" in stages:
                active.append((products(hh, i - 2, slot), 1))
            if "operands" in stages:
                active.append((operands(hh, i - 1, 1 - slot), 3))
            if "prelude" in stages:
                active.append((prelude(hh, i, slot), 1))
        stores = []
        while active:
            for gen, turns in list(active):
                try:
                    for _ in range(turns):
                        next(gen)
                except StopIteration as done:
                    stores += done.value
                    active.remove((gen, turns))
        for ref, idx, val in stores:
            ref[idx] = val

    assert ng % 2 == 0
    step(0, 0, ("prelude",))
    step(1, 1, ("operands", "prelude"))

    def steady(j, carry):
        i = 2 + 2 * j
        step(i, 0, ("products", "operands", "prelude"))
        step(i + 1, 1, ("products", "operands", "prelude"))
        return carry

    lax.fori_loop(0, (ng - 2) // 2, steady, 0)
    step(ng, 0, ("products", "operands"))
    step(ng + 1, 1, ("products",))

    per_step = 4 * group
    sr = per_step * CHUNK

    def state_step(g, sts):
        r0 = pl.multiple_of(g * sr, sr)
        rows = pl.ds(r0, sr)
        new_sts, normed = [], []
        for hh, st in zip(heads, sts):
            states = []
            for c in range(per_step):
                n = g * per_step + c
                states.append(st)
                st = st * dec_s[hh, n] + kv_s[hh, n]
            new_sts.append(st)
            outs = [oi_s[hh, pl.ds(r0 + c * CHUNK, CHUNK), :]
                    + _mm_nt(qd_s[hh, pl.ds(r0 + c * CHUNK, CHUNK), :], states[c])
                    for c in range(per_step)]
            normed.append(_rms(jnp.concatenate(outs, axis=0), on_ref[...]))
        o = jnp.concatenate(normed, axis=1)
        o_ref[rows, :] = (o * _silu(zg_ref[rows, :].astype(F32))).astype(o_ref.dtype)
        return tuple(new_sts)

    lax.fori_loop(0, t // sr, state_step, tuple(jnp.zeros((d, d), F32) for _ in heads))


def _hgrn(zb, zf, lb_logits, o_norm, b, t):
    n = zb.shape[0]
    hd = HGRN_HEAD_DIM
    nh = HGRN_HEADS
    nc = t // CHUNK
    gr = HG_GROUP * CHUNK
    hps = HG_HEADS_PER_STEP
    ns = nh // hps
    wide = hps * hd
    return pl.pallas_call(
        functools.partial(_hgrn_kernel, group=HG_GROUP),
        grid=(b, ns),
        in_specs=[pl.BlockSpec((t, wide), lambda bi, h: (bi, h)),
                  pl.BlockSpec((t, wide), lambda bi, h: (bi, ns + h)),
                  pl.BlockSpec((t, wide), lambda bi, h: (bi, 2 * ns + h)),
                  pl.BlockSpec((t, wide), lambda bi, h: (bi, h)),
                  pl.BlockSpec((lb_logits.shape[0], wide), lambda bi, h: (0, h)),
                  _const_spec(o_norm.shape)],
        out_specs=pl.BlockSpec((t, wide), lambda bi, h: (bi, h)),
        out_shape=jax.ShapeDtypeStruct((n, nh * hd), BF),
        scratch_shapes=[pltpu.VMEM((hps, t, hd), F32), pltpu.VMEM((hps, t, hd), BF),
                        pltpu.VMEM((hps, nc, hd, hd), F32), pltpu.VMEM((hps, nc, 1, hd), F32),
                        pltpu.VMEM((hps, 2, 3, gr, hd), F32),
                        pltpu.VMEM((hps, 2, gr, HG_SUB * hd), BF),
                        pltpu.VMEM((hps, 2, gr, (HG_SUB - 1) * hd), BF),
                        pltpu.VMEM((hps, 2, gr, (HG_SUB - 1) * hd), BF),
                        pltpu.VMEM((hps, 2, gr, hd), BF)],
        compiler_params=_cparams(("parallel", "parallel")),
        name="hgrn2",
    )(zb, zb, zb, zf, lb_logits, o_norm)


def kernel(x, l0_mix_norm, l0_w_in, l0_fox_q_norm, l0_fox_k_norm, l0_fox_f_bias, l0_gdn_conv,
           l0_gdn_A_log, l0_gdn_dt_bias, l0_gdn_o_norm, l0_w_out, l0_ffn_norm, l0_w_ff1,
           l0_w_ff2, l1_mix_norm, l1_w_in, l1_hgrn_o_norm, l1_w_out, l1_ffn_norm, l1_w_ff1,
           l1_w_ff2, hgrn_lb_logits):
    b, t, d = x.shape
    n = b * t
    fw = FOX_HEADS * FOX_HEAD_DIM
    gw = GDN_HEADS * GDN_HEAD_DIM
    assert hgrn_lb_logits.shape[0] == 2 and d == HGRN_HEADS * HGRN_HEAD_DIM
    assert t % (2 * HG_GROUP * CHUNK) == 0
    tm = 512 if t % 512 == 0 else 256
    tq = 256
    nc = t // CHUNK
    row = lambda p: p.reshape(1, -1).astype(F32)
    x2 = x.reshape(n, d)

    o_ff = 3 * fw
    o_gq = o_ff + FOX_HEADS
    o_gb = o_gq + 3 * gw
    o_gg = o_gb + 2 * GDN_HEADS
    wb0 = jnp.concatenate([l0_w_in[:, :o_ff], l0_w_in[:, o_gq:o_gb], l0_w_in[:, o_gg:]],
                          axis=1).astype(BF)
    n_small = FOX_HEADS + 2 * GDN_HEADS
    ws0 = jnp.concatenate([l0_w_in[:, o_ff:o_gq], l0_w_in[:, o_gb:o_gg],
                           jnp.zeros((d, LANES - n_small), F32)], axis=1).astype(BF)
    hid = jnp.arange(fw) // FOX_HEAD_DIM
    pm = jnp.where(hid[:, None] == hid[None, :], 1.0 / FOX_HEAD_DIM, 0.0).astype(BF)
    qkg = jnp.stack([jnp.tile(l0_fox_q_norm, FOX_HEADS) * (FOX_HEAD_DIM ** -0.5 * LOG2E),
                     jnp.tile(l0_fox_k_norm, FOX_HEADS)]).astype(F32)

    zb0, zs0 = _in0(x2, row(l0_mix_norm), wb0, ws0, pm, qkg, l0_gdn_conv.astype(F32), tm, t)

    f_bias = jnp.pad(l0_fox_f_bias.astype(F32), (0, LANES - FOX_HEADS)).reshape(1, LANES)
    qbias, kbias = _fox_cum(zs0, f_bias, b, t)
    fox_o = _fox_attention(zb0, qbias, kbias, b, t, tq)

    def gate_rows(lo):
        g = zs0[:, lo:lo + GDN_HEADS].reshape(b, t, GDN_HEADS).transpose(0, 2, 1)
        return g.reshape(b, GDN_HEADS, nc, CHUNK)
    gb_row = gate_rows(FOX_HEADS)
    ga_row = gate_rows(FOX_HEADS + GDN_HEADS)
    gdn_o = _gdn(zb0, ga_row, ga_row.swapaxes(-1, -2), gb_row.swapaxes(-1, -2),
                 l0_gdn_A_log.reshape(GDN_HEADS, 1, 1).astype(F32),
                 l0_gdn_dt_bias.reshape(GDN_HEADS, 1, 1).astype(F32), row(l0_gdn_o_norm), b, t)

    wo0 = l0_w_out.astype(BF)
    x1 = _mlp(fox_o, gdn_o, 0, x2, wo0[:fw], wo0[fw:], row(l0_ffn_norm),
              l0_w_ff1.astype(BF), l0_w_ff2.astype(BF), tm)

    wb1 = jnp.concatenate([l1_w_in[:, :d], l1_w_in[:, 2 * d:]], axis=1).astype(BF)
    wf1 = l1_w_in[:, d:2 * d].astype(BF)
    zb1, zf1 = _in1(x1, row(l1_mix_norm), wb1, wf1, tm)
    ho = _hgrn(zb1, zf1, hgrn_lb_logits.astype(F32), row(l1_hgrn_o_norm), b, t)
    wo1 = l1_w_out.astype(BF)
    half = d // 2
    x3 = _mlp(ho, ho, 1, x1, wo1[:half], wo1[half:], row(l1_ffn_norm),
              l1_w_ff1.astype(BF), l1_w_ff2.astype(BF), tm)
    return x3.reshape(b, t, d)
```

```python
import functools

import jax
import jax.numpy as jnp
from jax import lax
from jax.experimental import pallas as pl
from jax.experimental.pallas import tpu as pltpu

BF = jnp.bfloat16
F32 = jnp.float32
NORM_EPS = 1e-6
LOG2E = 1.4426950408889634
CHUNK = 64
FOX_HEADS, FOX_HEAD_DIM = 8, 64
GDN_HEADS, GDN_HEAD_DIM = 4, 128
HGRN_HEADS, HGRN_HEAD_DIM = 8, 128
GDN_CONV = 4
IN0_ROWS = 128
LANES = 128
SUBLANES = 8
VMEM_LIMIT = 56 * 1024 * 1024


def _cparams(sem):
    return pltpu.CompilerParams(dimension_semantics=sem, vmem_limit_bytes=VMEM_LIMIT)


def _const_spec(shape):
    nd = len(shape)
    return pl.BlockSpec(shape, lambda *_: (0,) * nd, pipeline_mode=pl.Buffered(1))


def _mm(a, b):
    return jnp.dot(a.astype(BF), b.astype(BF), preferred_element_type=F32)


def _mm_nt(a, b):
    return lax.dot_general(a.astype(BF), b.astype(BF), (((1,), (1,)), ((), ())),
                           preferred_element_type=F32)


def _mm_tn(a, b):
    return lax.dot_general(a.astype(BF), b.astype(BF), (((0,), (0,)), ((), ())),
                           preferred_element_type=F32)


def _split3(x):
    hi = x.astype(BF)
    r1 = x - hi.astype(F32)
    mid = r1.astype(BF)
    lo = (r1 - mid.astype(F32)).astype(BF)
    return hi, mid, lo


def _mm_exact_lhs(m_bf, x):
    n = x.shape[1]
    if n % LANES:
        hi, mid, lo = _split3(x)
        d = lambda p: jnp.dot(m_bf, p, preferred_element_type=F32)
        return d(hi) + d(mid) + d(lo)
    y = jnp.dot(m_bf, jnp.concatenate(_split3(x), axis=1), preferred_element_type=F32)
    return y[:, :n] + y[:, n:2 * n] + y[:, 2 * n:]


def _mm_exact_rhs(x, m_bf):
    hi, mid, lo = _split3(x)
    d = lambda p: jnp.dot(p, m_bf, preferred_element_type=F32)
    return d(hi) + d(mid) + d(lo)


def _sigmoid(x):
    return 1.0 / (1.0 + jnp.exp(-x))


def _silu(x):
    return x * _sigmoid(x)


def _softplus(x):
    return jnp.maximum(x, 0.0) + jnp.log1p(jnp.exp(-jnp.abs(x)))


def _rms(x, gain):
    return x * lax.rsqrt(jnp.mean(x * x, axis=-1, keepdims=True) + NORM_EPS) * gain


def _in0_kernel(x_ref, g_ref, wb_ref, ws_ref, pm_ref, qkg_ref, cw_ref, zb_ref, zs_ref, hist_ref,
                *, tiles_per_seq):
    tm = x_ref.shape[0]
    hd = GDN_HEAD_DIM
    hrows = hist_ref.shape[1]

    @pl.when(pl.program_id(0) % tiles_per_seq == 0)
    def _():
        hist_ref[...] = jnp.zeros_like(hist_ref)

    h = _rms(x_ref[...], g_ref[...]).astype(BF)
    gw = 512
    rb = IN0_ROWS

    def project(c, r):
        return jnp.dot(h[r * rb:(r + 1) * rb], wb_ref[:, c * gw:(c + 1) * gw],
                       preferred_element_type=F32)

    def finish(c, r, z, above):
        if c < 2:
            ms = jnp.dot((z * z).astype(BF), pm_ref[...], preferred_element_type=F32)
            z = z * lax.rsqrt(ms + NORM_EPS) * qkg_ref[c:c + 1, :]
        elif 3 <= c < 6:
            j = c - 3
            w = cw_ref[:, j * gw:(j + 1) * gw]
            ext = jnp.concatenate([above, z], axis=0)
            y = z * w[GDN_CONV - 1:GDN_CONV]
            for s in range(1, GDN_CONV):
                y = y + ext[hrows - s:hrows - s + rb] * w[GDN_CONV - 1 - s:GDN_CONV - s]
            z = _silu(y)
            if j < 2:
                scale = 1.0 if j else hd ** -0.5
                z = jnp.concatenate(
                    [zh * (lax.rsqrt(jnp.sum(zh * zh, axis=-1, keepdims=True) + NORM_EPS) * scale)
                     for zh in (z[:, k * hd:(k + 1) * hd] for k in range(gw // hd))], axis=1)
        zb_ref[r * rb:(r + 1) * rb, c * gw:(c + 1) * gw] = z.astype(BF)

    pending = None
    for c in range(wb_ref.shape[1] // gw):
        conv = 3 <= c < 6
        above = hist_ref[c - 3] if conv else None
        for r in range(tm // rb):
            z = project(c, r)
            if pending is not None:
                finish(*pending)
            pending = (c, r, z, above)
            above = z[rb - hrows:, :] if conv else None
        if conv:
            hist_ref[c - 3] = above
    finish(*pending)
    zs_ref[...] = jnp.dot(h, ws_ref[...], preferred_element_type=F32)


def _in0(x2, gain, wb, ws, pm, qkg, conv_w, tm, t):
    n, d = x2.shape
    nb = wb.shape[1]
    return pl.pallas_call(
        functools.partial(_in0_kernel, tiles_per_seq=t // tm),
        grid=(n // tm,),
        in_specs=[pl.BlockSpec((tm, d), lambda i: (i, 0)),
                  _const_spec((1, d)), _const_spec(wb.shape), _const_spec(ws.shape),
                  _const_spec(pm.shape), _const_spec(qkg.shape), _const_spec(conv_w.shape)],
        out_specs=[pl.BlockSpec((tm, nb), lambda i: (i, 0)),
                   pl.BlockSpec((tm, LANES), lambda i: (i, 0))],
        out_shape=[jax.ShapeDtypeStruct((n, nb), BF), jax.ShapeDtypeStruct((n, LANES), F32)],
        scratch_shapes=[pltpu.VMEM((3, SUBLANES, 512), F32)],
        compiler_params=_cparams(("arbitrary",)),
        name="l0_in_proj",
    )(x2, gain, wb, ws, pm, qkg, conv_w)


FOX_BIAS_LANES = 8


def _fox_cum_kernel(zs_ref, bias_ref, qb_ref, kb_ref, *, blk):
    t = zs_ref.shape[0]
    r = lax.broadcasted_iota(jnp.int32, (blk, blk), 0)
    c = lax.broadcasted_iota(jnp.int32, (blk, blk), 1)
    tril = (r >= c).astype(BF)
    pr = lax.broadcasted_iota(jnp.int32, (3 * LANES, 2 * LANES), 0)
    pc = lax.broadcasted_iota(jnp.int32, (3 * LANES, 2 * LANES), 1)
    place = ((pr % LANES < FOX_HEADS)
             & (pc % LANES == (pr % LANES) * FOX_BIAS_LANES + pr // LANES + 3 * (pc // LANES))
             ).astype(BF)
    lane = lax.broadcasted_iota(jnp.int32, (1, LANES), 1)
    slot = lane % FOX_BIAS_LANES
    used = lane < FOX_HEADS * FOX_BIAS_LANES
    ones_q = jnp.where(used & (slot >= 3) & (slot < 6), 1.0, 0.0)
    ones_k = jnp.where(used & (slot < 3), 1.0, 0.0)
    blocks = [slice(i * blk, (i + 1) * blk) for i in range(t // blk)]
    local = []
    for rows in blocks:
        z = zs_ref[rows, :] + bias_ref[...]
        ls = jnp.minimum(z, 0.0) - jnp.log1p(jnp.exp(-jnp.abs(z)))
        local.append(_mm_exact_lhs(tril, ls))
    carry = jnp.zeros((1, LANES), F32)
    for rows, loc in zip(blocks, local):
        cb = loc + carry
        carry = cb[blk - 1:blk, :]
        feat = jnp.dot(jnp.concatenate(_split3(cb * LOG2E), axis=1), place,
                       preferred_element_type=F32)
        qb_ref[rows, :] = (feat[:, :LANES] + ones_q).astype(BF)
        kb_ref[rows, :] = (ones_k - feat[:, LANES:]).astype(BF)


def _fox_cum(zs, bias_row, b, t):
    n = zs.shape[0]
    spec = pl.BlockSpec((t, LANES), lambda bi: (bi, 0))
    return pl.pallas_call(
        functools.partial(_fox_cum_kernel, blk=256),
        grid=(b,),
        in_specs=[spec, _const_spec(bias_row.shape)],
        out_specs=[spec, spec],
        out_shape=[jax.ShapeDtypeStruct((n, LANES), BF)] * 2,
        compiler_params=_cparams(("parallel",)),
        name="fox_cum_gate",
    )(zs, bias_row)


def _fox_kernel(q_ref, k_ref, v_ref, qb_ref, kb_ref, o_ref, *, tq):
    pair = pl.program_id(1)
    t = q_ref.shape[0]
    lane = lax.broadcasted_iota(jnp.int32, (1, LANES), 1)
    zero = jnp.zeros((tq, LANES), BF)
    first = lane < FOX_HEAD_DIM
    bias_head = lane // FOX_BIAS_LANES
    ones = jnp.ones((tq, LANES), BF)
    on_diag = (lax.broadcasted_iota(jnp.int32, (2 * tq, tq), 1)
               <= lax.broadcasted_iota(jnp.int32, (2 * tq, tq), 0) % tq)

    def scores(i):
        rows_i = slice(i * tq, (i + 1) * tq)
        kv = (i + 1) * tq
        q2, qb = q_ref[rows_i, :], qb_ref[rows_i, :]
        rows = []
        for h in range(2):
            qh = jnp.where(first if h == 0 else jnp.logical_not(first), q2, zero)
            qbh = jnp.where(bias_head == 2 * pair + h, qb, zero)
            rows.append(jnp.concatenate([qh, qbh], axis=1))
        ka = jnp.concatenate([k_ref[:kv, :], kb_ref[:kv, :]], axis=1)
        s = lax.dot_general(jnp.concatenate(rows, axis=0), ka, (((1,), (1,)), ((), ())),
                            preferred_element_type=F32)
        diag = jnp.where(on_diag, s[:, kv - tq:], -jnp.inf)
        return diag if i == 0 else jnp.concatenate([s[:, :kv - tq], diag], axis=1)

    def finish(i, s):
        kv = (i + 1) * tq
        p = jnp.exp2(s - jnp.max(s, axis=-1, keepdims=True)).astype(BF)
        va = jnp.concatenate([v_ref[:kv, :], jnp.concatenate([ones] * (i + 1), axis=0)], axis=1)
        acc = jnp.dot(p, va, preferred_element_type=F32)
        o = acc[:, :LANES] / acc[:, LANES:]
        o_ref[i * tq:(i + 1) * tq, :] = jnp.where(first, o[:tq], o[tq:]).astype(o_ref.dtype)

    s = scores(0)
    for i in range(t // tq):
        s_next = scores(i + 1) if (i + 1) * tq < t else None
        finish(i, s)
        s = s_next


def _fox_attention(zb, qbias, kbias, b, t, tq):
    n = zb.shape[0]
    hp = FOX_HEADS // 2
    return pl.pallas_call(
        functools.partial(_fox_kernel, tq=tq),
        grid=(b, hp),
        in_specs=[pl.BlockSpec((t, LANES), lambda bi, p: (bi, p)),
                  pl.BlockSpec((t, LANES), lambda bi, p: (bi, hp + p)),
                  pl.BlockSpec((t, LANES), lambda bi, p: (bi, 2 * hp + p)),
                  pl.BlockSpec((t, LANES), lambda bi, p: (bi, 0)),
                  pl.BlockSpec((t, LANES), lambda bi, p: (bi, 0))],
        out_specs=pl.BlockSpec((t, LANES), lambda bi, p: (bi, p)),
        out_shape=jax.ShapeDtypeStruct((n, FOX_HEADS * FOX_HEAD_DIM), BF),
        compiler_params=_cparams(("parallel", "parallel")),
        name="fox_attention",
    )(zb, zb, zb, qbias, kbias)


def _neumann_inverse(mats, eye):
    ps = [eye - a for a in mats]
    aps = list(mats)
    k = 2
    while k < CHUNK:
        aps = [_mm(ap, ap) for ap in aps]
        ps = [p + _mm(p, ap) for p, ap in zip(ps, aps)]
        k *= 2
    return ps


def _gdn_kernel(alog_ref, dtb_ref, q_ref, k_ref, v_ref, gg_ref, gar_ref, gac_ref, gbc_ref,
                on_ref, o_ref, lhs_s, ku_s, u_s, qk_s, pend_s, grow_s, gcol_s, bcol_s, s_s):
    t = q_ref.shape[0]
    nc = t // CHUNK
    hd = GDN_HEAD_DIM
    lrows = hd + 2 * CHUNK

    ri = lax.broadcasted_iota(jnp.int32, (CHUNK, CHUNK), 0)
    ci = lax.broadcasted_iota(jnp.int32, (CHUNK, CHUNK), 1)
    lower = ri >= ci
    strict = ri > ci
    tril = lower.astype(BF)
    triu = (ri <= ci).astype(BF)
    eye = (ri == ci).astype(F32)
    for h in range(GDN_HEADS):
        a = -jnp.exp(alog_ref[h])
        grow_s[h] = _mm_exact_rhs(a * _softplus(gar_ref[0, h] + dtb_ref[h]), triu)
        gcol_s[h] = _mm_exact_lhs(tril, a * _softplus(gac_ref[0, h] + dtb_ref[h]))
        bcol_s[h] = _sigmoid(gbc_ref[0, h])
    s_s[...] = jnp.zeros_like(s_s)

    lane_nc = lax.broadcasted_iota(jnp.int32, (CHUNK, nc), 1)
    heads = range(GDN_HEADS)
    cat = lambda parts: jnp.concatenate(parts, axis=1)

    per_iter = 8

    def local_step(it, carry):
        r0 = pl.multiple_of(it * (per_iter * CHUNK), per_iter * CHUNK)
        rows = pl.ds(r0, per_iter * CHUNK)
        chains = [(c, h) for c in range(per_iter) for h in heads]
        csl = lambda c: slice(c * CHUNK, (c + 1) * CHUNK)
        hsl = lambda h: slice(h * hd, (h + 1) * hd)
        q_all, k_all, v_all = (r[rows, :].astype(F32) for r in (q_ref, k_ref, v_ref))
        q = [q_all[csl(c), hsl(h)] for c, h in chains]
        k = [k_all[csl(c), hsl(h)] for c, h in chains]
        v = [v_all[csl(c), hsl(h)] for c, h in chains]
        sel = [lane_nc == it * per_iter + c for c in range(per_iter)]
        g_c = [jnp.sum(jnp.where(sel[c], gcol_s[h], 0.0), axis=1, keepdims=True)
               for c, h in chains]
        b_c = [jnp.sum(jnp.where(sel[c], bcol_s[h], 0.0), axis=1, keepdims=True)
               for c, h in chains]
        g_r = [grow_s[h, pl.ds(it * per_iter + c, 1), :] for c, h in chains]
        decay = [jnp.exp(jnp.where(lower, gc - gr, -jnp.inf)) for gc, gr in zip(g_c, g_r)]
        kb = [x * bc for x, bc in zip(k, b_c)]
        kk = [_mm_nt(x, y) for x, y in zip(kb, k)]
        qk = [_mm_nt(x, y) for x, y in zip(q, k)]
        t_inv = _neumann_inverse([jnp.where(strict, x * d, 0.0) for x, d in zip(kk, decay)], eye)
        eg = [jnp.exp(gc) for gc in g_c]
        uw = [_mm(ti, jnp.concatenate([x * bc, y * e], axis=1))
              for ti, x, bc, y, e in zip(t_inv, v, b_c, kb, eg)]
        kd = [x * jnp.exp(gr[:, CHUNK - 1:CHUNK] - gc) for x, gr, gc in zip(k, g_r, g_c)]
        kuw = [_mm_tn(x, y) for x, y in zip(kd, uw)]
        grid = lambda parts: jnp.concatenate(
            [cat(parts[c * GDN_HEADS:(c + 1) * GDN_HEADS]) for c in range(per_iter)], axis=0)
        lhs = [jnp.concatenate([m[:, hd:], x[:, hd:], y * e], axis=0)
               for m, x, y, e in zip(kuw, uw, q, eg)]
        lhs_s[pl.ds(pl.multiple_of(it * (per_iter * lrows), per_iter * lrows),
                    per_iter * lrows), :] = grid(lhs).astype(BF)
        ku_s[pl.ds(pl.multiple_of(it * (per_iter * hd), per_iter * hd), per_iter * hd), :] = (
            grid([m[:, :hd] for m in kuw]))
        u_s[rows, :] = grid([x[:, :hd] for x in uw])
        qk_s[rows, :] = grid([x * d for x, d in zip(qk, decay)]).astype(BF)
        return carry

    lax.fori_loop(0, nc // per_iter, local_step, 0)

    hsl = lambda h: slice(h * hd, (h + 1) * hd)

    def chunk_rows(n, size):
        return pl.ds(n * size if isinstance(n, int) else pl.multiple_of(n * size, size), size)

    def advance(n):
        s_prev = [s_s[h] for h in heads]
        prod = [_mm(lhs_s[chunk_rows(n, lrows), hsl(h)], s_prev[h]) for h in heads]
        g_last = [grow_s[h, pl.ds(n, 1), :][:, CHUNK - 1:CHUNK] for h in heads]
        s_s[...] = jnp.stack([s_prev[h] * jnp.exp(g_last[h]) - prod[h][:hd]
                              + ku_s[chunk_rows(n, hd), hsl(h)] for h in heads])
        pend_s[...] = cat([prod[h][hd:] for h in heads])

    def emit(n):
        rows = chunk_rows(n, CHUNK)
        pend = pend_s[...]
        v_new = [u_s[rows, hsl(h)] - pend[:CHUNK, hsl(h)] for h in heads]
        qv = [_mm(qk_s[rows, h * CHUNK:(h + 1) * CHUNK], v_new[h]) for h in heads]
        outs = [_rms(pend[CHUNK:, hsl(h)] + qv[h], on_ref[...]) for h in heads]
        gate = _silu(gg_ref[rows, :].astype(F32))
        o_ref[rows, :] = (cat(outs) * gate).astype(o_ref.dtype)

    def state_step(n, carry):
        emit(n - 1)
        advance(n)
        return carry

    advance(0)
    lax.fori_loop(1, nc, state_step, 0)
    emit(nc - 1)


def _gdn(zb, ga_row, ga_col, gb_col, a_log, dt_bias, o_norm, b, t):
    n = zb.shape[0]
    nc = t // CHUNK
    width = GDN_HEADS * GDN_HEAD_DIM
    fox_blocks = 3 * FOX_HEADS * FOX_HEAD_DIM // width
    colspec = lambda j: pl.BlockSpec((t, width), lambda bi: (bi, fox_blocks + j))
    gspec = lambda shp: pl.BlockSpec((1,) + shp, lambda bi: (bi, 0, 0, 0))
    return pl.pallas_call(
        _gdn_kernel,
        grid=(b,),
        in_specs=[_const_spec(a_log.shape), _const_spec(dt_bias.shape),
                  colspec(0), colspec(1), colspec(2), colspec(3),
                  gspec((GDN_HEADS, nc, CHUNK)), gspec((GDN_HEADS, CHUNK, nc)),
                  gspec((GDN_HEADS, CHUNK, nc)), _const_spec(o_norm.shape)],
        out_specs=pl.BlockSpec((t, width), lambda bi: (bi, 0)),
        out_shape=jax.ShapeDtypeStruct((n, width), BF),
        scratch_shapes=[pltpu.VMEM((nc * (GDN_HEAD_DIM + 2 * CHUNK), width), BF),
                        pltpu.VMEM((nc * GDN_HEAD_DIM, width), F32),
                        pltpu.VMEM((t, width), F32),
                        pltpu.VMEM((t, GDN_HEADS * CHUNK), BF),
                        pltpu.VMEM((2 * CHUNK, width), F32),
                        pltpu.VMEM((GDN_HEADS, nc, CHUNK), F32),
                        pltpu.VMEM((GDN_HEADS, CHUNK, nc), F32),
                        pltpu.VMEM((GDN_HEADS, CHUNK, nc), F32),
                        pltpu.VMEM((GDN_HEADS, GDN_HEAD_DIM, GDN_HEAD_DIM), F32)],
        compiler_params=_cparams(("parallel",)),
        name="gated_deltanet",
    )(a_log, dt_bias, zb, zb, zb, zb, ga_row, ga_col, gb_col, o_norm)


def _mlp_kernel(a1_ref, a2_ref, x_ref, wo1_ref, wo2_ref, g_ref, w1_ref, w2_ref, o_ref, acc_ref,
                *, fc):
    mix = (jnp.dot(a1_ref[...], wo1_ref[...], preferred_element_type=F32)
           + jnp.dot(a2_ref[...], wo2_ref[...], preferred_element_type=F32))
    x1 = x_ref[...] + mix
    h = _rms(x1, g_ref[...]).astype(BF)
    acc_ref[...] = x1
    for c in range(w1_ref.shape[1] // fc):
        a = jnp.maximum(jnp.dot(h, w1_ref[:, c * fc:(c + 1) * fc],
                                preferred_element_type=F32), 0.0)
        acc_ref[...] += jnp.dot((a * a).astype(BF), w2_ref[c * fc:(c + 1) * fc, :],
                                preferred_element_type=F32)
    o_ref[...] = acc_ref[...]


def _mlp(a1, a2, a2_block, x2, wo1, wo2, gain, w1, w2, tm):
    n, d = x2.shape
    half = wo1.shape[0]
    return pl.pallas_call(
        functools.partial(_mlp_kernel, fc=1024),
        grid=(n // tm,),
        in_specs=[pl.BlockSpec((tm, half), lambda i: (i, 0)),
                  pl.BlockSpec((tm, half), lambda i: (i, a2_block)),
                  pl.BlockSpec((tm, d), lambda i: (i, 0)),
                  _const_spec(wo1.shape), _const_spec(wo2.shape), _const_spec(gain.shape),
                  _const_spec(w1.shape), _const_spec(w2.shape)],
        out_specs=pl.BlockSpec((tm, d), lambda i: (i, 0)),
        out_shape=jax.ShapeDtypeStruct((n, d), F32),
        scratch_shapes=[pltpu.VMEM((tm, d), F32)],
        compiler_params=_cparams(("parallel",)),
        name="out_proj_mlp",
    )(a1, a2, x2, wo1, wo2, gain, w1, w2)


def _in1_kernel(x_ref, g_ref, wb_ref, wf_ref, zb_ref, zf_ref):
    h = _rms(x_ref[...], g_ref[...]).astype(BF)
    gw = 512
    for c in range(wb_ref.shape[1] // gw):
        zb_ref[:, c * gw:(c + 1) * gw] = jnp.dot(
            h, wb_ref[:, c * gw:(c + 1) * gw], preferred_element_type=F32).astype(BF)
    for c in range(wf_ref.shape[1] // gw):
        zf_ref[:, c * gw:(c + 1) * gw] = jnp.dot(
            h, wf_ref[:, c * gw:(c + 1) * gw], preferred_element_type=F32)


def _in1(x2, gain, wb, wf, tm):
    n, d = x2.shape
    return pl.pallas_call(
        _in1_kernel,
        grid=(n // tm,),
        in_specs=[pl.BlockSpec((tm, d), lambda i: (i, 0)),
                  _const_spec(gain.shape), _const_spec(wb.shape), _const_spec(wf.shape)],
        out_specs=[pl.BlockSpec((tm, wb.shape[1]), lambda i: (i, 0)),
                   pl.BlockSpec((tm, wf.shape[1]), lambda i: (i, 0))],
        out_shape=[jax.ShapeDtypeStruct((n, wb.shape[1]), BF),
                   jax.ShapeDtypeStruct((n, wf.shape[1]), F32)],
        compiler_params=_cparams(("parallel",)),
        name="l1_in_proj",
    )(x2, gain, wb, wf)


HG_SUB = SUBLANES
HG_GROUP = 4
HG_HEADS_PER_STEP = 1


def _hgrn_intra_operands(qs, ks, bs, key_row):
    nb = CHUNK // HG_SUB
    d = qs[0].shape[1]
    blk = lambda x, i: x[HG_SUB * i:HG_SUB * (i + 1)]

    wide = []
    for c, (q, b) in enumerate(zip(qs, bs)):
        cols = []
        for s in range(HG_SUB):
            parts = []
            for i in range(nb):
                r = HG_SUB * i + s
                e = jnp.exp2(jnp.minimum(blk(b, i) - key_row(c, r, 0), 0.0))
                parts.append(blk(q, i) * e * key_row(c, r, 1))
            cols.append(jnp.concatenate(parts, axis=0))
            yield
        wide.append(jnp.concatenate(cols, axis=1).astype(BF))

    zero = jnp.zeros((HG_SUB, d), F32)
    lhs_all, rhs_all = [], []
    for q, k, b in zip(qs, ks, bs):
        lhs, rhs = [], []
        for j in range(1, nb // 2):
            bref = b[2 * HG_SUB * j:2 * HG_SUB * j + 1, :]
            lg, rg = [zero] * nb, [zero] * nb
            for i in (2 * j, 2 * j + 1):
                lg[i] = blk(q, i) * jnp.exp2(blk(b, i) - bref)
            for i in range(2 * j):
                rg[i] = blk(k, i) * jnp.exp2(bref - blk(b, i))
            lhs.append(jnp.concatenate(lg, axis=0))
            rhs.append(jnp.concatenate(rg, axis=0))
            yield
        for j in range(nb // 2):
            bref = b[HG_SUB * (2 * j + 1):HG_SUB * (2 * j + 1) + 1, :]
            lg, rg = [zero] * nb, [zero] * nb
            lg[2 * j + 1] = blk(q, 2 * j + 1) * jnp.exp2(blk(b, 2 * j + 1) - bref)
            rg[2 * j] = blk(k, 2 * j) * jnp.exp2(bref - blk(b, 2 * j))
            lhs.append(jnp.concatenate(lg, axis=0))
            rhs.append(jnp.concatenate(rg, axis=0))
            yield
        lhs_all.append(jnp.concatenate(lhs, axis=1).astype(BF))
        rhs_all.append(jnp.concatenate(rhs, axis=1).astype(BF))
    return wide, lhs_all, rhs_all


def _hgrn_kernel(zq_ref, zi_ref, zg_ref, zf_ref, lbl_ref, on_ref, o_ref,
                 oi_s, qd_s, kv_s, dec_s, h1_s, *, group):
    t = zq_ref.shape[0]
    d = HGRN_HEAD_DIM
    heads = range(zq_ref.shape[1] // d)
    hs = lambda hh: slice(hh * d, (hh + 1) * d)
    gr = group * CHUNK
    ng = t // gr
    ri = lax.broadcasted_iota(jnp.int32, (gr, gr), 0)
    ci = lax.broadcasted_iota(jnp.int32, (gr, gr), 1)
    tril = ((ri >= ci) & (ri // CHUNK == ci // CHUNK)).astype(BF)
    row = lax.broadcasted_iota(jnp.int32, (CHUNK, CHUNK), 0)
    col = lax.broadcasted_iota(jnp.int32, (CHUNK, CHUNK), 1)
    own_block = (row // HG_SUB == col // HG_SUB) & (col % HG_SUB <= row % HG_SUB)
    gather = (lax.broadcasted_iota(jnp.int32, (CHUNK, HG_SUB * d), 0) % HG_SUB
              == lax.broadcasted_iota(jnp.int32, (CHUNK, HG_SUB * d), 1) // d).astype(BF)
    sls = [slice(c * CHUNK, (c + 1) * CHUNK) for c in range(group)]

    l0 = lbl_ref[0:1, :]
    l1 = lbl_ref[1:2, :]
    mx = jnp.maximum(l0, l1)
    e0 = jnp.exp(l0 - mx)
    e1 = jnp.exp(l1 - mx)
    s0 = e0 / (e0 + e1)
    s1 = e1 / (e0 + e1)
    lb = (s0 + s1) - s0

    def group_rows(g):
        return pl.ds(g * gr, gr)

    def prelude(hh, g):
        rows = group_rows(g)
        qs, ks, ls = [], [], []
        for sl in sls:
            qs.append(_silu(zq_ref[rows, hs(hh)][sl].astype(F32)))
            f = lb[:, hs(hh)] + (1.0 - lb[:, hs(hh)]) * _sigmoid(zf_ref[rows, hs(hh)][sl])
            ks.append(1.0 - f)
            ls.append(jnp.log2(f))
            yield
        q, k = jnp.concatenate(qs, axis=0), jnp.concatenate(ks, axis=0)
        b = _mm_exact_lhs(tril, jnp.concatenate(ls, axis=0))
        h1_s[hh, g, 0] = b
        h1_s[hh, g, 1] = k
        h1_s[hh, g, 2] = q
        qd_s[hh, rows, :] = (q * jnp.exp2(b)).astype(BF)
        yield

    def operands(hh, g):
        b, k, q = h1_s[hh, g, 0], h1_s[hh, g, 1], h1_s[hh, g, 2]
        key_row = lambda c, r, which: h1_s[hh, g, which, c * CHUNK + r:c * CHUNK + r + 1, :]
        bc = [b[sl] for sl in sls]
        b_last = [x[CHUNK - 1:CHUNK, :] for x in bc]
        wide, lhs, rhs = yield from _hgrn_intra_operands(
            [q[sl] for sl in sls], [k[sl] for sl in sls], bc, key_row)
        kdec = [(k[sl] * jnp.exp2(bl - x)).astype(BF) for sl, bl, x in zip(sls, b_last, bc)]
        dec_s[hh, pl.ds(g * group, group)] = jnp.stack([jnp.exp2(bl) for bl in b_last])
        yield
        return wide, lhs, rhs, kdec

    def products(hh, g, ops):
        wide, lhs, rhs, kdec = ops
        rows = group_rows(g)
        v = zi_ref[rows, hs(hh)]
        dmat, kv, oi = [], [], []
        for c, sl in enumerate(sls):
            diag = _mm_nt(wide[c], gather)
            yield
            off = _mm_nt(lhs[c], rhs[c])
            yield
            kv.append(_mm_tn(v[sl], kdec[c]))
            dmat.append(jnp.where(own_block, diag, 0.0) + off)
            yield
        for dm, sl in zip(dmat, sls):
            oi.append(_mm(dm, v[sl]))
            yield
        oi_s[hh, rows, :] = jnp.concatenate(oi, axis=0)
        kv_s[hh, pl.ds(g * group, group)] = jnp.stack(kv)

    def advance(*stages):
        results = [None] * len(stages)
        active = list(enumerate(stages))
        while active:
            for item in list(active):
                j, (gen, turns) = item
                try:
                    for _ in range(turns):
                        next(gen)
                except StopIteration as done:
                    results[j] = done.value
                    active.remove(item)
        return results

    for hh in heads:
        for g in range(0, ng, 2):
            advance(*[(prelude(hh, g2), 1) for g2 in range(g, min(g + 2, ng))])
        ops = advance((operands(hh, 0), 1))[0]
        for g in range(1, ng):
            _, ops = advance((products(hh, g - 1, ops), 1), (operands(hh, g), 3))
        advance((products(hh, ng - 1, ops), 1))

    per_step = 4 * group
    sr = per_step * CHUNK

    def state_step(g, sts):
        r0 = pl.multiple_of(g * sr, sr)
        rows = pl.ds(r0, sr)
        new_sts, normed = [], []
        for hh, st in zip(heads, sts):
            states = []
            for c in range(per_step):
                n = g * per_step + c
                states.append(st)
                st = st * dec_s[hh, n] + kv_s[hh, n]
            new_sts.append(st)
            outs = [oi_s[hh, pl.ds(r0 + c * CHUNK, CHUNK), :]
                    + _mm_nt(qd_s[hh, pl.ds(r0 + c * CHUNK, CHUNK), :], states[c])
                    for c in range(per_step)]
            normed.append(_rms(jnp.concatenate(outs, axis=0), on_ref[...]))
        o = jnp.concatenate(normed, axis=1)
        o_ref[rows, :] = (o * _silu(zg_ref[rows, :].astype(F32))).astype(o_ref.dtype)
        return tuple(new_sts)

    lax.fori_loop(0, t // sr, state_step, tuple(jnp.zeros((d, d), F32) for _ in heads))


def _hgrn(zb, zf, lb_logits, o_norm, b, t):
    n = zb.shape[0]
    hd = HGRN_HEAD_DIM
    nh = HGRN_HEADS
    nc = t // CHUNK
    gr = HG_GROUP * CHUNK
    hps = HG_HEADS_PER_STEP
    ns = nh // hps
    wide = hps * hd
    return pl.pallas_call(
        functools.partial(_hgrn_kernel, group=HG_GROUP),
        grid=(b, ns),
        in_specs=[pl.BlockSpec((t, wide), lambda bi, h: (bi, h)),
                  pl.BlockSpec((t, wide), lambda bi, h: (bi, ns + h)),
                  pl.BlockSpec((t, wide), lambda bi, h: (bi, 2 * ns + h)),
                  pl.BlockSpec((t, wide), lambda bi, h: (bi, h)),
                  pl.BlockSpec((lb_logits.shape[0], wide), lambda bi, h: (0, h)),
                  _const_spec(o_norm.shape)],
        out_specs=pl.BlockSpec((t, wide), lambda bi, h: (bi, h)),
        out_shape=jax.ShapeDtypeStruct((n, nh * hd), BF),
        scratch_shapes=[pltpu.VMEM((hps, t, hd), F32), pltpu.VMEM((hps, t, hd), BF),
                        pltpu.VMEM((hps, nc, hd, hd), F32), pltpu.VMEM((hps, nc, 1, hd), F32),
                        pltpu.VMEM((hps, t // gr, 3, gr, hd), F32)],
        compiler_params=_cparams(("parallel", "parallel")),
        name="hgrn2",
    )(zb, zb, zb, zf, lb_logits, o_norm)


def kernel(x, l0_mix_norm, l0_w_in, l0_fox_q_norm, l0_fox_k_norm, l0_fox_f_bias, l0_gdn_conv,
           l0_gdn_A_log, l0_gdn_dt_bias, l0_gdn_o_norm, l0_w_out, l0_ffn_norm, l0_w_ff1,
           l0_w_ff2, l1_mix_norm, l1_w_in, l1_hgrn_o_norm, l1_w_out, l1_ffn_norm, l1_w_ff1,
           l1_w_ff2, hgrn_lb_logits):
    b, t, d = x.shape
    n = b * t
    fw = FOX_HEADS * FOX_HEAD_DIM
    gw = GDN_HEADS * GDN_HEAD_DIM
    assert hgrn_lb_logits.shape[0] == 2 and d == HGRN_HEADS * HGRN_HEAD_DIM
    assert t % (2 * HG_GROUP * CHUNK) == 0
    tm = 512 if t % 512 == 0 else 256
    tq = 256
    nc = t // CHUNK
    row = lambda p: p.reshape(1, -1).astype(F32)
    x2 = x.reshape(n, d)

    o_ff = 3 * fw
    o_gq = o_ff + FOX_HEADS
    o_gb = o_gq + 3 * gw
    o_gg = o_gb + 2 * GDN_HEADS
    wb0 = jnp.concatenate([l0_w_in[:, :o_ff], l0_w_in[:, o_gq:o_gb], l0_w_in[:, o_gg:]],
                          axis=1).astype(BF)
    n_small = FOX_HEADS + 2 * GDN_HEADS
    ws0 = jnp.concatenate([l0_w_in[:, o_ff:o_gq], l0_w_in[:, o_gb:o_gg],
                           jnp.zeros((d, LANES - n_small), F32)], axis=1).astype(BF)
    hid = jnp.arange(fw) // FOX_HEAD_DIM
    pm = jnp.where(hid[:, None] == hid[None, :], 1.0 / FOX_HEAD_DIM, 0.0).astype(BF)
    qkg = jnp.stack([jnp.tile(l0_fox_q_norm, FOX_HEADS) * (FOX_HEAD_DIM ** -0.5 * LOG2E),
                     jnp.tile(l0_fox_k_norm, FOX_HEADS)]).astype(F32)

    zb0, zs0 = _in0(x2, row(l0_mix_norm), wb0, ws0, pm, qkg, l0_gdn_conv.astype(F32), tm, t)

    f_bias = jnp.pad(l0_fox_f_bias.astype(F32), (0, LANES - FOX_HEADS)).reshape(1, LANES)
    qbias, kbias = _fox_cum(zs0, f_bias, b, t)
    fox_o = _fox_attention(zb0, qbias, kbias, b, t, tq)

    def gate_rows(lo):
        g = zs0[:, lo:lo + GDN_HEADS].reshape(b, t, GDN_HEADS).transpose(0, 2, 1)
        return g.reshape(b, GDN_HEADS, nc, CHUNK)
    gb_row = gate_rows(FOX_HEADS)
    ga_row = gate_rows(FOX_HEADS + GDN_HEADS)
    gdn_o = _gdn(zb0, ga_row, ga_row.swapaxes(-1, -2), gb_row.swapaxes(-1, -2),
                 l0_gdn_A_log.reshape(GDN_HEADS, 1, 1).astype(F32),
                 l0_gdn_dt_bias.reshape(GDN_HEADS, 1, 1).astype(F32), row(l0_gdn_o_norm), b, t)

    wo0 = l0_w_out.astype(BF)
    x1 = _mlp(fox_o, gdn_o, 0, x2, wo0[:fw], wo0[fw:], row(l0_ffn_norm),
              l0_w_ff1.astype(BF), l0_w_ff2.astype(BF), tm)

    wb1 = jnp.concatenate([l1_w_in[:, :d], l1_w_in[:, 2 * d:]], axis=1).astype(BF)
    wf1 = l1_w_in[:, d:2 * d].astype(BF)
    zb1, zf1 = _in1(x1, row(l1_mix_norm), wb1, wf1, tm)
    ho = _hgrn(zb1, zf1, hgrn_lb_logits.astype(F32), row(l1_hgrn_o_norm), b, t)
    wo1 = l1_w_out.astype(BF)
    half = d // 2
    x3 = _mlp(ho, ho, 1, x1, wo1[:half], wo1[half:], row(l1_ffn_norm),
              l1_w_ff1.astype(BF), l1_w_ff2.astype(BF), tm)
    return x3.reshape(b, t, d)
```

```python
import functools

import jax
import jax.numpy as jnp
from jax import lax
from jax.experimental import pallas as pl
from jax.experimental.pallas import tpu as pltpu

BF = jnp.bfloat16
F32 = jnp.float32
NORM_EPS = 1e-6
LOG2E = 1.4426950408889634
CHUNK = 64
FOX_HEADS, FOX_HEAD_DIM = 8, 64
GDN_HEADS, GDN_HEAD_DIM = 4, 128
HGRN_HEADS, HGRN_HEAD_DIM = 8, 128
GDN_CONV = 4
IN0_ROWS = 128
LANES = 128
SUBLANES = 8
VMEM_LIMIT = 56 * 1024 * 1024


def _cparams(sem):
    return pltpu.CompilerParams(dimension_semantics=sem, vmem_limit_bytes=VMEM_LIMIT)


def _const_spec(shape):
    nd = len(shape)
    return pl.BlockSpec(shape, lambda *_: (0,) * nd, pipeline_mode=pl.Buffered(1))


def _mm(a, b):
    return jnp.dot(a.astype(BF), b.astype(BF), preferred_element_type=F32)


def _mm_nt(a, b):
    return lax.dot_general(a.astype(BF), b.astype(BF), (((1,), (1,)), ((), ())),
                           preferred_element_type=F32)


def _mm_tn(a, b):
    return lax.dot_general(a.astype(BF), b.astype(BF), (((0,), (0,)), ((), ())),
                           preferred_element_type=F32)


def _split3(x):
    hi = x.astype(BF)
    r1 = x - hi.astype(F32)
    mid = r1.astype(BF)
    lo = (r1 - mid.astype(F32)).astype(BF)
    return hi, mid, lo


def _mm_exact_lhs(m_bf, x):
    n = x.shape[1]
    if n % LANES:
        hi, mid, lo = _split3(x)
        d = lambda p: jnp.dot(m_bf, p, preferred_element_type=F32)
        return d(hi) + d(mid) + d(lo)
    y = jnp.dot(m_bf, jnp.concatenate(_split3(x), axis=1), preferred_element_type=F32)
    return y[:, :n] + y[:, n:2 * n] + y[:, 2 * n:]


def _mm_exact_rhs(x, m_bf):
    hi, mid, lo = _split3(x)
    d = lambda p: jnp.dot(p, m_bf, preferred_element_type=F32)
    return d(hi) + d(mid) + d(lo)


def _sigmoid(x):
    return 1.0 / (1.0 + jnp.exp(-x))


def _silu(x):
    return x * _sigmoid(x)


def _softplus(x):
    return jnp.maximum(x, 0.0) + jnp.log1p(jnp.exp(-jnp.abs(x)))


def _rms(x, gain):
    return x * lax.rsqrt(jnp.mean(x * x, axis=-1, keepdims=True) + NORM_EPS) * gain


def _in0_kernel(x_ref, g_ref, wb_ref, ws_ref, pm_ref, qkg_ref, cw_ref, zb_ref, zs_ref, hist_ref,
                *, tiles_per_seq):
    tm = x_ref.shape[0]
    hd = GDN_HEAD_DIM
    hrows = hist_ref.shape[1]

    @pl.when(pl.program_id(0) % tiles_per_seq == 0)
    def _():
        hist_ref[...] = jnp.zeros_like(hist_ref)

    h = _rms(x_ref[...], g_ref[...]).astype(BF)
    gw = 512
    rb = IN0_ROWS

    def project(c, r):
        return jnp.dot(h[r * rb:(r + 1) * rb], wb_ref[:, c * gw:(c + 1) * gw],
                       preferred_element_type=F32)

    def finish(c, r, z, above):
        if c < 2:
            ms = jnp.dot((z * z).astype(BF), pm_ref[...], preferred_element_type=F32)
            z = z * lax.rsqrt(ms + NORM_EPS) * qkg_ref[c:c + 1, :]
        elif 3 <= c < 6:
            j = c - 3
            w = cw_ref[:, j * gw:(j + 1) * gw]
            ext = jnp.concatenate([above, z], axis=0)
            y = z * w[GDN_CONV - 1:GDN_CONV]
            for s in range(1, GDN_CONV):
                y = y + ext[hrows - s:hrows - s + rb] * w[GDN_CONV - 1 - s:GDN_CONV - s]
            z = _silu(y)
            if j < 2:
                scale = 1.0 if j else hd ** -0.5
                z = jnp.concatenate(
                    [zh * (lax.rsqrt(jnp.sum(zh * zh, axis=-1, keepdims=True) + NORM_EPS) * scale)
                     for zh in (z[:, k * hd:(k + 1) * hd] for k in range(gw // hd))], axis=1)
        zb_ref[r * rb:(r + 1) * rb, c * gw:(c + 1) * gw] = z.astype(BF)

    pending = None
    for c in range(wb_ref.shape[1] // gw):
        conv = 3 <= c < 6
        above = hist_ref[c - 3] if conv else None
        for r in range(tm // rb):
            z = project(c, r)
            if pending is not None:
                finish(*pending)
            pending = (c, r, z, above)
            above = z[rb - hrows:, :] if conv else None
        if conv:
            hist_ref[c - 3] = above
    finish(*pending)
    zs_ref[...] = jnp.dot(h, ws_ref[...], preferred_element_type=F32)


def _in0(x2, gain, wb, ws, pm, qkg, conv_w, tm, t):
    n, d = x2.shape
    nb = wb.shape[1]
    return pl.pallas_call(
        functools.partial(_in0_kernel, tiles_per_seq=t // tm),
        grid=(n // tm,),
        in_specs=[pl.BlockSpec((tm, d), lambda i: (i, 0)),
                  _const_spec((1, d)), _const_spec(wb.shape), _const_spec(ws.shape),
                  _const_spec(pm.shape), _const_spec(qkg.shape), _const_spec(conv_w.shape)],
        out_specs=[pl.BlockSpec((tm, nb), lambda i: (i, 0)),
                   pl.BlockSpec((tm, LANES), lambda i: (i, 0))],
        out_shape=[jax.ShapeDtypeStruct((n, nb), BF), jax.ShapeDtypeStruct((n, LANES), F32)],
        scratch_shapes=[pltpu.VMEM((3, SUBLANES, 512), F32)],
        compiler_params=_cparams(("arbitrary",)),
        name="l0_in_proj",
    )(x2, gain, wb, ws, pm, qkg, conv_w)


FOX_BIAS_LANES = 8


def _fox_cum_kernel(zs_ref, bias_ref, qb_ref, kb_ref, *, blk):
    t = zs_ref.shape[0]
    r = lax.broadcasted_iota(jnp.int32, (blk, blk), 0)
    c = lax.broadcasted_iota(jnp.int32, (blk, blk), 1)
    tril = (r >= c).astype(BF)
    pr = lax.broadcasted_iota(jnp.int32, (3 * LANES, 2 * LANES), 0)
    pc = lax.broadcasted_iota(jnp.int32, (3 * LANES, 2 * LANES), 1)
    place = ((pr % LANES < FOX_HEADS)
             & (pc % LANES == (pr % LANES) * FOX_BIAS_LANES + pr // LANES + 3 * (pc // LANES))
             ).astype(BF)
    lane = lax.broadcasted_iota(jnp.int32, (1, LANES), 1)
    slot = lane % FOX_BIAS_LANES
    used = lane < FOX_HEADS * FOX_BIAS_LANES
    ones_q = jnp.where(used & (slot >= 3) & (slot < 6), 1.0, 0.0)
    ones_k = jnp.where(used & (slot < 3), 1.0, 0.0)
    blocks = [slice(i * blk, (i + 1) * blk) for i in range(t // blk)]
    local = []
    for rows in blocks:
        z = zs_ref[rows, :] + bias_ref[...]
        ls = jnp.minimum(z, 0.0) - jnp.log1p(jnp.exp(-jnp.abs(z)))
        local.append(_mm_exact_lhs(tril, ls))
    carry = jnp.zeros((1, LANES), F32)
    for rows, loc in zip(blocks, local):
        cb = loc + carry
        carry = cb[blk - 1:blk, :]
        feat = jnp.dot(jnp.concatenate(_split3(cb * LOG2E), axis=1), place,
                       preferred_element_type=F32)
        qb_ref[rows, :] = (feat[:, :LANES] + ones_q).astype(BF)
        kb_ref[rows, :] = (ones_k - feat[:, LANES:]).astype(BF)


def _fox_cum(zs, bias_row, b, t):
    n = zs.shape[0]
    spec = pl.BlockSpec((t, LANES), lambda bi: (bi, 0))
    return pl.pallas_call(
        functools.partial(_fox_cum_kernel, blk=256),
        grid=(b,),
        in_specs=[spec, _const_spec(bias_row.shape)],
        out_specs=[spec, spec],
        out_shape=[jax.ShapeDtypeStruct((n, LANES), BF)] * 2,
        compiler_params=_cparams(("parallel",)),
        name="fox_cum_gate",
    )(zs, bias_row)


def _fox_kernel(q_ref, k_ref, v_ref, qb_ref, kb_ref, o_ref, *, tq):
    pair = pl.program_id(1)
    t = q_ref.shape[0]
    lane = lax.broadcasted_iota(jnp.int32, (1, LANES), 1)
    zero = jnp.zeros((tq, LANES), BF)
    first = lane < FOX_HEAD_DIM
    bias_head = lane // FOX_BIAS_LANES
    ones = jnp.ones((tq, LANES), BF)
    on_diag = (lax.broadcasted_iota(jnp.int32, (2 * tq, tq), 1)
               <= lax.broadcasted_iota(jnp.int32, (2 * tq, tq), 0) % tq)

    def scores(i):
        rows_i = slice(i * tq, (i + 1) * tq)
        kv = (i + 1) * tq
        q2, qb = q_ref[rows_i, :], qb_ref[rows_i, :]
        rows = []
        for h in range(2):
            qh = jnp.where(first if h == 0 else jnp.logical_not(first), q2, zero)
            qbh = jnp.where(bias_head == 2 * pair + h, qb, zero)
            rows.append(jnp.concatenate([qh, qbh], axis=1))
        ka = jnp.concatenate([k_ref[:kv, :], kb_ref[:kv, :]], axis=1)
        s = lax.dot_general(jnp.concatenate(rows, axis=0), ka, (((1,), (1,)), ((), ())),
                            preferred_element_type=F32)
        diag = jnp.where(on_diag, s[:, kv - tq:], -jnp.inf)
        return diag if i == 0 else jnp.concatenate([s[:, :kv - tq], diag], axis=1)

    def finish(i, s):
        kv = (i + 1) * tq
        p = jnp.exp2(s - jnp.max(s, axis=-1, keepdims=True)).astype(BF)
        va = jnp.concatenate([v_ref[:kv, :], jnp.concatenate([ones] * (i + 1), axis=0)], axis=1)
        acc = jnp.dot(p, va, preferred_element_type=F32)
        o = acc[:, :LANES] / acc[:, LANES:]
        o_ref[i * tq:(i + 1) * tq, :] = jnp.where(first, o[:tq], o[tq:]).astype(o_ref.dtype)

    s = scores(0)
    for i in range(t // tq):
        s_next = scores(i + 1) if (i + 1) * tq < t else None
        finish(i, s)
        s = s_next


def _fox_attention(zb, qbias, kbias, b, t, tq):
    n = zb.shape[0]
    hp = FOX_HEADS // 2
    return pl.pallas_call(
        functools.partial(_fox_kernel, tq=tq),
        grid=(b, hp),
        in_specs=[pl.BlockSpec((t, LANES), lambda bi, p: (bi, p)),
                  pl.BlockSpec((t, LANES), lambda bi, p: (bi, hp + p)),
                  pl.BlockSpec((t, LANES), lambda bi, p: (bi, 2 * hp + p)),
                  pl.BlockSpec((t, LANES), lambda bi, p: (bi, 0)),
                  pl.BlockSpec((t, LANES), lambda bi, p: (bi, 0))],
        out_specs=pl.BlockSpec((t, LANES), lambda bi, p: (bi, p)),
        out_shape=jax.ShapeDtypeStruct((n, FOX_HEADS * FOX_HEAD_DIM), BF),
        compiler_params=_cparams(("parallel", "parallel")),
        name="fox_attention",
    )(zb, zb, zb, qbias, kbias)


def _neumann_inverse(mats, eye):
    ps = [eye - a for a in mats]
    aps = list(mats)
    k = 2
    while k < CHUNK:
        aps = [_mm(ap, ap) for ap in aps]
        ps = [p + _mm(p, ap) for p, ap in zip(ps, aps)]
        k *= 2
    return ps


def _gdn_kernel(alog_ref, dtb_ref, q_ref, k_ref, v_ref, gg_ref, gar_ref, gac_ref, gbc_ref,
                on_ref, o_ref, lhs_s, ku_s, u_s, qk_s, pend_s, grow_s, gcol_s, bcol_s, s_s):
    t = q_ref.shape[0]
    nc = t // CHUNK
    hd = GDN_HEAD_DIM
    lrows = hd + 2 * CHUNK

    ri = lax.broadcasted_iota(jnp.int32, (CHUNK, CHUNK), 0)
    ci = lax.broadcasted_iota(jnp.int32, (CHUNK, CHUNK), 1)
    lower = ri >= ci
    strict = ri > ci
    tril = lower.astype(BF)
    triu = (ri <= ci).astype(BF)
    eye = (ri == ci).astype(F32)
    for h in range(GDN_HEADS):
        a = -jnp.exp(alog_ref[h])
        grow_s[h] = _mm_exact_rhs(a * _softplus(gar_ref[0, h] + dtb_ref[h]), triu)
        gcol_s[h] = _mm_exact_lhs(tril, a * _softplus(gac_ref[0, h] + dtb_ref[h]))
        bcol_s[h] = _sigmoid(gbc_ref[0, h])
    s_s[...] = jnp.zeros_like(s_s)

    lane_nc = lax.broadcasted_iota(jnp.int32, (CHUNK, nc), 1)
    heads = range(GDN_HEADS)
    cat = lambda parts: jnp.concatenate(parts, axis=1)

    per_iter = 8

    def local_step(it, carry):
        r0 = pl.multiple_of(it * (per_iter * CHUNK), per_iter * CHUNK)
        rows = pl.ds(r0, per_iter * CHUNK)
        chains = [(c, h) for c in range(per_iter) for h in heads]
        csl = lambda c: slice(c * CHUNK, (c + 1) * CHUNK)
        hsl = lambda h: slice(h * hd, (h + 1) * hd)
        q_all, k_all, v_all = (r[rows, :].astype(F32) for r in (q_ref, k_ref, v_ref))
        q = [q_all[csl(c), hsl(h)] for c, h in chains]
        k = [k_all[csl(c), hsl(h)] for c, h in chains]
        v = [v_all[csl(c), hsl(h)] for c, h in chains]
        sel = [lane_nc == it * per_iter + c for c in range(per_iter)]
        g_c = [jnp.sum(jnp.where(sel[c], gcol_s[h], 0.0), axis=1, keepdims=True)
               for c, h in chains]
        b_c = [jnp.sum(jnp.where(sel[c], bcol_s[h], 0.0), axis=1, keepdims=True)
               for c, h in chains]
        g_r = [grow_s[h, pl.ds(it * per_iter + c, 1), :] for c, h in chains]
        decay = [jnp.exp(jnp.where(lower, gc - gr, -jnp.inf)) for gc, gr in zip(g_c, g_r)]
        kb = [x * bc for x, bc in zip(k, b_c)]
        kk = [_mm_nt(x, y) for x, y in zip(kb, k)]
        qk = [_mm_nt(x, y) for x, y in zip(q, k)]
        t_inv = _neumann_inverse([jnp.where(strict, x * d, 0.0) for x, d in zip(kk, decay)], eye)
        eg = [jnp.exp(gc) for gc in g_c]
        uw = [_mm(ti, jnp.concatenate([x * bc, y * e], axis=1))
              for ti, x, bc, y, e in zip(t_inv, v, b_c, kb, eg)]
        kd = [x * jnp.exp(gr[:, CHUNK - 1:CHUNK] - gc) for x, gr, gc in zip(k, g_r, g_c)]
        kuw = [_mm_tn(x, y) for x, y in zip(kd, uw)]
        grid = lambda parts: jnp.concatenate(
            [cat(parts[c * GDN_HEADS:(c + 1) * GDN_HEADS]) for c in range(per_iter)], axis=0)
        lhs = [jnp.concatenate([m[:, hd:], x[:, hd:], y * e], axis=0)
               for m, x, y, e in zip(kuw, uw, q, eg)]
        lhs_s[pl.ds(pl.multiple_of(it * (per_iter * lrows), per_iter * lrows),
                    per_iter * lrows), :] = grid(lhs).astype(BF)
        ku_s[pl.ds(pl.multiple_of(it * (per_iter * hd), per_iter * hd), per_iter * hd), :] = (
            grid([m[:, :hd] for m in kuw]))
        u_s[rows, :] = grid([x[:, :hd] for x in uw])
        qk_s[rows, :] = grid([x * d for x, d in zip(qk, decay)]).astype(BF)
        return carry

    lax.fori_loop(0, nc // per_iter, local_step, 0)

    hsl = lambda h: slice(h * hd, (h + 1) * hd)

    def chunk_rows(n, size):
        return pl.ds(n * size if isinstance(n, int) else pl.multiple_of(n * size, size), size)

    def advance(n):
        s_prev = [s_s[h] for h in heads]
        prod = [_mm(lhs_s[chunk_rows(n, lrows), hsl(h)], s_prev[h]) for h in heads]
        g_last = [grow_s[h, pl.ds(n, 1), :][:, CHUNK - 1:CHUNK] for h in heads]
        s_s[...] = jnp.stack([s_prev[h] * jnp.exp(g_last[h]) - prod[h][:hd]
                              + ku_s[chunk_rows(n, hd), hsl(h)] for h in heads])
        pend_s[...] = cat([prod[h][hd:] for h in heads])

    def emit(n):
        rows = chunk_rows(n, CHUNK)
        pend = pend_s[...]
        v_new = [u_s[rows, hsl(h)] - pend[:CHUNK, hsl(h)] for h in heads]
        qv = [_mm(qk_s[rows, h * CHUNK:(h + 1) * CHUNK], v_new[h]) for h in heads]
        outs = [_rms(pend[CHUNK:, hsl(h)] + qv[h], on_ref[...]) for h in heads]
        gate = _silu(gg_ref[rows, :].astype(F32))
        o_ref[rows, :] = (cat(outs) * gate).astype(o_ref.dtype)

    def state_step(n, carry):
        emit(n - 1)
        advance(n)
        return carry

    advance(0)
    lax.fori_loop(1, nc, state_step, 0)
    emit(nc - 1)


def _gdn(zb, ga_row, ga_col, gb_col, a_log, dt_bias, o_norm, b, t):
    n = zb.shape[0]
    nc = t // CHUNK
    width = GDN_HEADS * GDN_HEAD_DIM
    fox_blocks = 3 * FOX_HEADS * FOX_HEAD_DIM // width
    colspec = lambda j: pl.BlockSpec((t, width), lambda bi: (bi, fox_blocks + j))
    gspec = lambda shp: pl.BlockSpec((1,) + shp, lambda bi: (bi, 0, 0, 0))
    return pl.pallas_call(
        _gdn_kernel,
        grid=(b,),
        in_specs=[_const_spec(a_log.shape), _const_spec(dt_bias.shape),
                  colspec(0), colspec(1), colspec(2), colspec(3),
                  gspec((GDN_HEADS, nc, CHUNK)), gspec((GDN_HEADS, CHUNK, nc)),
                  gspec((GDN_HEADS, CHUNK, nc)), _const_spec(o_norm.shape)],
        out_specs=pl.BlockSpec((t, width), lambda bi: (bi, 0)),
        out_shape=jax.ShapeDtypeStruct((n, width), BF),
        scratch_shapes=[pltpu.VMEM((nc * (GDN_HEAD_DIM + 2 * CHUNK), width), BF),
                        pltpu.VMEM((nc * GDN_HEAD_DIM, width), F32),
                        pltpu.VMEM((t, width), F32),
                        pltpu.VMEM((t, GDN_HEADS * CHUNK), BF),
                        pltpu.VMEM((2 * CHUNK, width), F32),
                        pltpu.VMEM((GDN_HEADS, nc, CHUNK), F32),
                        pltpu.VMEM((GDN_HEADS, CHUNK, nc), F32),
                        pltpu.VMEM((GDN_HEADS, CHUNK, nc), F32),
                        pltpu.VMEM((GDN_HEADS, GDN_HEAD_DIM, GDN_HEAD_DIM), F32)],
        compiler_params=_cparams(("parallel",)),
        name="gated_deltanet",
    )(a_log, dt_bias, zb, zb, zb, zb, ga_row, ga_col, gb_col, o_norm)


def _mlp_kernel(a1_ref, a2_ref, x_ref, wo1_ref, wo2_ref, g_ref, w1_ref, w2_ref, o_ref, acc_ref,
                *, fc):
    mix = (jnp.dot(a1_ref[...], wo1_ref[...], preferred_element_type=F32)
           + jnp.dot(a2_ref[...], wo2_ref[...], preferred_element_type=F32))
    x1 = x_ref[...] + mix
    h = _rms(x1, g_ref[...]).astype(BF)
    acc_ref[...] = x1
    for c in range(w1_ref.shape[1] // fc):
        a = jnp.maximum(jnp.dot(h, w1_ref[:, c * fc:(c + 1) * fc],
                                preferred_element_type=F32), 0.0)
        acc_ref[...] += jnp.dot((a * a).astype(BF), w2_ref[c * fc:(c + 1) * fc, :],
                                preferred_element_type=F32)
    o_ref[...] = acc_ref[...]


def _mlp(a1, a2, a2_block, x2, wo1, wo2, gain, w1, w2, tm):
    n, d = x2.shape
    half = wo1.shape[0]
    return pl.pallas_call(
        functools.partial(_mlp_kernel, fc=1024),
        grid=(n // tm,),
        in_specs=[pl.BlockSpec((tm, half), lambda i: (i, 0)),
                  pl.BlockSpec((tm, half), lambda i: (i, a2_block)),
                  pl.BlockSpec((tm, d), lambda i: (i, 0)),
                  _const_spec(wo1.shape), _const_spec(wo2.shape), _const_spec(gain.shape),
                  _const_spec(w1.shape), _const_spec(w2.shape)],
        out_specs=pl.BlockSpec((tm, d), lambda i: (i, 0)),
        out_shape=jax.ShapeDtypeStruct((n, d), F32),
        scratch_shapes=[pltpu.VMEM((tm, d), F32)],
        compiler_params=_cparams(("parallel",)),
        name="out_proj_mlp",
    )(a1, a2, x2, wo1, wo2, gain, w1, w2)


def _in1_kernel(x_ref, g_ref, wb_ref, wf_ref, zb_ref, zf_ref):
    h = _rms(x_ref[...], g_ref[...]).astype(BF)
    gw = 512
    for c in range(wb_ref.shape[1] // gw):
        zb_ref[:, c * gw:(c + 1) * gw] = jnp.dot(
            h, wb_ref[:, c * gw:(c + 1) * gw], preferred_element_type=F32).astype(BF)
    for c in range(wf_ref.shape[1] // gw):
        zf_ref[:, c * gw:(c + 1) * gw] = jnp.dot(
            h, wf_ref[:, c * gw:(c + 1) * gw], preferred_element_type=F32)


def _in1(x2, gain, wb, wf, tm):
    n, d = x2.shape
    return pl.pallas_call(
        _in1_kernel,
        grid=(n // tm,),
        in_specs=[pl.BlockSpec((tm, d), lambda i: (i, 0)),
                  _const_spec(gain.shape), _const_spec(wb.shape), _const_spec(wf.shape)],
        out_specs=[pl.BlockSpec((tm, wb.shape[1]), lambda i: (i, 0)),
                   pl.BlockSpec((tm, wf.shape[1]), lambda i: (i, 0))],
        out_shape=[jax.ShapeDtypeStruct((n, wb.shape[1]), BF),
                   jax.ShapeDtypeStruct((n, wf.shape[1]), F32)],
        compiler_params=_cparams(("parallel",)),
        name="l1_in_proj",
    )(x2, gain, wb, wf)


HG_GROUP = 4
HG_HEADS_PER_STEP = 1


HG_LEVELS = (32, 16, 8, 4, 2, 1)


def _hgrn_term_mask(terms):
    t = lax.broadcasted_iota(jnp.int32, (CHUNK, len(terms) * CHUNK), 0)
    col = lax.broadcasted_iota(jnp.int32, (CHUNK, len(terms) * CHUNK), 1)
    s = col % CHUNK
    mask = None
    for j, m in enumerate(terms):
        own = (t == s) if m == 0 else (
            (t % (2 * m) >= m) & (s % (2 * m) < m) & (t // (2 * m) == s // (2 * m)))
        own = own & (col // CHUNK == j)
        mask = own if mask is None else mask | own
    return mask


def _hgrn_level_operands(q, k, b, ref_row):
    d = q.shape[1]
    nb = CHUNK // SUBLANES
    blk = lambda x, i: x[SUBLANES * i:SUBLANES * (i + 1)]
    sub = lax.broadcasted_iota(jnp.int32, (SUBLANES, d), 0)
    xs = []
    for m in HG_LEVELS:
        parts = []
        for i in range(nb):
            bi, qi, ki = blk(b, i), blk(q, i), blk(k, i)
            row0 = SUBLANES * i
            if m >= SUBLANES:
                ref = row0 // (2 * m) * (2 * m) + m
                if row0 % (2 * m) >= m:
                    parts.append(qi * jnp.exp2(bi - ref_row(ref)))
                else:
                    parts.append(ki * jnp.exp2(ref_row(ref) - bi))
                continue
            upper = sub % (2 * m) >= m
            if m == 4:
                bref = ref_row(row0 + 4)
            elif m == 2:
                bref = jnp.where(sub < 4, ref_row(row0 + 2), ref_row(row0 + 6))
            else:
                bref = jnp.where(upper, bi, pltpu.roll(bi, SUBLANES - 1, 0))
            dlt = bi - bref
            parts.append(jnp.where(upper, qi, ki) * jnp.exp2(jnp.where(upper, dlt, -dlt)))
        xs.append(jnp.concatenate(parts, axis=0).astype(BF))
        yield
    return xs


def _hgrn_kernel(zq_ref, zi_ref, zg_ref, zf_ref, lbl_ref, on_ref, o_ref,
                 oi_s, qd_s, kv_s, dec_s, h1_s, *, group):
    t = zq_ref.shape[0]
    d = HGRN_HEAD_DIM
    heads = range(zq_ref.shape[1] // d)
    hs = lambda hh: slice(hh * d, (hh + 1) * d)
    gr = group * CHUNK
    ng = t // gr
    ri = lax.broadcasted_iota(jnp.int32, (gr, gr), 0)
    ci = lax.broadcasted_iota(jnp.int32, (gr, gr), 1)
    tril = ((ri >= ci) & (ri // CHUNK == ci // CHUNK)).astype(BF)
    term_mask = {m: _hgrn_term_mask((m,)) for m in (0,) + HG_LEVELS}
    sls =[slice(c * CHUNK, (c + 1) * CHUNK) for c in range(group)]

    l0 = lbl_ref[0:1, :]
    l1 = lbl_ref[1:2, :]
    mx = jnp.maximum(l0, l1)
    e0 = jnp.exp(l0 - mx)
    e1 = jnp.exp(l1 - mx)
    s0 = e0 / (e0 + e1)
    s1 = e1 / (e0 + e1)
    lb = (s0 + s1) - s0

    def group_rows(g):
        return pl.ds(g * gr, gr)

    def prelude(hh, g):
        rows = group_rows(g)
        qs, ks, ls = [], [], []
        for sl in sls:
            qs.append(_silu(zq_ref[rows, hs(hh)][sl].astype(F32)))
            f = lb[:, hs(hh)] + (1.0 - lb[:, hs(hh)]) * _sigmoid(zf_ref[rows, hs(hh)][sl])
            ks.append(1.0 - f)
            ls.append(jnp.log2(f))
            yield
        q, k = jnp.concatenate(qs, axis=0), jnp.concatenate(ks, axis=0)
        b = _mm_exact_lhs(tril, jnp.concatenate(ls, axis=0))
        h1_s[hh, g, 0] = b
        h1_s[hh, g, 1] = k
        h1_s[hh, g, 2] = q
        qd_s[hh, rows, :] = (q * jnp.exp2(b)).astype(BF)
        yield

    def operands(hh, g):
        b, k, q = h1_s[hh, g, 0], h1_s[hh, g, 1], h1_s[hh, g, 2]
        ops = []
        for c, sl in enumerate(sls):
            ref_row = lambda r, c=c: h1_s[hh, g, 0, c * CHUNK + r:c * CHUNK + r + 1, :]
            xs = yield from _hgrn_level_operands(q[sl], k[sl], b[sl], ref_row)
            b_last = b[sl][CHUNK - 1:CHUNK, :]
            kdec = (k[sl] * jnp.exp2(b_last - b[sl])).astype(BF)
            dec_s[hh, g * group + c] = jnp.exp2(b_last)
            ops.append((xs, q[sl].astype(BF), k[sl].astype(BF), kdec))
            yield
        return ops

    def products(hh, g, ops):
        rows = group_rows(g)
        v = zi_ref[rows, hs(hh)]
        dmat, kv, oi = [], [], []
        for (xs, qb, kb, kdec), sl in zip(ops, sls):
            dm = jnp.where(term_mask[0], _mm_nt(qb, kb), 0.0)
            yield
            for m, x in zip(HG_LEVELS, xs):
                dm = jnp.where(term_mask[m], _mm_nt(x, x), dm)
                yield
            kv.append(_mm_tn(v[sl], kdec))
            dmat.append(dm)
            yield
        for dm, sl in zip(dmat, sls):
            oi.append(_mm(dm, v[sl]))
            yield
        oi_s[hh, rows, :] = jnp.concatenate(oi, axis=0)
        kv_s[hh, pl.ds(g * group, group)] = jnp.stack(kv)

    def advance(*stages):
        results = [None] * len(stages)
        active = list(enumerate(stages))
        while active:
            for item in list(active):
                j, (gen, turns) = item
                try:
                    for _ in range(turns):
                        next(gen)
                except StopIteration as done:
                    results[j] = done.value
                    active.remove(item)
        return results

    for hh in heads:
        advance((prelude(hh, 0), 1), (prelude(hh, 1), 1))
        ops = advance((operands(hh, 0), 1))[0]
        for g in range(1, ng):
            stages = [(products(hh, g - 1, ops), 1), (operands(hh, g), 1)]
            if g + 1 < ng:
                stages.append((prelude(hh, g + 1), 1))
            ops = advance(*stages)[1]
        advance((products(hh, ng - 1, ops), 1))

    per_step = 4 * group
    sr = per_step * CHUNK

    def state_step(g, sts):
        r0 = pl.multiple_of(g * sr, sr)
        rows = pl.ds(r0, sr)
        new_sts, normed = [], []
        for hh, st in zip(heads, sts):
            states = []
            for c in range(per_step):
                n = g * per_step + c
                states.append(st)
                st = st * dec_s[hh, n] + kv_s[hh, n]
            new_sts.append(st)
            outs = [oi_s[hh, pl.ds(r0 + c * CHUNK, CHUNK), :]
                    + _mm_nt(qd_s[hh, pl.ds(r0 + c * CHUNK, CHUNK), :], states[c])
                    for c in range(per_step)]
            normed.append(_rms(jnp.concatenate(outs, axis=0), on_ref[...]))
        o = jnp.concatenate(normed, axis=1)
        o_ref[rows, :] = (o * _silu(zg_ref[rows, :].astype(F32))).astype(o_ref.dtype)
        return tuple(new_sts)

    lax.fori_loop(0, t // sr, state_step, tuple(jnp.zeros((d, d), F32) for _ in heads))


def _hgrn(zb, zf, lb_logits, o_norm, b, t):
    n = zb.shape[0]
    hd = HGRN_HEAD_DIM
    nh = HGRN_HEADS
    nc = t // CHUNK
    gr = HG_GROUP * CHUNK
    hps = HG_HEADS_PER_STEP
    ns = nh // hps
    wide = hps * hd
    return pl.pallas_call(
        functools.partial(_hgrn_kernel, group=HG_GROUP),
        grid=(b, ns),
        in_specs=[pl.BlockSpec((t, wide), lambda bi, h: (bi, h)),
                  pl.BlockSpec((t, wide), lambda bi, h: (bi, ns + h)),
                  pl.BlockSpec((t, wide), lambda bi, h: (bi, 2 * ns + h)),
                  pl.BlockSpec((t, wide), lambda bi, h: (bi, h)),
                  pl.BlockSpec((lb_logits.shape[0], wide), lambda bi, h: (0, h)),
                  _const_spec(o_norm.shape)],
        out_specs=pl.BlockSpec((t, wide), lambda bi, h: (bi, h)),
        out_shape=jax.ShapeDtypeStruct((n, nh * hd), BF),
        scratch_shapes=[pltpu.VMEM((hps, t, hd), F32), pltpu.VMEM((hps, t, hd), BF),
                        pltpu.VMEM((hps, nc, hd, hd), F32), pltpu.VMEM((hps, nc, 1, hd), F32),
                        pltpu.VMEM((hps, t // gr, 3, gr, hd), F32)],
        compiler_params=_cparams(("parallel", "parallel")),
        name="hgrn2",
    )(zb, zb, zb, zf, lb_logits, o_norm)


def kernel(x, l0_mix_norm, l0_w_in, l0_fox_q_norm, l0_fox_k_norm, l0_fox_f_bias, l0_gdn_conv,
           l0_gdn_A_log, l0_gdn_dt_bias, l0_gdn_o_norm, l0_w_out, l0_ffn_norm, l0_w_ff1,
           l0_w_ff2, l1_mix_norm, l1_w_in, l1_hgrn_o_norm, l1_w_out, l1_ffn_norm, l1_w_ff1,
           l1_w_ff2, hgrn_lb_logits):
    b, t, d = x.shape
    n = b * t
    fw = FOX_HEADS * FOX_HEAD_DIM
    gw = GDN_HEADS * GDN_HEAD_DIM
    assert hgrn_lb_logits.shape[0] == 2 and d == HGRN_HEADS * HGRN_HEAD_DIM
    assert t % (2 * HG_GROUP * CHUNK) == 0
    tm = 512 if t % 512 == 0 else 256
    tq = 256
    nc = t // CHUNK
    row = lambda p: p.reshape(1, -1).astype(F32)
    x2 = x.reshape(n, d)

    o_ff = 3 * fw
    o_gq = o_ff + FOX_HEADS
    o_gb = o_gq + 3 * gw
    o_gg = o_gb + 2 * GDN_HEADS
    wb0 = jnp.concatenate([l0_w_in[:, :o_ff], l0_w_in[:, o_gq:o_gb], l0_w_in[:, o_gg:]],
                          axis=1).astype(BF)
    n_small = FOX_HEADS + 2 * GDN_HEADS
    ws0 = jnp.concatenate([l0_w_in[:, o_ff:o_gq], l0_w_in[:, o_gb:o_gg],
                           jnp.zeros((d, LANES - n_small), F32)], axis=1).astype(BF)
    hid = jnp.arange(fw) // FOX_HEAD_DIM
    pm = jnp.where(hid[:, None] == hid[None, :], 1.0 / FOX_HEAD_DIM, 0.0).astype(BF)
    qkg = jnp.stack([jnp.tile(l0_fox_q_norm, FOX_HEADS) * (FOX_HEAD_DIM ** -0.5 * LOG2E),
                     jnp.tile(l0_fox_k_norm, FOX_HEADS)]).astype(F32)

    zb0, zs0 = _in0(x2, row(l0_mix_norm), wb0, ws0, pm, qkg, l0_gdn_conv.astype(F32), tm, t)

    f_bias = jnp.pad(l0_fox_f_bias.astype(F32), (0, LANES - FOX_HEADS)).reshape(1, LANES)
    qbias, kbias = _fox_cum(zs0, f_bias, b, t)
    fox_o = _fox_attention(zb0, qbias, kbias, b, t, tq)

    def gate_rows(lo):
        g = zs0[:, lo:lo + GDN_HEADS].reshape(b, t, GDN_HEADS).transpose(0, 2, 1)
        return g.reshape(b, GDN_HEADS, nc, CHUNK)
    gb_row = gate_rows(FOX_HEADS)
    ga_row = gate_rows(FOX_HEADS + GDN_HEADS)
    gdn_o = _gdn(zb0, ga_row, ga_row.swapaxes(-1, -2), gb_row.swapaxes(-1, -2),
                 l0_gdn_A_log.reshape(GDN_HEADS, 1, 1).astype(F32),
                 l0_gdn_dt_bias.reshape(GDN_HEADS, 1, 1).astype(F32), row(l0_gdn_o_norm), b, t)

    wo0 = l0_w_out.astype(BF)
    x1 = _mlp(fox_o, gdn_o, 0, x2, wo0[:fw], wo0[fw:], row(l0_ffn_norm),
              l0_w_ff1.astype(BF), l0_w_ff2.astype(BF), tm)

    wb1 = jnp.concatenate([l1_w_in[:, :d], l1_w_in[:, 2 * d:]], axis=1).astype(BF)
    wf1 = l1_w_in[:, d:2 * d].astype(BF)
    zb1, zf1 = _in1(x1, row(l1_mix_norm), wb1, wf1, tm)
    ho = _hgrn(zb1, zf1, hgrn_lb_logits.astype(F32), row(l1_hgrn_o_norm), b, t)
    wo1 = l1_w_out.astype(BF)
    half = d // 2
    x3 = _mlp(ho, ho, 1, x1, wo1[:half], wo1[half:], row(l1_ffn_norm),
              l1_w_ff1.astype(BF), l1_w_ff2.astype(BF), tm)
    return x3.reshape(b, t, d)
```

```python
import functools

import jax
import jax.numpy as jnp
from jax import lax
from jax.experimental import pallas as pl
from jax.experimental.pallas import tpu as pltpu

BF = jnp.bfloat16
F32 = jnp.float32
NORM_EPS = 1e-6
LOG2E = 1.4426950408889634
CHUNK = 64
FOX_HEADS, FOX_HEAD_DIM = 8, 64
GDN_HEADS, GDN_HEAD_DIM = 4, 128
HGRN_HEADS, HGRN_HEAD_DIM = 8, 128
GDN_CONV = 4
IN0_ROWS = 128
LANES = 128
SUBLANES = 8
VMEM_LIMIT = 56 * 1024 * 1024


def _cparams(sem):
    return pltpu.CompilerParams(dimension_semantics=sem, vmem_limit_bytes=VMEM_LIMIT)


def _const_spec(shape):
    nd = len(shape)
    return pl.BlockSpec(shape, lambda *_: (0,) * nd, pipeline_mode=pl.Buffered(1))


def _mm(a, b):
    return jnp.dot(a.astype(BF), b.astype(BF), preferred_element_type=F32)


def _mm_nt(a, b):
    return lax.dot_general(a.astype(BF), b.astype(BF), (((1,), (1,)), ((), ())),
                           preferred_element_type=F32)


def _mm_tn(a, b):
    return lax.dot_general(a.astype(BF), b.astype(BF), (((0,), (0,)), ((), ())),
                           preferred_element_type=F32)


def _split3(x):
    hi = x.astype(BF)
    r1 = x - hi.astype(F32)
    mid = r1.astype(BF)
    lo = (r1 - mid.astype(F32)).astype(BF)
    return hi, mid, lo


def _mm_exact_lhs(m_bf, x):
    n = x.shape[1]
    if n % LANES:
        hi, mid, lo = _split3(x)
        d = lambda p: jnp.dot(m_bf, p, preferred_element_type=F32)
        return d(hi) + d(mid) + d(lo)
    y = jnp.dot(m_bf, jnp.concatenate(_split3(x), axis=1), preferred_element_type=F32)
    return y[:, :n] + y[:, n:2 * n] + y[:, 2 * n:]


def _mm_exact_rhs(x, m_bf):
    hi, mid, lo = _split3(x)
    d = lambda p: jnp.dot(p, m_bf, preferred_element_type=F32)
    return d(hi) + d(mid) + d(lo)


def _sigmoid(x):
    return 1.0 / (1.0 + jnp.exp(-x))


def _silu(x):
    return x * _sigmoid(x)


def _softplus(x):
    return jnp.maximum(x, 0.0) + jnp.log1p(jnp.exp(-jnp.abs(x)))


def _rms(x, gain):
    return x * lax.rsqrt(jnp.mean(x * x, axis=-1, keepdims=True) + NORM_EPS) * gain


def _in0_kernel(x_ref, g_ref, wb_ref, ws_ref, pm_ref, qkg_ref, cw_ref, zb_ref, zs_ref, hist_ref,
                *, tiles_per_seq):
    tm = x_ref.shape[0]
    hd = GDN_HEAD_DIM
    hrows = hist_ref.shape[1]

    @pl.when(pl.program_id(0) % tiles_per_seq == 0)
    def _():
        hist_ref[...] = jnp.zeros_like(hist_ref)

    h = _rms(x_ref[...], g_ref[...]).astype(BF)
    gw = 512
    rb = IN0_ROWS

    def project(c, r):
        return jnp.dot(h[r * rb:(r + 1) * rb], wb_ref[:, c * gw:(c + 1) * gw],
                       preferred_element_type=F32)

    def finish(c, r, z, above):
        if c < 2:
            ms = jnp.dot((z * z).astype(BF), pm_ref[...], preferred_element_type=F32)
            z = z * lax.rsqrt(ms + NORM_EPS) * qkg_ref[c:c + 1, :]
        elif 3 <= c < 6:
            j = c - 3
            w = cw_ref[:, j * gw:(j + 1) * gw]
            ext = jnp.concatenate([above, z], axis=0)
            y = z * w[GDN_CONV - 1:GDN_CONV]
            for s in range(1, GDN_CONV):
                y = y + ext[hrows - s:hrows - s + rb] * w[GDN_CONV - 1 - s:GDN_CONV - s]
            z = _silu(y)
            if j < 2:
                scale = 1.0 if j else hd ** -0.5
                z = jnp.concatenate(
                    [zh * (lax.rsqrt(jnp.sum(zh * zh, axis=-1, keepdims=True) + NORM_EPS) * scale)
                     for zh in (z[:, k * hd:(k + 1) * hd] for k in range(gw // hd))], axis=1)
        zb_ref[r * rb:(r + 1) * rb, c * gw:(c + 1) * gw] = z.astype(BF)

    pending = None
    for c in range(wb_ref.shape[1] // gw):
        conv = 3 <= c < 6
        above = hist_ref[c - 3] if conv else None
        for r in range(tm // rb):
            z = project(c, r)
            if pending is not None:
                finish(*pending)
            pending = (c, r, z, above)
            above = z[rb - hrows:, :] if conv else None
        if conv:
            hist_ref[c - 3] = above
    finish(*pending)
    zs_ref[...] = jnp.dot(h, ws_ref[...], preferred_element_type=F32)


def _in0(x2, gain, wb, ws, pm, qkg, conv_w, tm, t):
    n, d = x2.shape
    nb = wb.shape[1]
    return pl.pallas_call(
        functools.partial(_in0_kernel, tiles_per_seq=t // tm),
        grid=(n // tm,),
        in_specs=[pl.BlockSpec((tm, d), lambda i: (i, 0)),
                  _const_spec((1, d)), _const_spec(wb.shape), _const_spec(ws.shape),
                  _const_spec(pm.shape), _const_spec(qkg.shape), _const_spec(conv_w.shape)],
        out_specs=[pl.BlockSpec((tm, nb), lambda i: (i, 0)),
                   pl.BlockSpec((tm, LANES), lambda i: (i, 0))],
        out_shape=[jax.ShapeDtypeStruct((n, nb), BF), jax.ShapeDtypeStruct((n, LANES), F32)],
        scratch_shapes=[pltpu.VMEM((3, SUBLANES, 512), F32)],
        compiler_params=_cparams(("arbitrary",)),
        name="l0_in_proj",
    )(x2, gain, wb, ws, pm, qkg, conv_w)


FOX_BIAS_LANES = 8


def _fox_cum_kernel(zs_ref, bias_ref, qb_ref, kb_ref, *, blk):
    t = zs_ref.shape[0]
    r = lax.broadcasted_iota(jnp.int32, (blk, blk), 0)
    c = lax.broadcasted_iota(jnp.int32, (blk, blk), 1)
    tril = (r >= c).astype(BF)
    pr = lax.broadcasted_iota(jnp.int32, (3 * LANES, 2 * LANES), 0)
    pc = lax.broadcasted_iota(jnp.int32, (3 * LANES, 2 * LANES), 1)
    place = ((pr % LANES < FOX_HEADS)
             & (pc % LANES == (pr % LANES) * FOX_BIAS_LANES + pr // LANES + 3 * (pc // LANES))
             ).astype(BF)
    lane = lax.broadcasted_iota(jnp.int32, (1, LANES), 1)
    slot = lane % FOX_BIAS_LANES
    used = lane < FOX_HEADS * FOX_BIAS_LANES
    ones_q = jnp.where(used & (slot >= 3) & (slot < 6), 1.0, 0.0)
    ones_k = jnp.where(used & (slot < 3), 1.0, 0.0)
    blocks = [slice(i * blk, (i + 1) * blk) for i in range(t // blk)]
    local = []
    for rows in blocks:
        z = zs_ref[rows, :] + bias_ref[...]
        ls = jnp.minimum(z, 0.0) - jnp.log1p(jnp.exp(-jnp.abs(z)))
        local.append(_mm_exact_lhs(tril, ls))
    carry = jnp.zeros((1, LANES), F32)
    for rows, loc in zip(blocks, local):
        cb = loc + carry
        carry = cb[blk - 1:blk, :]
        feat = jnp.dot(jnp.concatenate(_split3(cb * LOG2E), axis=1), place,
                       preferred_element_type=F32)
        qb_ref[rows, :] = (feat[:, :LANES] + ones_q).astype(BF)
        kb_ref[rows, :] = (ones_k - feat[:, LANES:]).astype(BF)


def _fox_cum(zs, bias_row, b, t):
    n = zs.shape[0]
    spec = pl.BlockSpec((t, LANES), lambda bi: (bi, 0))
    return pl.pallas_call(
        functools.partial(_fox_cum_kernel, blk=256),
        grid=(b,),
        in_specs=[spec, _const_spec(bias_row.shape)],
        out_specs=[spec, spec],
        out_shape=[jax.ShapeDtypeStruct((n, LANES), BF)] * 2,
        compiler_params=_cparams(("parallel",)),
        name="fox_cum_gate",
    )(zs, bias_row)


def _fox_kernel(q_ref, k_ref, v_ref, qb_ref, kb_ref, o_ref, *, tq):
    pair = pl.program_id(1)
    t = q_ref.shape[0]
    lane = lax.broadcasted_iota(jnp.int32, (1, LANES), 1)
    zero = jnp.zeros((tq, LANES), BF)
    first = lane < FOX_HEAD_DIM
    bias_head = lane // FOX_BIAS_LANES
    ones = jnp.ones((tq, LANES), BF)
    on_diag = (lax.broadcasted_iota(jnp.int32, (2 * tq, tq), 1)
               <= lax.broadcasted_iota(jnp.int32, (2 * tq, tq), 0) % tq)

    def scores(i):
        rows_i = slice(i * tq, (i + 1) * tq)
        kv = (i + 1) * tq
        q2, qb = q_ref[rows_i, :], qb_ref[rows_i, :]
        rows = []
        for h in range(2):
            qh = jnp.where(first if h == 0 else jnp.logical_not(first), q2, zero)
            qbh = jnp.where(bias_head == 2 * pair + h, qb, zero)
            rows.append(jnp.concatenate([qh, qbh], axis=1))
        ka = jnp.concatenate([k_ref[:kv, :], kb_ref[:kv, :]], axis=1)
        s = lax.dot_general(jnp.concatenate(rows, axis=0), ka, (((1,), (1,)), ((), ())),
                            preferred_element_type=F32)
        diag = jnp.where(on_diag, s[:, kv - tq:], -jnp.inf)
        return diag if i == 0 else jnp.concatenate([s[:, :kv - tq], diag], axis=1)

    def finish(i, s):
        kv = (i + 1) * tq
        p = jnp.exp2(s - jnp.max(s, axis=-1, keepdims=True)).astype(BF)
        va = jnp.concatenate([v_ref[:kv, :], jnp.concatenate([ones] * (i + 1), axis=0)], axis=1)
        acc = jnp.dot(p, va, preferred_element_type=F32)
        o = acc[:, :LANES] / acc[:, LANES:]
        o_ref[i * tq:(i + 1) * tq, :] = jnp.where(first, o[:tq], o[tq:]).astype(o_ref.dtype)

    s = scores(0)
    for i in range(t // tq):
        s_next = scores(i + 1) if (i + 1) * tq < t else None
        finish(i, s)
        s = s_next


def _fox_attention(zb, qbias, kbias, b, t, tq):
    n = zb.shape[0]
    hp = FOX_HEADS // 2
    return pl.pallas_call(
        functools.partial(_fox_kernel, tq=tq),
        grid=(b, hp),
        in_specs=[pl.BlockSpec((t, LANES), lambda bi, p: (bi, p)),
                  pl.BlockSpec((t, LANES), lambda bi, p: (bi, hp + p)),
                  pl.BlockSpec((t, LANES), lambda bi, p: (bi, 2 * hp + p)),
                  pl.BlockSpec((t, LANES), lambda bi, p: (bi, 0)),
                  pl.BlockSpec((t, LANES), lambda bi, p: (bi, 0))],
        out_specs=pl.BlockSpec((t, LANES), lambda bi, p: (bi, p)),
        out_shape=jax.ShapeDtypeStruct((n, FOX_HEADS * FOX_HEAD_DIM), BF),
        compiler_params=_cparams(("parallel", "parallel")),
        name="fox_attention",
    )(zb, zb, zb, qbias, kbias)


def _neumann_inverse(mats, eye):
    ps = [eye - a for a in mats]
    aps = list(mats)
    k = 2
    while k < CHUNK:
        aps = [_mm(ap, ap) for ap in aps]
        yield
        ps = [p + _mm(p, ap) for p, ap in zip(ps, aps)]
        yield
        k *= 2
    return ps


def _gdn_kernel(alog_ref, dtb_ref, q_ref, k_ref, v_ref, gg_ref, gar_ref, gac_ref, gbc_ref,
                on_ref, o_ref, lhs_s, ku_s, u_s, qk_s, pend_s, grow_s, gcol_s, bcol_s, s_s):
    t = q_ref.shape[0]
    nc = t // CHUNK
    hd = GDN_HEAD_DIM
    lrows = hd + 2 * CHUNK

    ri = lax.broadcasted_iota(jnp.int32, (CHUNK, CHUNK), 0)
    ci = lax.broadcasted_iota(jnp.int32, (CHUNK, CHUNK), 1)
    lower = ri >= ci
    strict = ri > ci
    tril = lower.astype(BF)
    triu = (ri <= ci).astype(BF)
    eye = (ri == ci).astype(F32)
    for h in range(GDN_HEADS):
        a = -jnp.exp(alog_ref[h])
        grow_s[h] = _mm_exact_rhs(a * _softplus(gar_ref[0, h] + dtb_ref[h]), triu)
        gcol_s[h] = _mm_exact_lhs(tril, a * _softplus(gac_ref[0, h] + dtb_ref[h]))
        bcol_s[h] = _sigmoid(gbc_ref[0, h])
    s_s[...] = jnp.zeros_like(s_s)

    lane_nc = lax.broadcasted_iota(jnp.int32, (CHUNK, nc), 1)
    heads = range(GDN_HEADS)
    cat = lambda parts: jnp.concatenate(parts, axis=1)

    per_iter = 4

    hsl = lambda h: slice(h * hd, (h + 1) * hd)

    def chunk_rows(n, size):
        return pl.ds(n * size if isinstance(n, int) else pl.multiple_of(n * size, size), size)

    def local_work(it):
        rows = chunk_rows(it, per_iter * CHUNK)
        chains = [(c, h) for c in range(per_iter) for h in heads]
        csl = lambda c: slice(c * CHUNK, (c + 1) * CHUNK)
        q_all, k_all, v_all = (r[rows, :].astype(F32) for r in (q_ref, k_ref, v_ref))
        q = [q_all[csl(c), hsl(h)] for c, h in chains]
        k = [k_all[csl(c), hsl(h)] for c, h in chains]
        v = [v_all[csl(c), hsl(h)] for c, h in chains]
        sel = [lane_nc == it * per_iter + c for c in range(per_iter)]
        g_c = [jnp.sum(jnp.where(sel[c], gcol_s[h], 0.0), axis=1, keepdims=True)
               for c, h in chains]
        b_c = [jnp.sum(jnp.where(sel[c], bcol_s[h], 0.0), axis=1, keepdims=True)
               for c, h in chains]
        g_r = [grow_s[h, pl.ds(it * per_iter + c, 1), :] for c, h in chains]
        yield
        decay = [jnp.exp(jnp.where(lower, gc - gr, -jnp.inf)) for gc, gr in zip(g_c, g_r)]
        kb = [x * bc for x, bc in zip(k, b_c)]
        kk = [_mm_nt(x, y) for x, y in zip(kb, k)]
        yield
        qk = [_mm_nt(x, y) for x, y in zip(q, k)]
        yield
        t_inv = yield from _neumann_inverse(
            [jnp.where(strict, x * d, 0.0) for x, d in zip(kk, decay)], eye)
        eg = [jnp.exp(gc) for gc in g_c]
        uw = [_mm(ti, jnp.concatenate([x * bc, y * e], axis=1))
              for ti, x, bc, y, e in zip(t_inv, v, b_c, kb, eg)]
        yield
        kd = [x * jnp.exp(gr[:, CHUNK - 1:CHUNK] - gc) for x, gr, gc in zip(k, g_r, g_c)]
        kuw = [_mm_tn(x, y) for x, y in zip(kd, uw)]
        yield
        grid = lambda parts: jnp.concatenate(
            [cat(parts[c * GDN_HEADS:(c + 1) * GDN_HEADS]) for c in range(per_iter)], axis=0)
        lhs = [jnp.concatenate([m[:, hd:], x[:, hd:], y * e], axis=0)
               for m, x, y, e in zip(kuw, uw, q, eg)]
        lhs_s[chunk_rows(it, per_iter * lrows), :] = grid(lhs).astype(BF)
        ku_s[chunk_rows(it, per_iter * hd), :] = grid([m[:, :hd] for m in kuw])
        u_s[rows, :] = grid([x[:, :hd] for x in uw])
        qk_s[rows, :] = grid([x * d for x, d in zip(qk, decay)]).astype(BF)

    def advance(n):
        s_prev = [s_s[h] for h in heads]
        prod = [_mm(lhs_s[chunk_rows(n, lrows), hsl(h)], s_prev[h]) for h in heads]
        g_last = [grow_s[h, pl.ds(n, 1), :][:, CHUNK - 1:CHUNK] for h in heads]
        s_s[...] = jnp.stack([s_prev[h] * jnp.exp(g_last[h]) - prod[h][:hd]
                              + ku_s[chunk_rows(n, hd), hsl(h)] for h in heads])
        pend_s[...] = cat([prod[h][hd:] for h in heads])

    def emit(n):
        rows = chunk_rows(n, CHUNK)
        pend = pend_s[...]
        v_new = [u_s[rows, hsl(h)] - pend[:CHUNK, hsl(h)] for h in heads]
        qv = [_mm(qk_s[rows, h * CHUNK:(h + 1) * CHUNK], v_new[h]) for h in heads]
        outs = [_rms(pend[CHUNK:, hsl(h)] + qv[h], on_ref[...]) for h in heads]
        gate = _silu(gg_ref[rows, :].astype(F32))
        o_ref[rows, :] = (cat(outs) * gate).astype(o_ref.dtype)

    def state_work(first, has_previous):
        for c in range(per_iter):
            if has_previous or c:
                emit(first + c - 1)
                yield
            advance(first + c)
            yield

    def interleave(*gens):
        active = list(enumerate(gens))
        while active:
            for item in list(active):
                try:
                    for _ in range(2 if item[0] else 1):
                        next(item[1])
                except StopIteration:
                    active.remove(item)

    n_iter = nc // per_iter
    interleave(local_work(0))
    if n_iter > 1:
        interleave(local_work(1), state_work(0, False))

    def merged_step(it, carry):
        interleave(local_work(it), state_work((it - 1) * per_iter, True))
        return carry

    lax.fori_loop(2, n_iter, merged_step, 0)
    interleave(state_work((n_iter - 1) * per_iter, n_iter > 1))
    emit(nc - 1)


def _gdn(zb, ga_row, ga_col, gb_col, a_log, dt_bias, o_norm, b, t):
    n = zb.shape[0]
    nc = t // CHUNK
    width = GDN_HEADS * GDN_HEAD_DIM
    fox_blocks = 3 * FOX_HEADS * FOX_HEAD_DIM // width
    colspec = lambda j: pl.BlockSpec((t, width), lambda bi: (bi, fox_blocks + j))
    gspec = lambda shp: pl.BlockSpec((1,) + shp, lambda bi: (bi, 0, 0, 0))
    return pl.pallas_call(
        _gdn_kernel,
        grid=(b,),
        in_specs=[_const_spec(a_log.shape), _const_spec(dt_bias.shape),
                  colspec(0), colspec(1), colspec(2), colspec(3),
                  gspec((GDN_HEADS, nc, CHUNK)), gspec((GDN_HEADS, CHUNK, nc)),
                  gspec((GDN_HEADS, CHUNK, nc)), _const_spec(o_norm.shape)],
        out_specs=pl.BlockSpec((t, width), lambda bi: (bi, 0)),
        out_shape=jax.ShapeDtypeStruct((n, width), BF),
        scratch_shapes=[pltpu.VMEM((nc * (GDN_HEAD_DIM + 2 * CHUNK), width), BF),
                        pltpu.VMEM((nc * GDN_HEAD_DIM, width), F32),
                        pltpu.VMEM((t, width), F32),
                        pltpu.VMEM((t, GDN_HEADS * CHUNK), BF),
                        pltpu.VMEM((2 * CHUNK, width), F32),
                        pltpu.VMEM((GDN_HEADS, nc, CHUNK), F32),
                        pltpu.VMEM((GDN_HEADS, CHUNK, nc), F32),
                        pltpu.VMEM((GDN_HEADS, CHUNK, nc), F32),
                        pltpu.VMEM((GDN_HEADS, GDN_HEAD_DIM, GDN_HEAD_DIM), F32)],
        compiler_params=_cparams(("parallel",)),
        name="gated_deltanet",
    )(a_log, dt_bias, zb, zb, zb, zb, ga_row, ga_col, gb_col, o_norm)


def _mlp_kernel(a1_ref, a2_ref, x_ref, wo1_ref, wo2_ref, g_ref, w1_ref, w2_ref, o_ref, acc_ref,
                *, fc):
    mix = (jnp.dot(a1_ref[...], wo1_ref[...], preferred_element_type=F32)
           + jnp.dot(a2_ref[...], wo2_ref[...], preferred_element_type=F32))
    x1 = x_ref[...] + mix
    h = _rms(x1, g_ref[...]).astype(BF)
    acc_ref[...] = x1
    for c in range(w1_ref.shape[1] // fc):
        a = jnp.maximum(jnp.dot(h, w1_ref[:, c * fc:(c + 1) * fc],
                                preferred_element_type=F32), 0.0)
        acc_ref[...] += jnp.dot((a * a).astype(BF), w2_ref[c * fc:(c + 1) * fc, :],
                                preferred_element_type=F32)
    o_ref[...] = acc_ref[...]


def _mlp(a1, a2, a2_block, x2, wo1, wo2, gain, w1, w2, tm):
    n, d = x2.shape
    half = wo1.shape[0]
    return pl.pallas_call(
        functools.partial(_mlp_kernel, fc=1024),
        grid=(n // tm,),
        in_specs=[pl.BlockSpec((tm, half), lambda i: (i, 0)),
                  pl.BlockSpec((tm, half), lambda i: (i, a2_block)),
                  pl.BlockSpec((tm, d), lambda i: (i, 0)),
                  _const_spec(wo1.shape), _const_spec(wo2.shape), _const_spec(gain.shape),
                  _const_spec(w1.shape), _const_spec(w2.shape)],
        out_specs=pl.BlockSpec((tm, d), lambda i: (i, 0)),
        out_shape=jax.ShapeDtypeStruct((n, d), F32),
        scratch_shapes=[pltpu.VMEM((tm, d), F32)],
        compiler_params=_cparams(("parallel",)),
        name="out_proj_mlp",
    )(a1, a2, x2, wo1, wo2, gain, w1, w2)


def _in1_kernel(x_ref, g_ref, wb_ref, wf_ref, zb_ref, zf_ref):
    h = _rms(x_ref[...], g_ref[...]).astype(BF)
    gw = 512
    for c in range(wb_ref.shape[1] // gw):
        zb_ref[:, c * gw:(c + 1) * gw] = jnp.dot(
            h, wb_ref[:, c * gw:(c + 1) * gw], preferred_element_type=F32).astype(BF)
    for c in range(wf_ref.shape[1] // gw):
        zf_ref[:, c * gw:(c + 1) * gw] = jnp.dot(
            h, wf_ref[:, c * gw:(c + 1) * gw], preferred_element_type=F32)


def _in1(x2, gain, wb, wf, tm):
    n, d = x2.shape
    return pl.pallas_call(
        _in1_kernel,
        grid=(n // tm,),
        in_specs=[pl.BlockSpec((tm, d), lambda i: (i, 0)),
                  _const_spec(gain.shape), _const_spec(wb.shape), _const_spec(wf.shape)],
        out_specs=[pl.BlockSpec((tm, wb.shape[1]), lambda i: (i, 0)),
                   pl.BlockSpec((tm, wf.shape[1]), lambda i: (i, 0))],
        out_shape=[jax.ShapeDtypeStruct((n, wb.shape[1]), BF),
                   jax.ShapeDtypeStruct((n, wf.shape[1]), F32)],
        compiler_params=_cparams(("parallel",)),
        name="l1_in_proj",
    )(x2, gain, wb, wf)


HG_GROUP = 4
HG_HEADS_PER_STEP = 1


HG_LEVELS = (32, 16, 8, 4, 2, 1)


def _hgrn_term_mask(terms):
    t = lax.broadcasted_iota(jnp.int32, (CHUNK, len(terms) * CHUNK), 0)
    col = lax.broadcasted_iota(jnp.int32, (CHUNK, len(terms) * CHUNK), 1)
    s = col % CHUNK
    mask = None
    for j, m in enumerate(terms):
        own = (t == s) if m == 0 else (
            (t % (2 * m) >= m) & (s % (2 * m) < m) & (t // (2 * m) == s // (2 * m)))
        own = own & (col // CHUNK == j)
        mask = own if mask is None else mask | own
    return mask


def _hgrn_level_operands(q, k, b, ref_row):
    d = q.shape[1]
    nb = CHUNK // SUBLANES
    blk = lambda x, i: x[SUBLANES * i:SUBLANES * (i + 1)]
    sub = lax.broadcasted_iota(jnp.int32, (SUBLANES, d), 0)
    xs = []
    for m in HG_LEVELS:
        parts = []
        for i in range(nb):
            bi, qi, ki = blk(b, i), blk(q, i), blk(k, i)
            row0 = SUBLANES * i
            if m >= SUBLANES:
                ref = row0 // (2 * m) * (2 * m) + m
                if row0 % (2 * m) >= m:
                    parts.append(qi * jnp.exp2(bi - ref_row(ref)))
                else:
                    parts.append(ki * jnp.exp2(ref_row(ref) - bi))
                continue
            upper = sub % (2 * m) >= m
            if m == 4:
                bref = ref_row(row0 + 4)
            elif m == 2:
                bref = jnp.where(sub < 4, ref_row(row0 + 2), ref_row(row0 + 6))
            else:
                bref = jnp.where(upper, bi, pltpu.roll(bi, SUBLANES - 1, 0))
            dlt = bi - bref
            parts.append(jnp.where(upper, qi, ki) * jnp.exp2(jnp.where(upper, dlt, -dlt)))
        xs.append(jnp.concatenate(parts, axis=0).astype(BF))
        yield
    return xs


def _hgrn_kernel(zq_ref, zi_ref, zg_ref, zf_ref, lbl_ref, on_ref, o_ref,
                 oi_s, qd_s, kv_s, dec_s, h1_s, *, group):
    t = zq_ref.shape[0]
    d = HGRN_HEAD_DIM
    heads = range(zq_ref.shape[1] // d)
    hs = lambda hh: slice(hh * d, (hh + 1) * d)
    gr = group * CHUNK
    ng = t // gr
    ri = lax.broadcasted_iota(jnp.int32, (gr, gr), 0)
    ci = lax.broadcasted_iota(jnp.int32, (gr, gr), 1)
    tril = ((ri >= ci) & (ri // CHUNK == ci // CHUNK)).astype(BF)
    term_mask = {m: _hgrn_term_mask((m,)) for m in (0,) + HG_LEVELS}
    sls =[slice(c * CHUNK, (c + 1) * CHUNK) for c in range(group)]

    l0 = lbl_ref[0:1, :]
    l1 = lbl_ref[1:2, :]
    mx = jnp.maximum(l0, l1)
    e0 = jnp.exp(l0 - mx)
    e1 = jnp.exp(l1 - mx)
    s0 = e0 / (e0 + e1)
    s1 = e1 / (e0 + e1)
    lb = (s0 + s1) - s0

    def group_rows(g):
        return pl.ds(g * gr, gr)

    def prelude(hh, g):
        rows = group_rows(g)
        qs, ks, ls = [], [], []
        for sl in sls:
            qs.append(_silu(zq_ref[rows, hs(hh)][sl].astype(F32)))
            f = lb[:, hs(hh)] + (1.0 - lb[:, hs(hh)]) * _sigmoid(zf_ref[rows, hs(hh)][sl])
            ks.append(1.0 - f)
            ls.append(jnp.log2(f))
            yield
        q, k = jnp.concatenate(qs, axis=0), jnp.concatenate(ks, axis=0)
        b = _mm_exact_lhs(tril, jnp.concatenate(ls, axis=0))
        h1_s[hh, g, 0] = b
        h1_s[hh, g, 1] = k
        h1_s[hh, g, 2] = q
        qd_s[hh, rows, :] = (q * jnp.exp2(b)).astype(BF)
        yield

    def operands(hh, g):
        b, k, q = h1_s[hh, g, 0], h1_s[hh, g, 1], h1_s[hh, g, 2]
        ops = []
        for c, sl in enumerate(sls):
            ref_row = lambda r, c=c: h1_s[hh, g, 0, c * CHUNK + r:c * CHUNK + r + 1, :]
            xs = yield from _hgrn_level_operands(q[sl], k[sl], b[sl], ref_row)
            b_last = b[sl][CHUNK - 1:CHUNK, :]
            kdec = (k[sl] * jnp.exp2(b_last - b[sl])).astype(BF)
            dec_s[hh, g * group + c] = jnp.exp2(b_last)
            ops.append((xs, q[sl].astype(BF), k[sl].astype(BF), kdec))
            yield
        return ops

    def products(hh, g, ops):
        rows = group_rows(g)
        v = zi_ref[rows, hs(hh)]
        dmat, kv, oi = [], [], []
        for (xs, qb, kb, kdec), sl in zip(ops, sls):
            dm = jnp.where(term_mask[0], _mm_nt(qb, kb), 0.0)
            yield
            for m, x in zip(HG_LEVELS, xs):
                dm = jnp.where(term_mask[m], _mm_nt(x, x), dm)
                yield
            kv.append(_mm_tn(v[sl], kdec))
            dmat.append(dm)
            yield
        for dm, sl in zip(dmat, sls):
            oi.append(_mm(dm, v[sl]))
            yield
        oi_s[hh, rows, :] = jnp.concatenate(oi, axis=0)
        kv_s[hh, pl.ds(g * group, group)] = jnp.stack(kv)

    def advance(*stages):
        results = [None] * len(stages)
        active = list(enumerate(stages))
        while active:
            for item in list(active):
                j, (gen, turns) = item
                try:
                    for _ in range(turns):
                        next(gen)
                except StopIteration as done:
                    results[j] = done.value
                    active.remove(item)
        return results

    for hh in heads:
        advance((prelude(hh, 0), 1), (prelude(hh, 1), 1))
        ops = advance((operands(hh, 0), 1))[0]
        for g in range(1, ng):
            stages = [(products(hh, g - 1, ops), 1), (operands(hh, g), 1)]
            if g + 1 < ng:
                stages.append((prelude(hh, g + 1), 1))
            ops = advance(*stages)[1]
        advance((products(hh, ng - 1, ops), 1))

    per_step = 4 * group
    sr = per_step * CHUNK

    def state_step(g, sts):
        r0 = pl.multiple_of(g * sr, sr)
        rows = pl.ds(r0, sr)
        new_sts, normed = [], []
        for hh, st in zip(heads, sts):
            states = []
            for c in range(per_step):
                n = g * per_step + c
                states.append(st)
                st = st * dec_s[hh, n] + kv_s[hh, n]
            new_sts.append(st)
            outs = [oi_s[hh, pl.ds(r0 + c * CHUNK, CHUNK), :]
                    + _mm_nt(qd_s[hh, pl.ds(r0 + c * CHUNK, CHUNK), :], states[c])
                    for c in range(per_step)]
            normed.append(_rms(jnp.concatenate(outs, axis=0), on_ref[...]))
        o = jnp.concatenate(normed, axis=1)
        o_ref[rows, :] = (o * _silu(zg_ref[rows, :].astype(F32))).astype(o_ref.dtype)
        return tuple(new_sts)

    lax.fori_loop(0, t // sr, state_step, tuple(jnp.zeros((d, d), F32) for _ in heads))


def _hgrn(zb, zf, lb_logits, o_norm, b, t):
    n = zb.shape[0]
    hd = HGRN_HEAD_DIM
    nh = HGRN_HEADS
    nc = t // CHUNK
    gr = HG_GROUP * CHUNK
    hps = HG_HEADS_PER_STEP
    ns = nh // hps
    wide = hps * hd
    return pl.pallas_call(
        functools.partial(_hgrn_kernel, group=HG_GROUP),
        grid=(b, ns),
        in_specs=[pl.BlockSpec((t, wide), lambda bi, h: (bi, h)),
                  pl.BlockSpec((t, wide), lambda bi, h: (bi, ns + h)),
                  pl.BlockSpec((t, wide), lambda bi, h: (bi, 2 * ns + h)),
                  pl.BlockSpec((t, wide), lambda bi, h: (bi, h)),
                  pl.BlockSpec((lb_logits.shape[0], wide), lambda bi, h: (0, h)),
                  _const_spec(o_norm.shape)],
        out_specs=pl.BlockSpec((t, wide), lambda bi, h: (bi, h)),
        out_shape=jax.ShapeDtypeStruct((n, nh * hd), BF),
        scratch_shapes=[pltpu.VMEM((hps, t, hd), F32), pltpu.VMEM((hps, t, hd), BF),
                        pltpu.VMEM((hps, nc, hd, hd), F32), pltpu.VMEM((hps, nc, 1, hd), F32),
                        pltpu.VMEM((hps, t // gr, 3, gr, hd), F32)],
        compiler_params=_cparams(("parallel", "parallel")),
        name="hgrn2",
    )(zb, zb, zb, zf, lb_logits, o_norm)


def kernel(x, l0_mix_norm, l0_w_in, l0_fox_q_norm, l0_fox_k_norm, l0_fox_f_bias, l0_gdn_conv,
           l0_gdn_A_log, l0_gdn_dt_bias, l0_gdn_o_norm, l0_w_out, l0_ffn_norm, l0_w_ff1,
           l0_w_ff2, l1_mix_norm, l1_w_in, l1_hgrn_o_norm, l1_w_out, l1_ffn_norm, l1_w_ff1,
           l1_w_ff2, hgrn_lb_logits):
    b, t, d = x.shape
    n = b * t
    fw = FOX_HEADS * FOX_HEAD_DIM
    gw = GDN_HEADS * GDN_HEAD_DIM
    assert hgrn_lb_logits.shape[0] == 2 and d == HGRN_HEADS * HGRN_HEAD_DIM
    assert t % (2 * HG_GROUP * CHUNK) == 0
    tm = 512 if t % 512 == 0 else 256
    tq = 256
    nc = t // CHUNK
    row = lambda p: p.reshape(1, -1).astype(F32)
    x2 = x.reshape(n, d)

    o_ff = 3 * fw
    o_gq = o_ff + FOX_HEADS
    o_gb = o_gq + 3 * gw
    o_gg = o_gb + 2 * GDN_HEADS
    wb0 = jnp.concatenate([l0_w_in[:, :o_ff], l0_w_in[:, o_gq:o_gb], l0_w_in[:, o_gg:]],
                          axis=1).astype(BF)
    n_small = FOX_HEADS + 2 * GDN_HEADS
    ws0 = jnp.concatenate([l0_w_in[:, o_ff:o_gq], l0_w_in[:, o_gb:o_gg],
                           jnp.zeros((d, LANES - n_small), F32)], axis=1).astype(BF)
    hid = jnp.arange(fw) // FOX_HEAD_DIM
    pm = jnp.where(hid[:, None] == hid[None, :], 1.0 / FOX_HEAD_DIM, 0.0).astype(BF)
    qkg = jnp.stack([jnp.tile(l0_fox_q_norm, FOX_HEADS) * (FOX_HEAD_DIM ** -0.5 * LOG2E),
                     jnp.tile(l0_fox_k_norm, FOX_HEADS)]).astype(F32)

    zb0, zs0 = _in0(x2, row(l0_mix_norm), wb0, ws0, pm, qkg, l0_gdn_conv.astype(F32), tm, t)

    f_bias = jnp.pad(l0_fox_f_bias.astype(F32), (0, LANES - FOX_HEADS)).reshape(1, LANES)
    qbias, kbias = _fox_cum(zs0, f_bias, b, t)
    fox_o = _fox_attention(zb0, qbias, kbias, b, t, tq)

    def gate_rows(lo):
        g = zs0[:, lo:lo + GDN_HEADS].reshape(b, t, GDN_HEADS).transpose(0, 2, 1)
        return g.reshape(b, GDN_HEADS, nc, CHUNK)
    gb_row = gate_rows(FOX_HEADS)
    ga_row = gate_rows(FOX_HEADS + GDN_HEADS)
    gdn_o = _gdn(zb0, ga_row, ga_row.swapaxes(-1, -2), gb_row.swapaxes(-1, -2),
                 l0_gdn_A_log.reshape(GDN_HEADS, 1, 1).astype(F32),
                 l0_gdn_dt_bias.reshape(GDN_HEADS, 1, 1).astype(F32), row(l0_gdn_o_norm), b, t)

    wo0 = l0_w_out.astype(BF)
    x1 = _mlp(fox_o, gdn_o, 0, x2, wo0[:fw], wo0[fw:], row(l0_ffn_norm),
              l0_w_ff1.astype(BF), l0_w_ff2.astype(BF), tm)

    wb1 = jnp.concatenate([l1_w_in[:, :d], l1_w_in[:, 2 * d:]], axis=1).astype(BF)
    wf1 = l1_w_in[:, d:2 * d].astype(BF)
    zb1, zf1 = _in1(x1, row(l1_mix_norm), wb1, wf1, tm)
    ho = _hgrn(zb1, zf1, hgrn_lb_logits.astype(F32), row(l1_hgrn_o_norm), b, t)
    wo1 = l1_w_out.astype(BF)
    half = d // 2
    x3 = _mlp(ho, ho, 1, x1, wo1[:half], wo1[half:], row(l1_ffn_norm),
              l1_w_ff1.astype(BF), l1_w_ff2.astype(BF), tm)
    return x3.reshape(b, t, d)
```

```python
import functools

import jax
import jax.numpy as jnp
from jax import lax
from jax.experimental import pallas as pl
from jax.experimental.pallas import tpu as pltpu

BF = jnp.bfloat16
F32 = jnp.float32
NORM_EPS = 1e-6
LOG2E = 1.4426950408889634
CHUNK = 64
FOX_HEADS, FOX_HEAD_DIM = 8, 64
GDN_HEADS, GDN_HEAD_DIM = 4, 128
HGRN_HEADS, HGRN_HEAD_DIM = 8, 128
GDN_CONV = 4
IN0_ROWS = 128
LANES = 128
SUBLANES = 8
VMEM_LIMIT = 56 * 1024 * 1024


def _cparams(sem):
    return pltpu.CompilerParams(dimension_semantics=sem, vmem_limit_bytes=VMEM_LIMIT)


def _const_spec(shape):
    nd = len(shape)
    return pl.BlockSpec(shape, lambda *_: (0,) * nd, pipeline_mode=pl.Buffered(1))


def _mm(a, b):
    return jnp.dot(a.astype(BF), b.astype(BF), preferred_element_type=F32)


def _mm_nt(a, b):
    return lax.dot_general(a.astype(BF), b.astype(BF), (((1,), (1,)), ((), ())),
                           preferred_element_type=F32)


def _mm_tn(a, b):
    return lax.dot_general(a.astype(BF), b.astype(BF), (((0,), (0,)), ((), ())),
                           preferred_element_type=F32)


def _split3(x):
    hi = x.astype(BF)
    r1 = x - hi.astype(F32)
    mid = r1.astype(BF)
    lo = (r1 - mid.astype(F32)).astype(BF)
    return hi, mid, lo


def _mm_exact_lhs(m_bf, x):
    n = x.shape[1]
    if n % LANES:
        hi, mid, lo = _split3(x)
        d = lambda p: jnp.dot(m_bf, p, preferred_element_type=F32)
        return d(hi) + d(mid) + d(lo)
    y = jnp.dot(m_bf, jnp.concatenate(_split3(x), axis=1), preferred_element_type=F32)
    return y[:, :n] + y[:, n:2 * n] + y[:, 2 * n:]


def _mm_exact_rhs(x, m_bf):
    hi, mid, lo = _split3(x)
    d = lambda p: jnp.dot(p, m_bf, preferred_element_type=F32)
    return d(hi) + d(mid) + d(lo)


def _sigmoid(x):
    return 1.0 / (1.0 + jnp.exp(-x))


def _silu(x):
    return x * _sigmoid(x)


def _softplus(x):
    return jnp.maximum(x, 0.0) + jnp.log1p(jnp.exp(-jnp.abs(x)))


def _rms(x, gain):
    return x * lax.rsqrt(jnp.mean(x * x, axis=-1, keepdims=True) + NORM_EPS) * gain


def _in0_kernel(x_ref, g_ref, wb_ref, ws_ref, pm_ref, qkg_ref, cw_ref, zb_ref, zs_ref, hist_ref,
                *, tiles_per_seq):
    tm = x_ref.shape[0]
    hd = GDN_HEAD_DIM
    hrows = hist_ref.shape[1]

    @pl.when(pl.program_id(0) % tiles_per_seq == 0)
    def _():
        hist_ref[...] = jnp.zeros_like(hist_ref)

    h = _rms(x_ref[...], g_ref[...]).astype(BF)
    gw = 512
    rb = IN0_ROWS

    def project(c, r):
        return jnp.dot(h[r * rb:(r + 1) * rb], wb_ref[:, c * gw:(c + 1) * gw],
                       preferred_element_type=F32)

    def finish(c, r, z, above):
        if c < 2:
            ms = jnp.dot((z * z).astype(BF), pm_ref[...], preferred_element_type=F32)
            z = z * lax.rsqrt(ms + NORM_EPS) * qkg_ref[c:c + 1, :]
        elif 3 <= c < 6:
            j = c - 3
            w = cw_ref[:, j * gw:(j + 1) * gw]
            ext = jnp.concatenate([above, z], axis=0)
            y = z * w[GDN_CONV - 1:GDN_CONV]
            for s in range(1, GDN_CONV):
                y = y + ext[hrows - s:hrows - s + rb] * w[GDN_CONV - 1 - s:GDN_CONV - s]
            z = _silu(y)
            if j < 2:
                scale = 1.0 if j else hd ** -0.5
                z = jnp.concatenate(
                    [zh * (lax.rsqrt(jnp.sum(zh * zh, axis=-1, keepdims=True) + NORM_EPS) * scale)
                     for zh in (z[:, k * hd:(k + 1) * hd] for k in range(gw // hd))], axis=1)
        zb_ref[r * rb:(r + 1) * rb, c * gw:(c + 1) * gw] = z.astype(BF)

    pending = None
    for c in range(wb_ref.shape[1] // gw):
        conv = 3 <= c < 6
        above = hist_ref[c - 3] if conv else None
        for r in range(tm // rb):
            z = project(c, r)
            if pending is not None:
                finish(*pending)
            pending = (c, r, z, above)
            above = z[rb - hrows:, :] if conv else None
        if conv:
            hist_ref[c - 3] = above
    finish(*pending)
    zs_ref[...] = jnp.dot(h, ws_ref[...], preferred_element_type=F32)


def _in0(x2, gain, wb, ws, pm, qkg, conv_w, tm, t):
    n, d = x2.shape
    nb = wb.shape[1]
    return pl.pallas_call(
        functools.partial(_in0_kernel, tiles_per_seq=t // tm),
        grid=(n // tm,),
        in_specs=[pl.BlockSpec((tm, d), lambda i: (i, 0)),
                  _const_spec((1, d)), _const_spec(wb.shape), _const_spec(ws.shape),
                  _const_spec(pm.shape), _const_spec(qkg.shape), _const_spec(conv_w.shape)],
        out_specs=[pl.BlockSpec((tm, nb), lambda i: (i, 0)),
                   pl.BlockSpec((tm, LANES), lambda i: (i, 0))],
        out_shape=[jax.ShapeDtypeStruct((n, nb), BF), jax.ShapeDtypeStruct((n, LANES), F32)],
        scratch_shapes=[pltpu.VMEM((3, SUBLANES, 512), F32)],
        compiler_params=_cparams(("arbitrary",)),
        name="l0_in_proj",
    )(x2, gain, wb, ws, pm, qkg, conv_w)


FOX_BIAS_LANES = 8


def _fox_cum_kernel(zs_ref, bias_ref, qb_ref, kb_ref, *, blk):
    t = zs_ref.shape[0]
    r = lax.broadcasted_iota(jnp.int32, (blk, blk), 0)
    c = lax.broadcasted_iota(jnp.int32, (blk, blk), 1)
    tril = (r >= c).astype(BF)
    pr = lax.broadcasted_iota(jnp.int32, (3 * LANES, 2 * LANES), 0)
    pc = lax.broadcasted_iota(jnp.int32, (3 * LANES, 2 * LANES), 1)
    place = ((pr % LANES < FOX_HEADS)
             & (pc % LANES == (pr % LANES) * FOX_BIAS_LANES + pr // LANES + 3 * (pc // LANES))
             ).astype(BF)
    lane = lax.broadcasted_iota(jnp.int32, (1, LANES), 1)
    slot = lane % FOX_BIAS_LANES
    used = lane < FOX_HEADS * FOX_BIAS_LANES
    ones_q = jnp.where(used & (slot >= 3) & (slot < 6), 1.0, 0.0)
    ones_k = jnp.where(used & (slot < 3), 1.0, 0.0)
    blocks = [slice(i * blk, (i + 1) * blk) for i in range(t // blk)]
    local = []
    for rows in blocks:
        z = zs_ref[rows, :] + bias_ref[...]
        ls = jnp.minimum(z, 0.0) - jnp.log1p(jnp.exp(-jnp.abs(z)))
        local.append(_mm_exact_lhs(tril, ls))
    carry = jnp.zeros((1, LANES), F32)
    for rows, loc in zip(blocks, local):
        cb = loc + carry
        carry = cb[blk - 1:blk, :]
        feat = jnp.dot(jnp.concatenate(_split3(cb * LOG2E), axis=1), place,
                       preferred_element_type=F32)
        qb_ref[rows, :] = (feat[:, :LANES] + ones_q).astype(BF)
        kb_ref[rows, :] = (ones_k - feat[:, LANES:]).astype(BF)


def _fox_cum(zs, bias_row, b, t):
    n = zs.shape[0]
    spec = pl.BlockSpec((t, LANES), lambda bi: (bi, 0))
    return pl.pallas_call(
        functools.partial(_fox_cum_kernel, blk=256),
        grid=(b,),
        in_specs=[spec, _const_spec(bias_row.shape)],
        out_specs=[spec, spec],
        out_shape=[jax.ShapeDtypeStruct((n, LANES), BF)] * 2,
        compiler_params=_cparams(("parallel",)),
        name="fox_cum_gate",
    )(zs, bias_row)


def _fox_kernel(q_ref, k_ref, v_ref, qb_ref, kb_ref, o_ref, *, tq):
    pair = pl.program_id(1)
    t = q_ref.shape[0]
    lane = lax.broadcasted_iota(jnp.int32, (1, LANES), 1)
    zero = jnp.zeros((tq, LANES), BF)
    first = lane < FOX_HEAD_DIM
    bias_head = lane // FOX_BIAS_LANES
    ones = jnp.ones((tq, LANES), BF)
    on_diag = (lax.broadcasted_iota(jnp.int32, (2 * tq, tq), 1)
               <= lax.broadcasted_iota(jnp.int32, (2 * tq, tq), 0) % tq)

    def scores(i):
        rows_i = slice(i * tq, (i + 1) * tq)
        kv = (i + 1) * tq
        q2, qb = q_ref[rows_i, :], qb_ref[rows_i, :]
        rows = []
        for h in range(2):
            qh = jnp.where(first if h == 0 else jnp.logical_not(first), q2, zero)
            qbh = jnp.where(bias_head == 2 * pair + h, qb, zero)
            rows.append(jnp.concatenate([qh, qbh], axis=1))
        ka = jnp.concatenate([k_ref[:kv, :], kb_ref[:kv, :]], axis=1)
        s = lax.dot_general(jnp.concatenate(rows, axis=0), ka, (((1,), (1,)), ((), ())),
                            preferred_element_type=F32)
        diag = jnp.where(on_diag, s[:, kv - tq:], -jnp.inf)
        return diag if i == 0 else jnp.concatenate([s[:, :kv - tq], diag], axis=1)

    def finish(i, s):
        kv = (i + 1) * tq
        p = jnp.exp2(s - jnp.max(s, axis=-1, keepdims=True)).astype(BF)
        va = jnp.concatenate([v_ref[:kv, :], jnp.concatenate([ones] * (i + 1), axis=0)], axis=1)
        acc = jnp.dot(p, va, preferred_element_type=F32)
        o = acc[:, :LANES] / acc[:, LANES:]
        o_ref[i * tq:(i + 1) * tq, :] = jnp.where(first, o[:tq], o[tq:]).astype(o_ref.dtype)

    s = scores(0)
    for i in range(t // tq):
        s_next = scores(i + 1) if (i + 1) * tq < t else None
        finish(i, s)
        s = s_next


def _fox_attention(zb, qbias, kbias, b, t, tq):
    n = zb.shape[0]
    hp = FOX_HEADS // 2
    return pl.pallas_call(
        functools.partial(_fox_kernel, tq=tq),
        grid=(b, hp),
        in_specs=[pl.BlockSpec((t, LANES), lambda bi, p: (bi, p)),
                  pl.BlockSpec((t, LANES), lambda bi, p: (bi, hp + p)),
                  pl.BlockSpec((t, LANES), lambda bi, p: (bi, 2 * hp + p)),
                  pl.BlockSpec((t, LANES), lambda bi, p: (bi, 0)),
                  pl.BlockSpec((t, LANES), lambda bi, p: (bi, 0))],
        out_specs=pl.BlockSpec((t, LANES), lambda bi, p: (bi, p)),
        out_shape=jax.ShapeDtypeStruct((n, FOX_HEADS * FOX_HEAD_DIM), BF),
        compiler_params=_cparams(("parallel", "parallel")),
        name="fox_attention",
    )(zb, zb, zb, qbias, kbias)


def _neumann_inverse(mats, eye):
    ps = [eye - a for a in mats]
    aps = list(mats)
    k = 2
    while k < CHUNK:
        aps = [_mm(ap, ap) for ap in aps]
        yield
        ps = [p + _mm(p, ap) for p, ap in zip(ps, aps)]
        yield
        k *= 2
    return ps


def _gdn_kernel(alog_ref, dtb_ref, q_ref, k_ref, v_ref, gg_ref, gar_ref, gac_ref, gbc_ref,
                on_ref, o_ref, lhs_s, ku_s, u_s, qk_s, pend_s, grow_s, gcol_s, bcol_s, s_s):
    t = q_ref.shape[0]
    nc = t // CHUNK
    hd = GDN_HEAD_DIM
    lrows = hd + 2 * CHUNK

    ri = lax.broadcasted_iota(jnp.int32, (CHUNK, CHUNK), 0)
    ci = lax.broadcasted_iota(jnp.int32, (CHUNK, CHUNK), 1)
    lower = ri >= ci
    strict = ri > ci
    tril = lower.astype(BF)
    triu = (ri <= ci).astype(BF)
    eye = (ri == ci).astype(F32)
    for h in range(GDN_HEADS):
        a = -jnp.exp(alog_ref[h])
        grow_s[h] = _mm_exact_rhs(a * _softplus(gar_ref[0, h] + dtb_ref[h]), triu)
        gcol_s[h] = _mm_exact_lhs(tril, a * _softplus(gac_ref[0, h] + dtb_ref[h]))
        bcol_s[h] = _sigmoid(gbc_ref[0, h])
    s_s[...] = jnp.zeros_like(s_s)

    lane_nc = lax.broadcasted_iota(jnp.int32, (CHUNK, nc), 1)
    heads = range(GDN_HEADS)
    cat = lambda parts: jnp.concatenate(parts, axis=1)

    per_iter = 4

    hsl = lambda h: slice(h * hd, (h + 1) * hd)

    def chunk_rows(n, size):
        return pl.ds(n * size if isinstance(n, int) else pl.multiple_of(n * size, size), size)

    def local_work(it):
        rows = chunk_rows(it, per_iter * CHUNK)
        chains = [(c, h) for c in range(per_iter) for h in heads]
        csl = lambda c: slice(c * CHUNK, (c + 1) * CHUNK)
        q_all, k_all, v_all = (r[rows, :].astype(F32) for r in (q_ref, k_ref, v_ref))
        q = [q_all[csl(c), hsl(h)] for c, h in chains]
        k = [k_all[csl(c), hsl(h)] for c, h in chains]
        v = [v_all[csl(c), hsl(h)] for c, h in chains]
        sel = [lane_nc == it * per_iter + c for c in range(per_iter)]
        g_c = [jnp.sum(jnp.where(sel[c], gcol_s[h], 0.0), axis=1, keepdims=True)
               for c, h in chains]
        b_c = [jnp.sum(jnp.where(sel[c], bcol_s[h], 0.0), axis=1, keepdims=True)
               for c, h in chains]
        g_r = [grow_s[h, pl.ds(it * per_iter + c, 1), :] for c, h in chains]
        yield
        decay = [jnp.exp(jnp.where(lower, gc - gr, -jnp.inf)) for gc, gr in zip(g_c, g_r)]
        kb = [x * bc for x, bc in zip(k, b_c)]
        kk = [_mm_nt(x, y) for x, y in zip(kb, k)]
        yield
        qk = [_mm_nt(x, y) for x, y in zip(q, k)]
        yield
        t_inv = yield from _neumann_inverse(
            [jnp.where(strict, x * d, 0.0) for x, d in zip(kk, decay)], eye)
        eg = [jnp.exp(gc) for gc in g_c]
        uw = [_mm(ti, jnp.concatenate([x * bc, y * e], axis=1))
              for ti, x, bc, y, e in zip(t_inv, v, b_c, kb, eg)]
        yield
        kd = [x * jnp.exp(gr[:, CHUNK - 1:CHUNK] - gc) for x, gr, gc in zip(k, g_r, g_c)]
        kuw = [_mm_tn(x, y) for x, y in zip(kd, uw)]
        yield
        grid = lambda parts: jnp.concatenate(
            [cat(parts[c * GDN_HEADS:(c + 1) * GDN_HEADS]) for c in range(per_iter)], axis=0)
        lhs = [jnp.concatenate([m[:, hd:], x[:, hd:], y * e], axis=0)
               for m, x, y, e in zip(kuw, uw, q, eg)]
        lhs_s[chunk_rows(it, per_iter * lrows), :] = grid(lhs).astype(BF)
        ku_s[chunk_rows(it, per_iter * hd), :] = grid([m[:, :hd] for m in kuw])
        u_s[rows, :] = grid([x[:, :hd] for x in uw])
        qk_s[rows, :] = grid([x * d for x, d in zip(qk, decay)]).astype(BF)

    def advance(n):
        s_prev = [s_s[h] for h in heads]
        prod = [_mm(lhs_s[chunk_rows(n, lrows), hsl(h)], s_prev[h]) for h in heads]
        g_last = [grow_s[h, pl.ds(n, 1), :][:, CHUNK - 1:CHUNK] for h in heads]
        s_s[...] = jnp.stack([s_prev[h] * jnp.exp(g_last[h]) - prod[h][:hd]
                              + ku_s[chunk_rows(n, hd), hsl(h)] for h in heads])
        pend_s[...] = cat([prod[h][hd:] for h in heads])

    def emit(n):
        rows = chunk_rows(n, CHUNK)
        pend = pend_s[...]
        v_new = [u_s[rows, hsl(h)] - pend[:CHUNK, hsl(h)] for h in heads]
        qv = [_mm(qk_s[rows, h * CHUNK:(h + 1) * CHUNK], v_new[h]) for h in heads]
        outs = [_rms(pend[CHUNK:, hsl(h)] + qv[h], on_ref[...]) for h in heads]
        gate = _silu(gg_ref[rows, :].astype(F32))
        o_ref[rows, :] = (cat(outs) * gate).astype(o_ref.dtype)

    def state_work(first, has_previous):
        for c in range(per_iter):
            if has_previous or c:
                emit(first + c - 1)
                yield
            advance(first + c)
            yield

    def interleave(*gens):
        active = list(enumerate(gens))
        while active:
            for item in list(active):
                try:
                    for _ in range(2 if item[0] else 1):
                        next(item[1])
                except StopIteration:
                    active.remove(item)

    n_iter = nc // per_iter
    interleave(local_work(0))
    if n_iter > 1:
        interleave(local_work(1), state_work(0, False))

    def merged_step(it, carry):
        interleave(local_work(it), state_work((it - 1) * per_iter, True))
        return carry

    lax.fori_loop(2, n_iter, merged_step, 0)
    interleave(state_work((n_iter - 1) * per_iter, n_iter > 1))
    emit(nc - 1)


def _gdn(zb, ga_row, ga_col, gb_col, a_log, dt_bias, o_norm, b, t):
    n = zb.shape[0]
    nc = t // CHUNK
    width = GDN_HEADS * GDN_HEAD_DIM
    fox_blocks = 3 * FOX_HEADS * FOX_HEAD_DIM // width
    colspec = lambda j: pl.BlockSpec((t, width), lambda bi: (bi, fox_blocks + j))
    gspec = lambda shp: pl.BlockSpec((1,) + shp, lambda bi: (bi, 0, 0, 0))
    return pl.pallas_call(
        _gdn_kernel,
        grid=(b,),
        in_specs=[_const_spec(a_log.shape), _const_spec(dt_bias.shape),
                  colspec(0), colspec(1), colspec(2), colspec(3),
                  gspec((GDN_HEADS, nc, CHUNK)), gspec((GDN_HEADS, CHUNK, nc)),
                  gspec((GDN_HEADS, CHUNK, nc)), _const_spec(o_norm.shape)],
        out_specs=pl.BlockSpec((t, width), lambda bi: (bi, 0)),
        out_shape=jax.ShapeDtypeStruct((n, width), BF),
        scratch_shapes=[pltpu.VMEM((nc * (GDN_HEAD_DIM + 2 * CHUNK), width), BF),
                        pltpu.VMEM((nc * GDN_HEAD_DIM, width), F32),
                        pltpu.VMEM((t, width), F32),
                        pltpu.VMEM((t, GDN_HEADS * CHUNK), BF),
                        pltpu.VMEM((2 * CHUNK, width), F32),
                        pltpu.VMEM((GDN_HEADS, nc, CHUNK), F32),
                        pltpu.VMEM((GDN_HEADS, CHUNK, nc), F32),
                        pltpu.VMEM((GDN_HEADS, CHUNK, nc), F32),
                        pltpu.VMEM((GDN_HEADS, GDN_HEAD_DIM, GDN_HEAD_DIM), F32)],
        compiler_params=_cparams(("parallel",)),
        name="gated_deltanet",
    )(a_log, dt_bias, zb, zb, zb, zb, ga_row, ga_col, gb_col, o_norm)


def _mlp_kernel(a1_ref, a2_ref, x_ref, wo1_ref, wo2_ref, g_ref, w1_ref, w2_ref, o_ref, acc_ref,
                *, fc):
    mix = (jnp.dot(a1_ref[...], wo1_ref[...], preferred_element_type=F32)
           + jnp.dot(a2_ref[...], wo2_ref[...], preferred_element_type=F32))
    x1 = x_ref[...] + mix
    h = _rms(x1, g_ref[...]).astype(BF)
    acc_ref[...] = x1
    for c in range(w1_ref.shape[1] // fc):
        a = jnp.maximum(jnp.dot(h, w1_ref[:, c * fc:(c + 1) * fc],
                                preferred_element_type=F32), 0.0)
        acc_ref[...] += jnp.dot((a * a).astype(BF), w2_ref[c * fc:(c + 1) * fc, :],
                                preferred_element_type=F32)
    o_ref[...] = acc_ref[...]


def _mlp(a1, a2, a2_block, x2, wo1, wo2, gain, w1, w2, tm):
    n, d = x2.shape
    half = wo1.shape[0]
    return pl.pallas_call(
        functools.partial(_mlp_kernel, fc=1024),
        grid=(n // tm,),
        in_specs=[pl.BlockSpec((tm, half), lambda i: (i, 0)),
                  pl.BlockSpec((tm, half), lambda i: (i, a2_block)),
                  pl.BlockSpec((tm, d), lambda i: (i, 0)),
                  _const_spec(wo1.shape), _const_spec(wo2.shape), _const_spec(gain.shape),
                  _const_spec(w1.shape), _const_spec(w2.shape)],
        out_specs=pl.BlockSpec((tm, d), lambda i: (i, 0)),
        out_shape=jax.ShapeDtypeStruct((n, d), F32),
        scratch_shapes=[pltpu.VMEM((tm, d), F32)],
        compiler_params=_cparams(("parallel",)),
        name="out_proj_mlp",
    )(a1, a2, x2, wo1, wo2, gain, w1, w2)


def _in1_kernel(x_ref, g_ref, wb_ref, wf_ref, zb_ref, zf_ref):
    h = _rms(x_ref[...], g_ref[...]).astype(BF)
    gw = 512
    for c in range(wb_ref.shape[1] // gw):
        zb_ref[:, c * gw:(c + 1) * gw] = jnp.dot(
            h, wb_ref[:, c * gw:(c + 1) * gw], preferred_element_type=F32).astype(BF)
    for c in range(wf_ref.shape[1] // gw):
        zf_ref[:, c * gw:(c + 1) * gw] = jnp.dot(
            h, wf_ref[:, c * gw:(c + 1) * gw], preferred_element_type=F32)


def _in1(x2, gain, wb, wf, tm):
    n, d = x2.shape
    return pl.pallas_call(
        _in1_kernel,
        grid=(n // tm,),
        in_specs=[pl.BlockSpec((tm, d), lambda i: (i, 0)),
                  _const_spec(gain.shape), _const_spec(wb.shape), _const_spec(wf.shape)],
        out_specs=[pl.BlockSpec((tm, wb.shape[1]), lambda i: (i, 0)),
                   pl.BlockSpec((tm, wf.shape[1]), lambda i: (i, 0))],
        out_shape=[jax.ShapeDtypeStruct((n, wb.shape[1]), BF),
                   jax.ShapeDtypeStruct((n, wf.shape[1]), F32)],
        compiler_params=_cparams(("parallel",)),
        name="l1_in_proj",
    )(x2, gain, wb, wf)


HG_GROUP = 4
HG_HEADS_PER_STEP = 1


HG_LEVELS = (32, 16, 8, 4, 2, 1)


def _hgrn_term_mask(terms):
    t = lax.broadcasted_iota(jnp.int32, (CHUNK, len(terms) * CHUNK), 0)
    col = lax.broadcasted_iota(jnp.int32, (CHUNK, len(terms) * CHUNK), 1)
    s = col % CHUNK
    mask = None
    for j, m in enumerate(terms):
        own = (t == s) if m == 0 else (
            (t % (2 * m) >= m) & (s % (2 * m) < m) & (t // (2 * m) == s // (2 * m)))
        own = own & (col // CHUNK == j)
        mask = own if mask is None else mask | own
    return mask


def _hgrn_level_operands(q, k, b, ref_row):
    d = q.shape[1]
    nb = CHUNK // SUBLANES
    blk = lambda x, i: x[SUBLANES * i:SUBLANES * (i + 1)]
    sub = lax.broadcasted_iota(jnp.int32, (SUBLANES, d), 0)
    xs = []
    for m in HG_LEVELS:
        parts = []
        for i in range(nb):
            bi, qi, ki = blk(b, i), blk(q, i), blk(k, i)
            row0 = SUBLANES * i
            if m >= SUBLANES:
                ref = row0 // (2 * m) * (2 * m) + m
                if row0 % (2 * m) >= m:
                    parts.append(qi * jnp.exp2(bi - ref_row(ref)))
                else:
                    parts.append(ki * jnp.exp2(ref_row(ref) - bi))
                continue
            upper = sub % (2 * m) >= m
            if m == 4:
                bref = ref_row(row0 + 4)
            elif m == 2:
                bref = jnp.where(sub < 4, ref_row(row0 + 2), ref_row(row0 + 6))
            else:
                bref = jnp.where(upper, bi, pltpu.roll(bi, SUBLANES - 1, 0))
            dlt = bi - bref
            parts.append(jnp.where(upper, qi, ki) * jnp.exp2(jnp.where(upper, dlt, -dlt)))
        xs.append(jnp.concatenate(parts, axis=0).astype(BF))
        yield
    return xs


def _hgrn_kernel(zq_ref, zi_ref, zg_ref, zf_ref, lbl_ref, on_ref, o_ref,
                 oi_s, qd_s, kv_s, dec_s, h1_s, *, group):
    t = zq_ref.shape[0]
    d = HGRN_HEAD_DIM
    heads = range(zq_ref.shape[1] // d)
    hs = lambda hh: slice(hh * d, (hh + 1) * d)
    gr = group * CHUNK
    ng = t // gr
    ri = lax.broadcasted_iota(jnp.int32, (gr, gr), 0)
    ci = lax.broadcasted_iota(jnp.int32, (gr, gr), 1)
    tril = ((ri >= ci) & (ri // CHUNK == ci // CHUNK)).astype(BF)
    term_mask = {m: _hgrn_term_mask((m,)) for m in (0,) + HG_LEVELS}
    sls =[slice(c * CHUNK, (c + 1) * CHUNK) for c in range(group)]

    l0 = lbl_ref[0:1, :]
    l1 = lbl_ref[1:2, :]
    mx = jnp.maximum(l0, l1)
    e0 = jnp.exp(l0 - mx)
    e1 = jnp.exp(l1 - mx)
    s0 = e0 / (e0 + e1)
    s1 = e1 / (e0 + e1)
    lb = (s0 + s1) - s0

    def group_rows(g):
        return pl.ds(g * gr, gr)

    def prelude(hh, g):
        rows = group_rows(g)
        qs, ks, ls = [], [], []
        for sl in sls:
            qs.append(_silu(zq_ref[rows, hs(hh)][sl].astype(F32)))
            f = lb[:, hs(hh)] + (1.0 - lb[:, hs(hh)]) * _sigmoid(zf_ref[rows, hs(hh)][sl])
            ks.append(1.0 - f)
            ls.append(jnp.log2(f))
            yield
        q, k = jnp.concatenate(qs, axis=0), jnp.concatenate(ks, axis=0)
        b = _mm_exact_lhs(tril, jnp.concatenate(ls, axis=0))
        h1_s[hh, g, 0] = b
        h1_s[hh, g, 1] = k
        h1_s[hh, g, 2] = q
        qd_s[hh, rows, :] = (q * jnp.exp2(b)).astype(BF)
        yield

    def operands(hh, g):
        b, k, q = h1_s[hh, g, 0], h1_s[hh, g, 1], h1_s[hh, g, 2]
        ops = []
        for c, sl in enumerate(sls):
            ref_row = lambda r, c=c: h1_s[hh, g, 0, c * CHUNK + r:c * CHUNK + r + 1, :]
            xs = yield from _hgrn_level_operands(q[sl], k[sl], b[sl], ref_row)
            b_last = b[sl][CHUNK - 1:CHUNK, :]
            kdec = (k[sl] * jnp.exp2(b_last - b[sl])).astype(BF)
            dec_s[hh, g * group + c] = jnp.exp2(b_last)
            ops.append((xs, q[sl].astype(BF), k[sl].astype(BF), kdec))
            yield
        return ops

    def products(hh, g, ops):
        rows = group_rows(g)
        v = zi_ref[rows, hs(hh)]
        dmat, kv, oi = [], [], []
        for (xs, qb, kb, kdec), sl in zip(ops, sls):
            dm = jnp.where(term_mask[0], _mm_nt(qb, kb), 0.0)
            yield
            for m, x in zip(HG_LEVELS, xs):
                dm = jnp.where(term_mask[m], _mm_nt(x, x), dm)
                yield
            kv.append(_mm_tn(v[sl], kdec))
            dmat.append(dm)
            yield
        for dm, sl in zip(dmat, sls):
            oi.append(_mm(dm, v[sl]))
            yield
        oi_s[hh, rows, :] = jnp.concatenate(oi, axis=0)
        kv_s[hh, pl.ds(g * group, group)] = jnp.stack(kv)

    def advance(*stages):
        results = [None] * len(stages)
        active = list(enumerate(stages))
        while active:
            for item in list(active):
                j, (gen, turns) = item
                try:
                    for _ in range(turns):
                        next(gen)
                except StopIteration as done:
                    results[j] = done.value
                    active.remove(item)
        return results

    for hh in heads:
        advance((prelude(hh, 0), 1), (prelude(hh, 1), 1))
        ops = advance((operands(hh, 0), 1))[0]
        for g in range(1, ng):
            stages = [(products(hh, g - 1, ops), 1), (operands(hh, g), 1)]
            if g + 1 < ng:
                stages.append((prelude(hh, g + 1), 1))
            ops = advance(*stages)[1]
        advance((products(hh, ng - 1, ops), 1))

    per_step = 4 * group
    sr = per_step * CHUNK

    def state_step(g, sts):
        r0 = pl.multiple_of(g * sr, sr)
        rows = pl.ds(r0, sr)
        new_sts, normed = [], []
        for hh, st in zip(heads, sts):
            states = []
            for c in range(per_step):
                n = g * per_step + c
                states.append(st)
                st = st * dec_s[hh, n] + kv_s[hh, n]
            new_sts.append(st)
            outs = [oi_s[hh, pl.ds(r0 + c * CHUNK, CHUNK), :]
                    + _mm_nt(qd_s[hh, pl.ds(r0 + c * CHUNK, CHUNK), :], states[c])
                    for c in range(per_step)]
            normed.append(_rms(jnp.concatenate(outs, axis=0), on_ref[...]))
        o = jnp.concatenate(normed, axis=1)
        o_ref[rows, :] = (o * _silu(zg_ref[rows, :].astype(F32))).astype(o_ref.dtype)
        return tuple(new_sts)

    lax.fori_loop(0, t // sr, state_step, tuple(jnp.zeros((d, d), F32) for _ in heads))


def _hgrn(zb, zf, lb_logits, o_norm, b, t):
    n = zb.shape[0]
    hd = HGRN_HEAD_DIM
    nh = HGRN_HEADS
    nc = t // CHUNK
    gr = HG_GROUP * CHUNK
    hps = HG_HEADS_PER_STEP
    ns = nh // hps
    wide = hps * hd
    return pl.pallas_call(
        functools.partial(_hgrn_kernel, group=HG_GROUP),
        grid=(b, ns),
        in_specs=[pl.BlockSpec((t, wide), lambda bi, h: (bi, h)),
                  pl.BlockSpec((t, wide), lambda bi, h: (bi, ns + h)),
                  pl.BlockSpec((t, wide), lambda bi, h: (bi, 2 * ns + h)),
                  pl.BlockSpec((t, wide), lambda bi, h: (bi, h)),
                  pl.BlockSpec((lb_logits.shape[0], wide), lambda bi, h: (0, h)),
                  _const_spec(o_norm.shape)],
        out_specs=pl.BlockSpec((t, wide), lambda bi, h: (bi, h)),
        out_shape=jax.ShapeDtypeStruct((n, nh * hd), BF),
        scratch_shapes=[pltpu.VMEM((hps, t, hd), F32), pltpu.VMEM((hps, t, hd), BF),
                        pltpu.VMEM((hps, nc, hd, hd), F32), pltpu.VMEM((hps, nc, 1, hd), F32),
                        pltpu.VMEM((hps, t // gr, 3, gr, hd), F32)],
        compiler_params=_cparams(("parallel", "parallel")),
        name="hgrn2",
    )(zb, zb, zb, zf, lb_logits, o_norm)


def kernel(x, l0_mix_norm, l0_w_in, l0_fox_q_norm, l0_fox_k_norm, l0_fox_f_bias, l0_gdn_conv,
           l0_gdn_A_log, l0_gdn_dt_bias, l0_gdn_o_norm, l0_w_out, l0_ffn_norm, l0_w_ff1,
           l0_w_ff2, l1_mix_norm, l1_w_in, l1_hgrn_o_norm, l1_w_out, l1_ffn_norm, l1_w_ff1,
           l1_w_ff2, hgrn_lb_logits):
    b, t, d = x.shape
    n = b * t
    fw = FOX_HEADS * FOX_HEAD_DIM
    gw = GDN_HEADS * GDN_HEAD_DIM
    assert hgrn_lb_logits.shape[0] == 2 and d == HGRN_HEADS * HGRN_HEAD_DIM
    assert t % (2 * HG_GROUP * CHUNK) == 0
    tm = 512 if t % 512 == 0 else 256
    tq = 256
    nc = t // CHUNK
    row = lambda p: p.reshape(1, -1).astype(F32)
    x2 = x.reshape(n, d)

    o_ff = 3 * fw
    o_gq = o_ff + FOX_HEADS
    o_gb = o_gq + 3 * gw
    o_gg = o_gb + 2 * GDN_HEADS
    wb0 = jnp.concatenate([l0_w_in[:, :o_ff], l0_w_in[:, o_gq:o_gb], l0_w_in[:, o_gg:]],
                          axis=1).astype(BF)
    n_small = FOX_HEADS + 2 * GDN_HEADS
    ws0 = jnp.concatenate([l0_w_in[:, o_ff:o_gq], l0_w_in[:, o_gb:o_gg],
                           jnp.zeros((d, LANES - n_small), F32)], axis=1).astype(BF)
    hid = jnp.arange(fw) // FOX_HEAD_DIM
    pm = jnp.where(hid[:, None] == hid[None, :], 1.0 / FOX_HEAD_DIM, 0.0).astype(BF)
    qkg = jnp.stack([jnp.tile(l0_fox_q_norm, FOX_HEADS) * (FOX_HEAD_DIM ** -0.5 * LOG2E),
                     jnp.tile(l0_fox_k_norm, FOX_HEADS)]).astype(F32)

    tm0 = 1024 if t % 1024 == 0 else tm
    zb0, zs0 = _in0(x2, row(l0_mix_norm), wb0, ws0, pm, qkg, l0_gdn_conv.astype(F32), tm0, t)

    f_bias = jnp.pad(l0_fox_f_bias.astype(F32), (0, LANES - FOX_HEADS)).reshape(1, LANES)
    qbias, kbias = _fox_cum(zs0, f_bias, b, t)
    fox_o = _fox_attention(zb0, qbias, kbias, b, t, tq)

    def gate_rows(lo):
        g = zs0[:, lo:lo + GDN_HEADS].reshape(b, t, GDN_HEADS).transpose(0, 2, 1)
        return g.reshape(b, GDN_HEADS, nc, CHUNK)
    gb_row = gate_rows(FOX_HEADS)
    ga_row = gate_rows(FOX_HEADS + GDN_HEADS)
    gdn_o = _gdn(zb0, ga_row, ga_row.swapaxes(-1, -2), gb_row.swapaxes(-1, -2),
                 l0_gdn_A_log.reshape(GDN_HEADS, 1, 1).astype(F32),
                 l0_gdn_dt_bias.reshape(GDN_HEADS, 1, 1).astype(F32), row(l0_gdn_o_norm), b, t)

    wo0 = l0_w_out.astype(BF)
    x1 = _mlp(fox_o, gdn_o, 0, x2, wo0[:fw], wo0[fw:], row(l0_ffn_norm),
              l0_w_ff1.astype(BF), l0_w_ff2.astype(BF), tm)

    wb1 = jnp.concatenate([l1_w_in[:, :d], l1_w_in[:, 2 * d:]], axis=1).astype(BF)
    wf1 = l1_w_in[:, d:2 * d].astype(BF)
    zb1, zf1 = _in1(x1, row(l1_mix_norm), wb1, wf1, tm0)
    ho = _hgrn(zb1, zf1, hgrn_lb_logits.astype(F32), row(l1_hgrn_o_norm), b, t)
    wo1 = l1_w_out.astype(BF)
    half = d // 2
    x3 = _mlp(ho, ho, 1, x1, wo1[:half], wo1[half:], row(l1_ffn_norm),
              l1_w_ff1.astype(BF), l1_w_ff2.astype(BF), tm)
    return x3.reshape(b, t, d)
```

```python
import functools

import jax
import jax.numpy as jnp
from jax import lax
from jax.experimental import pallas as pl
from jax.experimental.pallas import tpu as pltpu

BF = jnp.bfloat16
F32 = jnp.float32
NORM_EPS = 1e-6
LOG2E = 1.4426950408889634
CHUNK = 64
FOX_HEADS, FOX_HEAD_DIM = 8, 64
GDN_HEADS, GDN_HEAD_DIM = 4, 128
HGRN_HEADS, HGRN_HEAD_DIM = 8, 128
GDN_CONV = 4
IN0_ROWS = 128
LANES = 128
SUBLANES = 8
VMEM_LIMIT = 56 * 1024 * 1024


def _cparams(sem):
    return pltpu.CompilerParams(dimension_semantics=sem, vmem_limit_bytes=VMEM_LIMIT)


def _const_spec(shape):
    nd = len(shape)
    return pl.BlockSpec(shape, lambda *_: (0,) * nd, pipeline_mode=pl.Buffered(1))


def _mm(a, b):
    return jnp.dot(a.astype(BF), b.astype(BF), preferred_element_type=F32)


def _mm_nt(a, b):
    return lax.dot_general(a.astype(BF), b.astype(BF), (((1,), (1,)), ((), ())),
                           preferred_element_type=F32)


def _mm_tn(a, b):
    return lax.dot_general(a.astype(BF), b.astype(BF), (((0,), (0,)), ((), ())),
                           preferred_element_type=F32)


def _split3(x):
    hi = x.astype(BF)
    r1 = x - hi.astype(F32)
    mid = r1.astype(BF)
    lo = (r1 - mid.astype(F32)).astype(BF)
    return hi, mid, lo


def _mm_exact_lhs(m_bf, x):
    n = x.shape[1]
    if n % LANES:
        hi, mid, lo = _split3(x)
        d = lambda p: jnp.dot(m_bf, p, preferred_element_type=F32)
        return d(hi) + d(mid) + d(lo)
    y = jnp.dot(m_bf, jnp.concatenate(_split3(x), axis=1), preferred_element_type=F32)
    return y[:, :n] + y[:, n:2 * n] + y[:, 2 * n:]


def _mm_exact_rhs(x, m_bf):
    hi, mid, lo = _split3(x)
    d = lambda p: jnp.dot(p, m_bf, preferred_element_type=F32)
    return d(hi) + d(mid) + d(lo)


def _sigmoid(x):
    return 1.0 / (1.0 + jnp.exp(-x))


def _silu(x):
    return x * _sigmoid(x)


def _softplus(x):
    return jnp.maximum(x, 0.0) + jnp.log1p(jnp.exp(-jnp.abs(x)))


def _rms(x, gain):
    return x * lax.rsqrt(jnp.mean(x * x, axis=-1, keepdims=True) + NORM_EPS) * gain


def _in0_kernel(x_ref, g_ref, wb_ref, ws_ref, pm_ref, qkg_ref, cw_ref, zb_ref, zs_ref, zst_ref,
                hist_ref, *, tiles_per_seq):
    tm = x_ref.shape[0]
    hd = GDN_HEAD_DIM
    hrows = hist_ref.shape[1]

    @pl.when(pl.program_id(0) % tiles_per_seq == 0)
    def _():
        hist_ref[...] = jnp.zeros_like(hist_ref)

    h = _rms(x_ref[...], g_ref[...]).astype(BF)
    gw = 512
    rb = IN0_ROWS

    def project(c, r):
        return jnp.dot(h[r * rb:(r + 1) * rb], wb_ref[:, c * gw:(c + 1) * gw],
                       preferred_element_type=F32)

    def finish(c, r, z, above):
        if c < 2:
            ms = jnp.dot((z * z).astype(BF), pm_ref[...], preferred_element_type=F32)
            z = z * lax.rsqrt(ms + NORM_EPS) * qkg_ref[c:c + 1, :]
        elif 3 <= c < 6:
            j = c - 3
            w = cw_ref[:, j * gw:(j + 1) * gw]
            ext = jnp.concatenate([above, z], axis=0)
            y = z * w[GDN_CONV - 1:GDN_CONV]
            for s in range(1, GDN_CONV):
                y = y + ext[hrows - s:hrows - s + rb] * w[GDN_CONV - 1 - s:GDN_CONV - s]
            z = _silu(y)
            if j < 2:
                scale = 1.0 if j else hd ** -0.5
                z = jnp.concatenate(
                    [zh * (lax.rsqrt(jnp.sum(zh * zh, axis=-1, keepdims=True) + NORM_EPS) * scale)
                     for zh in (z[:, k * hd:(k + 1) * hd] for k in range(gw // hd))], axis=1)
        zb_ref[r * rb:(r + 1) * rb, c * gw:(c + 1) * gw] = z.astype(BF)

    pending = None
    for c in range(wb_ref.shape[1] // gw):
        conv = 3 <= c < 6
        above = hist_ref[c - 3] if conv else None
        for r in range(tm // rb):
            z = project(c, r)
            if pending is not None:
                finish(*pending)
            pending = (c, r, z, above)
            above = z[rb - hrows:, :] if conv else None
        if conv:
            hist_ref[c - 3] = above
    finish(*pending)
    zs = jnp.dot(h, ws_ref[...], preferred_element_type=F32)
    zs_ref[...] = zs
    zst_ref[...] = zs.T


def _in0(x2, gain, wb, ws, pm, qkg, conv_w, tm, t):
    n, d = x2.shape
    nb = wb.shape[1]
    return pl.pallas_call(
        functools.partial(_in0_kernel, tiles_per_seq=t // tm),
        grid=(n // tm,),
        in_specs=[pl.BlockSpec((tm, d), lambda i: (i, 0)),
                  _const_spec((1, d)), _const_spec(wb.shape), _const_spec(ws.shape),
                  _const_spec(pm.shape), _const_spec(qkg.shape), _const_spec(conv_w.shape)],
        out_specs=[pl.BlockSpec((tm, nb), lambda i: (i, 0)),
                   pl.BlockSpec((tm, LANES), lambda i: (i, 0)),
                   pl.BlockSpec((LANES, tm), lambda i: (0, i))],
        out_shape=[jax.ShapeDtypeStruct((n, nb), BF), jax.ShapeDtypeStruct((n, LANES), F32),
                   jax.ShapeDtypeStruct((LANES, n), F32)],
        scratch_shapes=[pltpu.VMEM((3, SUBLANES, 512), F32)],
        compiler_params=_cparams(("arbitrary",)),
        name="l0_in_proj",
    )(x2, gain, wb, ws, pm, qkg, conv_w)


FOX_BIAS_LANES = 8


def _fox_cum_kernel(zs_ref, bias_ref, qb_ref, kb_ref, *, blk):
    t = zs_ref.shape[0]
    r = lax.broadcasted_iota(jnp.int32, (blk, blk), 0)
    c = lax.broadcasted_iota(jnp.int32, (blk, blk), 1)
    tril = (r >= c).astype(BF)
    pr = lax.broadcasted_iota(jnp.int32, (3 * LANES, 2 * LANES), 0)
    pc = lax.broadcasted_iota(jnp.int32, (3 * LANES, 2 * LANES), 1)
    place = ((pr % LANES < FOX_HEADS)
             & (pc % LANES == (pr % LANES) * FOX_BIAS_LANES + pr // LANES + 3 * (pc // LANES))
             ).astype(BF)
    lane = lax.broadcasted_iota(jnp.int32, (1, LANES), 1)
    slot = lane % FOX_BIAS_LANES
    used = lane < FOX_HEADS * FOX_BIAS_LANES
    ones_q = jnp.where(used & (slot >= 3) & (slot < 6), 1.0, 0.0)
    ones_k = jnp.where(used & (slot < 3), 1.0, 0.0)
    blocks = [slice(i * blk, (i + 1) * blk) for i in range(t // blk)]
    local = []
    for rows in blocks:
        z = zs_ref[rows, :] + bias_ref[...]
        ls = jnp.minimum(z, 0.0) - jnp.log1p(jnp.exp(-jnp.abs(z)))
        local.append(_mm_exact_lhs(tril, ls))
    carry = jnp.zeros((1, LANES), F32)
    for rows, loc in zip(blocks, local):
        cb = loc + carry
        carry = cb[blk - 1:blk, :]
        feat = jnp.dot(jnp.concatenate(_split3(cb * LOG2E), axis=1), place,
                       preferred_element_type=F32)
        qb_ref[rows, :] = (feat[:, :LANES] + ones_q).astype(BF)
        kb_ref[rows, :] = (ones_k - feat[:, LANES:]).astype(BF)


def _fox_cum(zs, bias_row, b, t):
    n = zs.shape[0]
    spec = pl.BlockSpec((t, LANES), lambda bi: (bi, 0))
    return pl.pallas_call(
        functools.partial(_fox_cum_kernel, blk=256),
        grid=(b,),
        in_specs=[spec, _const_spec(bias_row.shape)],
        out_specs=[spec, spec],
        out_shape=[jax.ShapeDtypeStruct((n, LANES), BF)] * 2,
        compiler_params=_cparams(("parallel",)),
        name="fox_cum_gate",
    )(zs, bias_row)


def _fox_kernel(q_ref, k_ref, v_ref, qb_ref, kb_ref, o_ref, *, tq):
    pair = pl.program_id(1)
    t = q_ref.shape[0]
    lane = lax.broadcasted_iota(jnp.int32, (1, LANES), 1)
    zero = jnp.zeros((tq, LANES), BF)
    first = lane < FOX_HEAD_DIM
    bias_head = lane // FOX_BIAS_LANES
    ones = jnp.ones((tq, LANES), BF)
    on_diag = (lax.broadcasted_iota(jnp.int32, (2 * tq, tq), 1)
               <= lax.broadcasted_iota(jnp.int32, (2 * tq, tq), 0) % tq)

    def scores(i):
        rows_i = slice(i * tq, (i + 1) * tq)
        kv = (i + 1) * tq
        q2, qb = q_ref[rows_i, :], qb_ref[rows_i, :]
        rows = []
        for h in range(2):
            qh = jnp.where(first if h == 0 else jnp.logical_not(first), q2, zero)
            qbh = jnp.where(bias_head == 2 * pair + h, qb, zero)
            rows.append(jnp.concatenate([qh, qbh], axis=1))
        ka = jnp.concatenate([k_ref[:kv, :], kb_ref[:kv, :]], axis=1)
        s = lax.dot_general(jnp.concatenate(rows, axis=0), ka, (((1,), (1,)), ((), ())),
                            preferred_element_type=F32)
        diag = jnp.where(on_diag, s[:, kv - tq:], -jnp.inf)
        return diag if i == 0 else jnp.concatenate([s[:, :kv - tq], diag], axis=1)

    def finish(i, s):
        kv = (i + 1) * tq
        p = jnp.exp2(s - jnp.max(s, axis=-1, keepdims=True)).astype(BF)
        va = jnp.concatenate([v_ref[:kv, :], jnp.concatenate([ones] * (i + 1), axis=0)], axis=1)
        acc = jnp.dot(p, va, preferred_element_type=F32)
        o = acc[:, :LANES] / acc[:, LANES:]
        o_ref[i * tq:(i + 1) * tq, :] = jnp.where(first, o[:tq], o[tq:]).astype(o_ref.dtype)

    s = scores(0)
    for i in range(t // tq):
        s_next = scores(i + 1) if (i + 1) * tq < t else None
        finish(i, s)
        s = s_next


def _fox_attention(zb, qbias, kbias, b, t, tq):
    n = zb.shape[0]
    hp = FOX_HEADS // 2
    return pl.pallas_call(
        functools.partial(_fox_kernel, tq=tq),
        grid=(b, hp),
        in_specs=[pl.BlockSpec((t, LANES), lambda bi, p: (bi, p)),
                  pl.BlockSpec((t, LANES), lambda bi, p: (bi, hp + p)),
                  pl.BlockSpec((t, LANES), lambda bi, p: (bi, 2 * hp + p)),
                  pl.BlockSpec((t, LANES), lambda bi, p: (bi, 0)),
                  pl.BlockSpec((t, LANES), lambda bi, p: (bi, 0))],
        out_specs=pl.BlockSpec((t, LANES), lambda bi, p: (bi, p)),
        out_shape=jax.ShapeDtypeStruct((n, FOX_HEADS * FOX_HEAD_DIM), BF),
        compiler_params=_cparams(("parallel", "parallel")),
        name="fox_attention",
    )(zb, zb, zb, qbias, kbias)


def _neumann_inverse(mats, eye):
    ps = [eye - a for a in mats]
    aps = list(mats)
    k = 2
    while k < CHUNK:
        aps = [_mm(ap, ap) for ap in aps]
        yield
        ps = [p + _mm(p, ap) for p, ap in zip(ps, aps)]
        yield
        k *= 2
    return ps


def _gdn_kernel(alog_ref, dtb_ref, q_ref, k_ref, v_ref, gg_ref, gar_ref, gbr_ref,
                on_ref, o_ref, lhs_s, ku_s, u_s, qk_s, pend_s, grow_s, gcol_s, bcol_s, s_s):
    t = q_ref.shape[0]
    nc = t // CHUNK
    hd = GDN_HEAD_DIM
    lrows = hd + 2 * CHUNK

    ri = lax.broadcasted_iota(jnp.int32, (CHUNK, CHUNK), 0)
    ci = lax.broadcasted_iota(jnp.int32, (CHUNK, CHUNK), 1)
    lower = ri >= ci
    strict = ri > ci
    triu = (ri <= ci).astype(BF)
    eye = (ri == ci).astype(F32)
    eye_bf = eye.astype(BF)

    def transposed(x):
        return sum(lax.dot_general(eye_bf, p, (((1,), (1,)), ((), ())),
                                   preferred_element_type=F32) for p in _split3(x))

    for h in range(GDN_HEADS):
        a = -jnp.exp(alog_ref[h])
        g_row = _mm_exact_rhs(a * _softplus(gar_ref[0, h] + dtb_ref[h]), triu)
        grow_s[h] = g_row
        gcol_s[h] = transposed(g_row)
        bcol_s[h] = transposed(_sigmoid(gbr_ref[0, h]))
    s_s[...] = jnp.zeros_like(s_s)

    lane_nc = lax.broadcasted_iota(jnp.int32, (CHUNK, nc), 1)
    heads = range(GDN_HEADS)
    cat = lambda parts: jnp.concatenate(parts, axis=1)

    per_iter = 4

    hsl = lambda h: slice(h * hd, (h + 1) * hd)

    def chunk_rows(n, size):
        return pl.ds(n * size if isinstance(n, int) else pl.multiple_of(n * size, size), size)

    def local_work(it):
        rows = chunk_rows(it, per_iter * CHUNK)
        chains = [(c, h) for c in range(per_iter) for h in heads]
        csl = lambda c: slice(c * CHUNK, (c + 1) * CHUNK)
        q_all, k_all, v_all = (r[rows, :].astype(F32) for r in (q_ref, k_ref, v_ref))
        q = [q_all[csl(c), hsl(h)] for c, h in chains]
        k = [k_all[csl(c), hsl(h)] for c, h in chains]
        v = [v_all[csl(c), hsl(h)] for c, h in chains]
        sel = [lane_nc == it * per_iter + c for c in range(per_iter)]
        g_c = [jnp.sum(jnp.where(sel[c], gcol_s[h], 0.0), axis=1, keepdims=True)
               for c, h in chains]
        b_c = [jnp.sum(jnp.where(sel[c], bcol_s[h], 0.0), axis=1, keepdims=True)
               for c, h in chains]
        g_r = [grow_s[h, pl.ds(it * per_iter + c, 1), :] for c, h in chains]
        yield
        decay = [jnp.exp(jnp.where(lower, gc - gr, -jnp.inf)) for gc, gr in zip(g_c, g_r)]
        kb = [x * bc for x, bc in zip(k, b_c)]
        kk = [_mm_nt(x, y) for x, y in zip(kb, k)]
        yield
        qk = [_mm_nt(x, y) for x, y in zip(q, k)]
        yield
        t_inv = yield from _neumann_inverse(
            [jnp.where(strict, x * d, 0.0) for x, d in zip(kk, decay)], eye)
        eg = [jnp.exp(gc) for gc in g_c]
        uw = [_mm(ti, jnp.concatenate([x * bc, y * e], axis=1))
              for ti, x, bc, y, e in zip(t_inv, v, b_c, kb, eg)]
        yield
        kd = [x * jnp.exp(gr[:, CHUNK - 1:CHUNK] - gc) for x, gr, gc in zip(k, g_r, g_c)]
        kuw = [_mm_tn(x, y) for x, y in zip(kd, uw)]
        yield
        grid = lambda parts: jnp.concatenate(
            [cat(parts[c * GDN_HEADS:(c + 1) * GDN_HEADS]) for c in range(per_iter)], axis=0)
        lhs = [jnp.concatenate([m[:, hd:], x[:, hd:], y * e], axis=0)
               for m, x, y, e in zip(kuw, uw, q, eg)]
        lhs_s[chunk_rows(it, per_iter * lrows), :] = grid(lhs).astype(BF)
        ku_s[chunk_rows(it, per_iter * hd), :] = grid([m[:, :hd] for m in kuw])
        u_s[rows, :] = grid([x[:, :hd] for x in uw])
        qk_s[rows, :] = grid([x * d for x, d in zip(qk, decay)]).astype(BF)

    def advance(n):
        s_prev = [s_s[h] for h in heads]
        prod = [_mm(lhs_s[chunk_rows(n, lrows), hsl(h)], s_prev[h]) for h in heads]
        g_last = [grow_s[h, pl.ds(n, 1), :][:, CHUNK - 1:CHUNK] for h in heads]
        s_s[...] = jnp.stack([s_prev[h] * jnp.exp(g_last[h]) - prod[h][:hd]
                              + ku_s[chunk_rows(n, hd), hsl(h)] for h in heads])
        pend_s[...] = cat([prod[h][hd:] for h in heads])

    def emit(n):
        rows = chunk_rows(n, CHUNK)
        pend = pend_s[...]
        v_new = [u_s[rows, hsl(h)] - pend[:CHUNK, hsl(h)] for h in heads]
        qv = [_mm(qk_s[rows, h * CHUNK:(h + 1) * CHUNK], v_new[h]) for h in heads]
        outs = [_rms(pend[CHUNK:, hsl(h)] + qv[h], on_ref[...]) for h in heads]
        gate = _silu(gg_ref[rows, :].astype(F32))
        o_ref[rows, :] = (cat(outs) * gate).astype(o_ref.dtype)

    def state_work(first, has_previous):
        for c in range(per_iter):
            if has_previous or c:
                emit(first + c - 1)
                yield
            advance(first + c)
            yield

    def interleave(*gens):
        active = list(enumerate(gens))
        while active:
            for item in list(active):
                try:
                    for _ in range(2 if item[0] else 1):
                        next(item[1])
                except StopIteration:
                    active.remove(item)

    n_iter = nc // per_iter
    interleave(local_work(0))
    if n_iter > 1:
        interleave(local_work(1), state_work(0, False))

    def merged_step(it, carry):
        interleave(local_work(it), state_work((it - 1) * per_iter, True))
        return carry

    lax.fori_loop(2, n_iter, merged_step, 0)
    interleave(state_work((n_iter - 1) * per_iter, n_iter > 1))
    emit(nc - 1)


def _gdn(zb, ga_row, gb_row, a_log, dt_bias, o_norm, b, t):
    n = zb.shape[0]
    nc = t // CHUNK
    width = GDN_HEADS * GDN_HEAD_DIM
    fox_blocks = 3 * FOX_HEADS * FOX_HEAD_DIM // width
    colspec = lambda j: pl.BlockSpec((t, width), lambda bi: (bi, fox_blocks + j))
    gspec = lambda shp: pl.BlockSpec((1,) + shp, lambda bi: (bi, 0, 0, 0))
    return pl.pallas_call(
        _gdn_kernel,
        grid=(b,),
        in_specs=[_const_spec(a_log.shape), _const_spec(dt_bias.shape),
                  colspec(0), colspec(1), colspec(2), colspec(3),
                  gspec((GDN_HEADS, nc, CHUNK)), gspec((GDN_HEADS, nc, CHUNK)),
                  _const_spec(o_norm.shape)],
        out_specs=pl.BlockSpec((t, width), lambda bi: (bi, 0)),
        out_shape=jax.ShapeDtypeStruct((n, width), BF),
        scratch_shapes=[pltpu.VMEM((nc * (GDN_HEAD_DIM + 2 * CHUNK), width), BF),
                        pltpu.VMEM((nc * GDN_HEAD_DIM, width), F32),
                        pltpu.VMEM((t, width), F32),
                        pltpu.VMEM((t, GDN_HEADS * CHUNK), BF),
                        pltpu.VMEM((2 * CHUNK, width), F32),
                        pltpu.VMEM((GDN_HEADS, nc, CHUNK), F32),
                        pltpu.VMEM((GDN_HEADS, CHUNK, nc), F32),
                        pltpu.VMEM((GDN_HEADS, CHUNK, nc), F32),
                        pltpu.VMEM((GDN_HEADS, GDN_HEAD_DIM, GDN_HEAD_DIM), F32)],
        compiler_params=_cparams(("parallel",)),
        name="gated_deltanet",
    )(a_log, dt_bias, zb, zb, zb, zb, ga_row, gb_row, o_norm)


def _mlp_kernel(a1_ref, a2_ref, x_ref, wo1_ref, wo2_ref, g_ref, w1_ref, w2_ref, o_ref, acc_ref,
                *, fc):
    mix = (jnp.dot(a1_ref[...], wo1_ref[...], preferred_element_type=F32)
           + jnp.dot(a2_ref[...], wo2_ref[...], preferred_element_type=F32))
    x1 = x_ref[...] + mix
    h = _rms(x1, g_ref[...]).astype(BF)
    acc_ref[...] = x1
    for c in range(w1_ref.shape[1] // fc):
        a = jnp.maximum(jnp.dot(h, w1_ref[:, c * fc:(c + 1) * fc],
                                preferred_element_type=F32), 0.0)
        acc_ref[...] += jnp.dot((a * a).astype(BF), w2_ref[c * fc:(c + 1) * fc, :],
                                preferred_element_type=F32)
    o_ref[...] = acc_ref[...]


def _mlp(a1, a2, a2_block, x2, wo1, wo2, gain, w1, w2, tm):
    n, d = x2.shape
    half = wo1.shape[0]
    return pl.pallas_call(
        functools.partial(_mlp_kernel, fc=1024),
        grid=(n // tm,),
        in_specs=[pl.BlockSpec((tm, half), lambda i: (i, 0)),
                  pl.BlockSpec((tm, half), lambda i: (i, a2_block)),
                  pl.BlockSpec((tm, d), lambda i: (i, 0)),
                  _const_spec(wo1.shape), _const_spec(wo2.shape), _const_spec(gain.shape),
                  _const_spec(w1.shape), _const_spec(w2.shape)],
        out_specs=pl.BlockSpec((tm, d), lambda i: (i, 0)),
        out_shape=jax.ShapeDtypeStruct((n, d), F32),
        scratch_shapes=[pltpu.VMEM((tm, d), F32)],
        compiler_params=_cparams(("parallel",)),
        name="out_proj_mlp",
    )(a1, a2, x2, wo1, wo2, gain, w1, w2)


def _in1_kernel(x_ref, g_ref, wb_ref, wf_ref, zb_ref, zf_ref):
    h = _rms(x_ref[...], g_ref[...]).astype(BF)
    gw = 512
    for c in range(wb_ref.shape[1] // gw):
        zb_ref[:, c * gw:(c + 1) * gw] = jnp.dot(
            h, wb_ref[:, c * gw:(c + 1) * gw], preferred_element_type=F32).astype(BF)
    for c in range(wf_ref.shape[1] // gw):
        zf_ref[:, c * gw:(c + 1) * gw] = jnp.dot(
            h, wf_ref[:, c * gw:(c + 1) * gw], preferred_element_type=F32)


def _in1(x2, gain, wb, wf, tm):
    n, d = x2.shape
    return pl.pallas_call(
        _in1_kernel,
        grid=(n // tm,),
        in_specs=[pl.BlockSpec((tm, d), lambda i: (i, 0)),
                  _const_spec(gain.shape), _const_spec(wb.shape), _const_spec(wf.shape)],
        out_specs=[pl.BlockSpec((tm, wb.shape[1]), lambda i: (i, 0)),
                   pl.BlockSpec((tm, wf.shape[1]), lambda i: (i, 0))],
        out_shape=[jax.ShapeDtypeStruct((n, wb.shape[1]), BF),
                   jax.ShapeDtypeStruct((n, wf.shape[1]), F32)],
        compiler_params=_cparams(("parallel",)),
        name="l1_in_proj",
    )(x2, gain, wb, wf)


HG_GROUP = 4
HG_HEADS_PER_STEP = 1


HG_LEVELS = (32, 16, 8, 4, 2, 1)


def _hgrn_term_mask(terms):
    t = lax.broadcasted_iota(jnp.int32, (CHUNK, len(terms) * CHUNK), 0)
    col = lax.broadcasted_iota(jnp.int32, (CHUNK, len(terms) * CHUNK), 1)
    s = col % CHUNK
    mask = None
    for j, m in enumerate(terms):
        own = (t == s) if m == 0 else (
            (t % (2 * m) >= m) & (s % (2 * m) < m) & (t // (2 * m) == s // (2 * m)))
        own = own & (col // CHUNK == j)
        mask = own if mask is None else mask | own
    return mask


def _hgrn_level_operands(q, k, b, ref_row):
    d = q.shape[1]
    nb = CHUNK // SUBLANES
    blk = lambda x, i: x[SUBLANES * i:SUBLANES * (i + 1)]
    sub = lax.broadcasted_iota(jnp.int32, (SUBLANES, d), 0)
    xs = []
    for m in HG_LEVELS:
        parts = []
        for i in range(nb):
            bi, qi, ki = blk(b, i), blk(q, i), blk(k, i)
            row0 = SUBLANES * i
            if m >= SUBLANES:
                ref = row0 // (2 * m) * (2 * m) + m
                if row0 % (2 * m) >= m:
                    parts.append(qi * jnp.exp2(bi - ref_row(ref)))
                else:
                    parts.append(ki * jnp.exp2(ref_row(ref) - bi))
                continue
            upper = sub % (2 * m) >= m
            if m == 4:
                bref = ref_row(row0 + 4)
            elif m == 2:
                bref = jnp.where(sub < 4, ref_row(row0 + 2), ref_row(row0 + 6))
            else:
                bref = jnp.where(upper, bi, pltpu.roll(bi, SUBLANES - 1, 0))
            dlt = bi - bref
            parts.append(jnp.where(upper, qi, ki) * jnp.exp2(jnp.where(upper, dlt, -dlt)))
        xs.append(jnp.concatenate(parts, axis=0).astype(BF))
        yield
    return xs


def _hgrn_kernel(zq_ref, zi_ref, zg_ref, zf_ref, lbl_ref, on_ref, o_ref,
                 oi_s, qd_s, kv_s, dec_s, h1_s, *, group):
    t = zq_ref.shape[0]
    d = HGRN_HEAD_DIM
    heads = range(zq_ref.shape[1] // d)
    hs = lambda hh: slice(hh * d, (hh + 1) * d)
    gr = group * CHUNK
    ng = t // gr
    ri = lax.broadcasted_iota(jnp.int32, (gr, gr), 0)
    ci = lax.broadcasted_iota(jnp.int32, (gr, gr), 1)
    tril = ((ri >= ci) & (ri // CHUNK == ci // CHUNK)).astype(BF)
    term_mask = {m: _hgrn_term_mask((m,)) for m in (0,) + HG_LEVELS}
    sls =[slice(c * CHUNK, (c + 1) * CHUNK) for c in range(group)]

    l0 = lbl_ref[0:1, :]
    l1 = lbl_ref[1:2, :]
    mx = jnp.maximum(l0, l1)
    e0 = jnp.exp(l0 - mx)
    e1 = jnp.exp(l1 - mx)
    s0 = e0 / (e0 + e1)
    s1 = e1 / (e0 + e1)
    lb = (s0 + s1) - s0

    def group_rows(g):
        return pl.ds(g * gr, gr)

    def prelude(hh, g):
        rows = group_rows(g)
        qs, ks, ls = [], [], []
        for sl in sls:
            qs.append(_silu(zq_ref[rows, hs(hh)][sl].astype(F32)))
            f = lb[:, hs(hh)] + (1.0 - lb[:, hs(hh)]) * _sigmoid(zf_ref[rows, hs(hh)][sl])
            ks.append(1.0 - f)
            ls.append(jnp.log2(f))
            yield
        q, k = jnp.concatenate(qs, axis=0), jnp.concatenate(ks, axis=0)
        b = _mm_exact_lhs(tril, jnp.concatenate(ls, axis=0))
        h1_s[hh, g, 0] = b
        h1_s[hh, g, 1] = k
        h1_s[hh, g, 2] = q
        qd_s[hh, rows, :] = (q * jnp.exp2(b)).astype(BF)
        yield

    def operands(hh, g):
        b, k, q = h1_s[hh, g, 0], h1_s[hh, g, 1], h1_s[hh, g, 2]
        ops = []
        for c, sl in enumerate(sls):
            ref_row = lambda r, c=c: h1_s[hh, g, 0, c * CHUNK + r:c * CHUNK + r + 1, :]
            xs = yield from _hgrn_level_operands(q[sl], k[sl], b[sl], ref_row)
            b_last = b[sl][CHUNK - 1:CHUNK, :]
            kdec = (k[sl] * jnp.exp2(b_last - b[sl])).astype(BF)
            dec_s[hh, g * group + c] = jnp.exp2(b_last)
            ops.append((xs, q[sl].astype(BF), k[sl].astype(BF), kdec))
            yield
        return ops

    def products(hh, g, ops):
        rows = group_rows(g)
        v = zi_ref[rows, hs(hh)]
        dmat, kv, oi = [], [], []
        for (xs, qb, kb, kdec), sl in zip(ops, sls):
            dm = jnp.where(term_mask[0], _mm_nt(qb, kb), 0.0)
            yield
            for m, x in zip(HG_LEVELS, xs):
                dm = jnp.where(term_mask[m], _mm_nt(x, x), dm)
                yield
            kv.append(_mm_tn(v[sl], kdec))
            dmat.append(dm)
            yield
        for dm, sl in zip(dmat, sls):
            oi.append(_mm(dm, v[sl]))
            yield
        oi_s[hh, rows, :] = jnp.concatenate(oi, axis=0)
        kv_s[hh, pl.ds(g * group, group)] = jnp.stack(kv)

    def advance(*stages):
        results = [None] * len(stages)
        active = list(enumerate(stages))
        while active:
            for item in list(active):
                j, (gen, turns) = item
                try:
                    for _ in range(turns):
                        next(gen)
                except StopIteration as done:
                    results[j] = done.value
                    active.remove(item)
        return results

    for hh in heads:
        advance((prelude(hh, 0), 1), (prelude(hh, 1), 1))
        ops = advance((operands(hh, 0), 1))[0]
        for g in range(1, ng):
            stages = [(products(hh, g - 1, ops), 1), (operands(hh, g), 1)]
            if g + 1 < ng:
                stages.append((prelude(hh, g + 1), 1))
            ops = advance(*stages)[1]
        advance((products(hh, ng - 1, ops), 1))

    per_step = 4 * group
    sr = per_step * CHUNK

    def state_step(g, sts):
        r0 = pl.multiple_of(g * sr, sr)
        rows = pl.ds(r0, sr)
        new_sts, normed = [], []
        for hh, st in zip(heads, sts):
            states = []
            for c in range(per_step):
                n = g * per_step + c
                states.append(st)
                st = st * dec_s[hh, n] + kv_s[hh, n]
            new_sts.append(st)
            outs = [oi_s[hh, pl.ds(r0 + c * CHUNK, CHUNK), :]
                    + _mm_nt(qd_s[hh, pl.ds(r0 + c * CHUNK, CHUNK), :], states[c])
                    for c in range(per_step)]
            normed.append(_rms(jnp.concatenate(outs, axis=0), on_ref[...]))
        o = jnp.concatenate(normed, axis=1)
        o_ref[rows, :] = (o * _silu(zg_ref[rows, :].astype(F32))).astype(o_ref.dtype)
        return tuple(new_sts)

    lax.fori_loop(0, t // sr, state_step, tuple(jnp.zeros((d, d), F32) for _ in heads))


def _hgrn(zb, zf, lb_logits, o_norm, b, t):
    n = zb.shape[0]
    hd = HGRN_HEAD_DIM
    nh = HGRN_HEADS
    nc = t // CHUNK
    gr = HG_GROUP * CHUNK
    hps = HG_HEADS_PER_STEP
    ns = nh // hps
    wide = hps * hd
    return pl.pallas_call(
        functools.partial(_hgrn_kernel, group=HG_GROUP),
        grid=(b, ns),
        in_specs=[pl.BlockSpec((t, wide), lambda bi, h: (bi, h)),
                  pl.BlockSpec((t, wide), lambda bi, h: (bi, ns + h)),
                  pl.BlockSpec((t, wide), lambda bi, h: (bi, 2 * ns + h)),
                  pl.BlockSpec((t, wide), lambda bi, h: (bi, h)),
                  pl.BlockSpec((lb_logits.shape[0], wide), lambda bi, h: (0, h)),
                  _const_spec(o_norm.shape)],
        out_specs=pl.BlockSpec((t, wide), lambda bi, h: (bi, h)),
        out_shape=jax.ShapeDtypeStruct((n, nh * hd), BF),
        scratch_shapes=[pltpu.VMEM((hps, t, hd), F32), pltpu.VMEM((hps, t, hd), BF),
                        pltpu.VMEM((hps, nc, hd, hd), F32), pltpu.VMEM((hps, nc, 1, hd), F32),
                        pltpu.VMEM((hps, t // gr, 3, gr, hd), F32)],
        compiler_params=_cparams(("parallel", "parallel")),
        name="hgrn2",
    )(zb, zb, zb, zf, lb_logits, o_norm)


def kernel(x, l0_mix_norm, l0_w_in, l0_fox_q_norm, l0_fox_k_norm, l0_fox_f_bias, l0_gdn_conv,
           l0_gdn_A_log, l0_gdn_dt_bias, l0_gdn_o_norm, l0_w_out, l0_ffn_norm, l0_w_ff1,
           l0_w_ff2, l1_mix_norm, l1_w_in, l1_hgrn_o_norm, l1_w_out, l1_ffn_norm, l1_w_ff1,
           l1_w_ff2, hgrn_lb_logits):
    b, t, d = x.shape
    n = b * t
    fw = FOX_HEADS * FOX_HEAD_DIM
    gw = GDN_HEADS * GDN_HEAD_DIM
    assert hgrn_lb_logits.shape[0] == 2 and d == HGRN_HEADS * HGRN_HEAD_DIM
    assert t % (2 * HG_GROUP * CHUNK) == 0
    tm = 512 if t % 512 == 0 else 256
    tq = 256
    nc = t // CHUNK
    row = lambda p: p.reshape(1, -1).astype(F32)
    x2 = x.reshape(n, d)

    o_ff = 3 * fw
    o_gq = o_ff + FOX_HEADS
    o_gb = o_gq + 3 * gw
    o_gg = o_gb + 2 * GDN_HEADS
    wb0 = jnp.concatenate([l0_w_in[:, :o_ff], l0_w_in[:, o_gq:o_gb], l0_w_in[:, o_gg:]],
                          axis=1).astype(BF)
    n_small = FOX_HEADS + 2 * GDN_HEADS
    ws0 = jnp.concatenate([l0_w_in[:, o_ff:o_gq], l0_w_in[:, o_gb:o_gg],
                           jnp.zeros((d, LANES - n_small), F32)], axis=1).astype(BF)
    hid = jnp.arange(fw) // FOX_HEAD_DIM
    pm = jnp.where(hid[:, None] == hid[None, :], 1.0 / FOX_HEAD_DIM, 0.0).astype(BF)
    qkg = jnp.stack([jnp.tile(l0_fox_q_norm, FOX_HEADS) * (FOX_HEAD_DIM ** -0.5 * LOG2E),
                     jnp.tile(l0_fox_k_norm, FOX_HEADS)]).astype(F32)

    tm0 = 1024 if t % 1024 == 0 else tm
    zb0, zs0, zst0 = _in0(x2, row(l0_mix_norm), wb0, ws0, pm, qkg, l0_gdn_conv.astype(F32),
                          tm0, t)

    f_bias = jnp.pad(l0_fox_f_bias.astype(F32), (0, LANES - FOX_HEADS)).reshape(1, LANES)
    qbias, kbias = _fox_cum(zs0, f_bias, b, t)
    fox_o = _fox_attention(zb0, qbias, kbias, b, t, tq)

    def gate_rows(lo):
        g = zst0[lo:lo + GDN_HEADS].reshape(GDN_HEADS, b, nc, CHUNK)
        return g.transpose(1, 0, 2, 3)
    gb_row = gate_rows(FOX_HEADS)
    ga_row = gate_rows(FOX_HEADS + GDN_HEADS)
    gdn_o = _gdn(zb0, ga_row, gb_row,
                 l0_gdn_A_log.reshape(GDN_HEADS, 1, 1).astype(F32),
                 l0_gdn_dt_bias.reshape(GDN_HEADS, 1, 1).astype(F32), row(l0_gdn_o_norm), b, t)

    wo0 = l0_w_out.astype(BF)
    x1 = _mlp(fox_o, gdn_o, 0, x2, wo0[:fw], wo0[fw:], row(l0_ffn_norm),
              l0_w_ff1.astype(BF), l0_w_ff2.astype(BF), tm)

    wb1 = jnp.concatenate([l1_w_in[:, :d], l1_w_in[:, 2 * d:]], axis=1).astype(BF)
    wf1 = l1_w_in[:, d:2 * d].astype(BF)
    zb1, zf1 = _in1(x1, row(l1_mix_norm), wb1, wf1, tm0)
    ho = _hgrn(zb1, zf1, hgrn_lb_logits.astype(F32), row(l1_hgrn_o_norm), b, t)
    wo1 = l1_w_out.astype(BF)
    half = d // 2
    x3 = _mlp(ho, ho, 1, x1, wo1[:half], wo1[half:], row(l1_ffn_norm),
              l1_w_ff1.astype(BF), l1_w_ff2.astype(BF), tm)
    return x3.reshape(b, t, d)
```

```python
import functools

import jax
import jax.numpy as jnp
from jax import lax
from jax.experimental import pallas as pl
from jax.experimental.pallas import tpu as pltpu

BF = jnp.bfloat16
F32 = jnp.float32
NORM_EPS = 1e-6
LOG2E = 1.4426950408889634
CHUNK = 64
FOX_HEADS, FOX_HEAD_DIM = 8, 64
GDN_HEADS, GDN_HEAD_DIM = 4, 128
HGRN_HEADS, HGRN_HEAD_DIM = 8, 128
GDN_CONV = 4
IN0_ROWS = 128
LANES = 128
SUBLANES = 8
VMEM_LIMIT = 56 * 1024 * 1024


def _cparams(sem):
    return pltpu.CompilerParams(dimension_semantics=sem, vmem_limit_bytes=VMEM_LIMIT)


def _const_spec(shape):
    nd = len(shape)
    return pl.BlockSpec(shape, lambda *_: (0,) * nd, pipeline_mode=pl.Buffered(1))


def _mm(a, b):
    return jnp.dot(a.astype(BF), b.astype(BF), preferred_element_type=F32)


def _mm_nt(a, b):
    return lax.dot_general(a.astype(BF), b.astype(BF), (((1,), (1,)), ((), ())),
                           preferred_element_type=F32)


def _mm_tn(a, b):
    return lax.dot_general(a.astype(BF), b.astype(BF), (((0,), (0,)), ((), ())),
                           preferred_element_type=F32)


def _split3(x):
    hi = x.astype(BF)
    r1 = x - hi.astype(F32)
    mid = r1.astype(BF)
    lo = (r1 - mid.astype(F32)).astype(BF)
    return hi, mid, lo


def _mm_exact_lhs(m_bf, x):
    n = x.shape[1]
    if n % LANES:
        hi, mid, lo = _split3(x)
        d = lambda p: jnp.dot(m_bf, p, preferred_element_type=F32)
        return d(hi) + d(mid) + d(lo)
    y = jnp.dot(m_bf, jnp.concatenate(_split3(x), axis=1), preferred_element_type=F32)
    return y[:, :n] + y[:, n:2 * n] + y[:, 2 * n:]


def _mm_exact_rhs(x, m_bf):
    hi, mid, lo = _split3(x)
    d = lambda p: jnp.dot(p, m_bf, preferred_element_type=F32)
    return d(hi) + d(mid) + d(lo)


def _sigmoid(x):
    return 1.0 / (1.0 + jnp.exp(-x))


def _silu(x):
    return x * _sigmoid(x)


def _softplus(x):
    return jnp.maximum(x, 0.0) + jnp.log1p(jnp.exp(-jnp.abs(x)))


def _rms(x, gain):
    return x * lax.rsqrt(jnp.mean(x * x, axis=-1, keepdims=True) + NORM_EPS) * gain


def _in0_kernel(x_ref, g_ref, wb_ref, ws_ref, pm_ref, qkg_ref, cw_ref, zb_ref, zs_ref, zst_ref,
                hist_ref, *, tiles_per_seq):
    tm = x_ref.shape[0]
    hd = GDN_HEAD_DIM
    hrows = hist_ref.shape[1]

    @pl.when(pl.program_id(0) % tiles_per_seq == 0)
    def _():
        hist_ref[...] = jnp.zeros_like(hist_ref)

    h = _rms(x_ref[...], g_ref[...]).astype(BF)
    gw = 512
    rb = IN0_ROWS

    def project(c, r):
        return jnp.dot(h[r * rb:(r + 1) * rb], wb_ref[:, c * gw:(c + 1) * gw],
                       preferred_element_type=F32)

    def finish(c, r, z, above):
        if c < 2:
            ms = jnp.dot((z * z).astype(BF), pm_ref[...], preferred_element_type=F32)
            z = z * lax.rsqrt(ms + NORM_EPS) * qkg_ref[c:c + 1, :]
        elif 3 <= c < 6:
            j = c - 3
            w = cw_ref[:, j * gw:(j + 1) * gw]
            ext = jnp.concatenate([above, z], axis=0)
            y = z * w[GDN_CONV - 1:GDN_CONV]
            for s in range(1, GDN_CONV):
                y = y + ext[hrows - s:hrows - s + rb] * w[GDN_CONV - 1 - s:GDN_CONV - s]
            z = _silu(y)
            if j < 2:
                scale = 1.0 if j else hd ** -0.5
                z = jnp.concatenate(
                    [zh * (lax.rsqrt(jnp.sum(zh * zh, axis=-1, keepdims=True) + NORM_EPS) * scale)
                     for zh in (z[:, k * hd:(k + 1) * hd] for k in range(gw // hd))], axis=1)
        zb_ref[r * rb:(r + 1) * rb, c * gw:(c + 1) * gw] = z.astype(BF)

    pending = None
    for c in range(wb_ref.shape[1] // gw):
        conv = 3 <= c < 6
        above = hist_ref[c - 3] if conv else None
        for r in range(tm // rb):
            z = project(c, r)
            if pending is not None:
                finish(*pending)
            pending = (c, r, z, above)
            above = z[rb - hrows:, :] if conv else None
        if conv:
            hist_ref[c - 3] = above
    finish(*pending)
    zs = jnp.dot(h, ws_ref[...], preferred_element_type=F32)
    zs_ref[...] = zs
    zst_ref[...] = zs.T


def _in0(x2, gain, wb, ws, pm, qkg, conv_w, tm, t):
    n, d = x2.shape
    nb = wb.shape[1]
    return pl.pallas_call(
        functools.partial(_in0_kernel, tiles_per_seq=t // tm),
        grid=(n // tm,),
        in_specs=[pl.BlockSpec((tm, d), lambda i: (i, 0)),
                  _const_spec((1, d)), _const_spec(wb.shape), _const_spec(ws.shape),
                  _const_spec(pm.shape), _const_spec(qkg.shape), _const_spec(conv_w.shape)],
        out_specs=[pl.BlockSpec((tm, nb), lambda i: (i, 0)),
                   pl.BlockSpec((tm, LANES), lambda i: (i, 0)),
                   pl.BlockSpec((LANES, tm), lambda i: (0, i))],
        out_shape=[jax.ShapeDtypeStruct((n, nb), BF), jax.ShapeDtypeStruct((n, LANES), F32),
                   jax.ShapeDtypeStruct((LANES, n), F32)],
        scratch_shapes=[pltpu.VMEM((3, SUBLANES, 512), F32)],
        compiler_params=_cparams(("arbitrary",)),
        name="l0_in_proj",
    )(x2, gain, wb, ws, pm, qkg, conv_w)


FOX_BIAS_LANES = 8


def _fox_cum_kernel(zs_ref, bias_ref, qb_ref, kb_ref, *, blk):
    t = zs_ref.shape[0]
    r = lax.broadcasted_iota(jnp.int32, (blk, blk), 0)
    c = lax.broadcasted_iota(jnp.int32, (blk, blk), 1)
    tril = (r >= c).astype(BF)
    pr = lax.broadcasted_iota(jnp.int32, (3 * LANES, 2 * LANES), 0)
    pc = lax.broadcasted_iota(jnp.int32, (3 * LANES, 2 * LANES), 1)
    place = ((pr % LANES < FOX_HEADS)
             & (pc % LANES == (pr % LANES) * FOX_BIAS_LANES + pr // LANES + 3 * (pc // LANES))
             ).astype(BF)
    lane = lax.broadcasted_iota(jnp.int32, (1, LANES), 1)
    slot = lane % FOX_BIAS_LANES
    used = lane < FOX_HEADS * FOX_BIAS_LANES
    ones_q = jnp.where(used & (slot >= 3) & (slot < 6), 1.0, 0.0)
    ones_k = jnp.where(used & (slot < 3), 1.0, 0.0)
    blocks = [slice(i * blk, (i + 1) * blk) for i in range(t // blk)]
    local = []
    for rows in blocks:
        z = zs_ref[rows, :] + bias_ref[...]
        ls = jnp.minimum(z, 0.0) - jnp.log1p(jnp.exp(-jnp.abs(z)))
        local.append(_mm_exact_lhs(tril, ls))
    carry = jnp.zeros((1, LANES), F32)
    for rows, loc in zip(blocks, local):
        cb = loc + carry
        carry = cb[blk - 1:blk, :]
        feat = jnp.dot(jnp.concatenate(_split3(cb * LOG2E), axis=1), place,
                       preferred_element_type=F32)
        qb_ref[rows, :] = (feat[:, :LANES] + ones_q).astype(BF)
        kb_ref[rows, :] = (ones_k - feat[:, LANES:]).astype(BF)


def _fox_cum(zs, bias_row, b, t):
    n = zs.shape[0]
    spec = pl.BlockSpec((t, LANES), lambda bi: (bi, 0))
    return pl.pallas_call(
        functools.partial(_fox_cum_kernel, blk=256),
        grid=(b,),
        in_specs=[spec, _const_spec(bias_row.shape)],
        out_specs=[spec, spec],
        out_shape=[jax.ShapeDtypeStruct((n, LANES), BF)] * 2,
        compiler_params=_cparams(("parallel",)),
        name="fox_cum_gate",
    )(zs, bias_row)


def _fox_kernel(q_ref, k_ref, v_ref, qb_ref, kb_ref, o_ref, *, tq):
    pair = pl.program_id(1)
    t = q_ref.shape[0]
    lane = lax.broadcasted_iota(jnp.int32, (1, LANES), 1)
    zero = jnp.zeros((tq, LANES), BF)
    first = lane < FOX_HEAD_DIM
    bias_head = lane // FOX_BIAS_LANES
    ones = jnp.ones((tq, LANES), BF)
    on_diag = (lax.broadcasted_iota(jnp.int32, (2 * tq, tq), 1)
               <= lax.broadcasted_iota(jnp.int32, (2 * tq, tq), 0) % tq)

    def scores(i):
        rows_i = slice(i * tq, (i + 1) * tq)
        kv = (i + 1) * tq
        q2, qb = q_ref[rows_i, :], qb_ref[rows_i, :]
        rows = []
        for h in range(2):
            qh = jnp.where(first if h == 0 else jnp.logical_not(first), q2, zero)
            qbh = jnp.where(bias_head == 2 * pair + h, qb, zero)
            rows.append(jnp.concatenate([qh, qbh], axis=1))
        ka = jnp.concatenate([k_ref[:kv, :], kb_ref[:kv, :]], axis=1)
        s = lax.dot_general(jnp.concatenate(rows, axis=0), ka, (((1,), (1,)), ((), ())),
                            preferred_element_type=F32)
        diag = jnp.where(on_diag, s[:, kv - tq:], -jnp.inf)
        return diag if i == 0 else jnp.concatenate([s[:, :kv - tq], diag], axis=1)

    def finish(i, s):
        kv = (i + 1) * tq
        p = jnp.exp2(s - jnp.max(s, axis=-1, keepdims=True)).astype(BF)
        va = jnp.concatenate([v_ref[:kv, :], jnp.concatenate([ones] * (i + 1), axis=0)], axis=1)
        acc = jnp.dot(p, va, preferred_element_type=F32)
        o = acc[:, :LANES] / acc[:, LANES:]
        o_ref[i * tq:(i + 1) * tq, :] = jnp.where(first, o[:tq], o[tq:]).astype(o_ref.dtype)

    s = scores(0)
    for i in range(t // tq):
        s_next = scores(i + 1) if (i + 1) * tq < t else None
        finish(i, s)
        s = s_next


def _fox_attention(zb, qbias, kbias, b, t, tq):
    n = zb.shape[0]
    hp = FOX_HEADS // 2
    return pl.pallas_call(
        functools.partial(_fox_kernel, tq=tq),
        grid=(b, hp),
        in_specs=[pl.BlockSpec((t, LANES), lambda bi, p: (bi, p)),
                  pl.BlockSpec((t, LANES), lambda bi, p: (bi, hp + p)),
                  pl.BlockSpec((t, LANES), lambda bi, p: (bi, 2 * hp + p)),
                  pl.BlockSpec((t, LANES), lambda bi, p: (bi, 0)),
                  pl.BlockSpec((t, LANES), lambda bi, p: (bi, 0))],
        out_specs=pl.BlockSpec((t, LANES), lambda bi, p: (bi, p)),
        out_shape=jax.ShapeDtypeStruct((n, FOX_HEADS * FOX_HEAD_DIM), BF),
        compiler_params=_cparams(("parallel", "parallel")),
        name="fox_attention",
    )(zb, zb, zb, qbias, kbias)


def _neumann_inverse(mats, eye):
    ps = [eye - a for a in mats]
    aps = list(mats)
    k = 2
    while k < CHUNK:
        aps = [_mm(ap, ap) for ap in aps]
        yield
        ps = [p + _mm(p, ap) for p, ap in zip(ps, aps)]
        yield
        k *= 2
    return ps


def _gdn_kernel(alog_ref, dtb_ref, q_ref, k_ref, v_ref, gg_ref, gar_ref, gbr_ref,
                on_ref, o_ref, lhs_s, ku_s, u_s, qk_s, pend_s, grow_s, gcol_s, bcol_s, s_s):
    t = q_ref.shape[0]
    nc = t // CHUNK
    hd = GDN_HEAD_DIM
    lrows = hd + 2 * CHUNK

    ri = lax.broadcasted_iota(jnp.int32, (CHUNK, CHUNK), 0)
    ci = lax.broadcasted_iota(jnp.int32, (CHUNK, CHUNK), 1)
    lower = ri >= ci
    strict = ri > ci
    triu = (ri <= ci).astype(BF)
    eye = (ri == ci).astype(F32)
    eye_bf = eye.astype(BF)

    def transposed(x):
        return sum(lax.dot_general(eye_bf, p, (((1,), (1,)), ((), ())),
                                   preferred_element_type=F32) for p in _split3(x))

    for h in range(GDN_HEADS):
        a = -jnp.exp(alog_ref[h])
        g_row = _mm_exact_rhs(a * _softplus(gar_ref[0, h] + dtb_ref[h]), triu)
        grow_s[h] = g_row
        gcol_s[h] = transposed(g_row)
        bcol_s[h] = transposed(_sigmoid(gbr_ref[0, h]))
    s_s[...] = jnp.zeros_like(s_s)

    lane_nc = lax.broadcasted_iota(jnp.int32, (CHUNK, nc), 1)
    heads = range(GDN_HEADS)
    cat = lambda parts: jnp.concatenate(parts, axis=1)

    per_iter = 4

    hsl = lambda h: slice(h * hd, (h + 1) * hd)

    def chunk_rows(n, size):
        return pl.ds(n * size if isinstance(n, int) else pl.multiple_of(n * size, size), size)

    def local_work(it):
        rows = chunk_rows(it, per_iter * CHUNK)
        chains = [(c, h) for c in range(per_iter) for h in heads]
        csl = lambda c: slice(c * CHUNK, (c + 1) * CHUNK)
        q_all, k_all, v_all = (r[rows, :].astype(F32) for r in (q_ref, k_ref, v_ref))
        q = [q_all[csl(c), hsl(h)] for c, h in chains]
        k = [k_all[csl(c), hsl(h)] for c, h in chains]
        v = [v_all[csl(c), hsl(h)] for c, h in chains]
        sel = [lane_nc == it * per_iter + c for c in range(per_iter)]
        g_c = [jnp.sum(jnp.where(sel[c], gcol_s[h], 0.0), axis=1, keepdims=True)
               for c, h in chains]
        b_c = [jnp.sum(jnp.where(sel[c], bcol_s[h], 0.0), axis=1, keepdims=True)
               for c, h in chains]
        g_r = [grow_s[h, pl.ds(it * per_iter + c, 1), :] for c, h in chains]
        yield
        decay = [jnp.exp(jnp.where(lower, gc - gr, -jnp.inf)) for gc, gr in zip(g_c, g_r)]
        kb = [x * bc for x, bc in zip(k, b_c)]
        kk = [_mm_nt(x, y) for x, y in zip(kb, k)]
        yield
        qk = [_mm_nt(x, y) for x, y in zip(q, k)]
        yield
        t_inv = yield from _neumann_inverse(
            [jnp.where(strict, x * d, 0.0) for x, d in zip(kk, decay)], eye)
        eg = [jnp.exp(gc) for gc in g_c]
        uw = [_mm(ti, jnp.concatenate([x * bc, y * e], axis=1))
              for ti, x, bc, y, e in zip(t_inv, v, b_c, kb, eg)]
        yield
        kd = [x * jnp.exp(gr[:, CHUNK - 1:CHUNK] - gc) for x, gr, gc in zip(k, g_r, g_c)]
        kuw = [_mm_tn(x, y) for x, y in zip(kd, uw)]
        yield
        grid = lambda parts: jnp.concatenate(
            [cat(parts[c * GDN_HEADS:(c + 1) * GDN_HEADS]) for c in range(per_iter)], axis=0)
        lhs = [jnp.concatenate([m[:, hd:], x[:, hd:], y * e], axis=0)
               for m, x, y, e in zip(kuw, uw, q, eg)]
        lhs_s[chunk_rows(it, per_iter * lrows), :] = grid(lhs).astype(BF)
        ku_s[chunk_rows(it, per_iter * hd), :] = grid([m[:, :hd] for m in kuw])
        u_s[rows, :] = grid([x[:, :hd] for x in uw])
        qk_s[rows, :] = grid([x * d for x, d in zip(qk, decay)]).astype(BF)

    def advance(n):
        s_prev = [s_s[h] for h in heads]
        prod = [_mm(lhs_s[chunk_rows(n, lrows), hsl(h)], s_prev[h]) for h in heads]
        g_last = [grow_s[h, pl.ds(n, 1), :][:, CHUNK - 1:CHUNK] for h in heads]
        s_s[...] = jnp.stack([s_prev[h] * jnp.exp(g_last[h]) - prod[h][:hd]
                              + ku_s[chunk_rows(n, hd), hsl(h)] for h in heads])
        pend_s[...] = cat([prod[h][hd:] for h in heads])

    def emit(n):
        rows = chunk_rows(n, CHUNK)
        pend = pend_s[...]
        v_new = [u_s[rows, hsl(h)] - pend[:CHUNK, hsl(h)] for h in heads]
        qv = [_mm(qk_s[rows, h * CHUNK:(h + 1) * CHUNK], v_new[h]) for h in heads]
        outs = [_rms(pend[CHUNK:, hsl(h)] + qv[h], on_ref[...]) for h in heads]
        gate = _silu(gg_ref[rows, :].astype(F32))
        o_ref[rows, :] = (cat(outs) * gate).astype(o_ref.dtype)

    def state_work(first, has_previous):
        for c in range(per_iter):
            if has_previous or c:
                emit(first + c - 1)
                yield
            advance(first + c)
            yield

    def interleave(*gens):
        active = list(enumerate(gens))
        while active:
            for item in list(active):
                try:
                    for _ in range(2 if item[0] else 1):
                        next(item[1])
                except StopIteration:
                    active.remove(item)

    n_iter = nc // per_iter
    interleave(local_work(0))
    if n_iter > 1:
        interleave(local_work(1), state_work(0, False))

    def merged_step(it, carry):
        interleave(local_work(it), state_work((it - 1) * per_iter, True))
        return carry

    lax.fori_loop(2, n_iter, merged_step, 0)
    interleave(state_work((n_iter - 1) * per_iter, n_iter > 1))
    emit(nc - 1)


def _gdn(zb, ga_row, gb_row, a_log, dt_bias, o_norm, b, t):
    n = zb.shape[0]
    nc = t // CHUNK
    width = GDN_HEADS * GDN_HEAD_DIM
    fox_blocks = 3 * FOX_HEADS * FOX_HEAD_DIM // width
    colspec = lambda j: pl.BlockSpec((t, width), lambda bi: (bi, fox_blocks + j))
    gspec = lambda shp: pl.BlockSpec((1,) + shp, lambda bi: (bi, 0, 0, 0))
    return pl.pallas_call(
        _gdn_kernel,
        grid=(b,),
        in_specs=[_const_spec(a_log.shape), _const_spec(dt_bias.shape),
                  colspec(0), colspec(1), colspec(2), colspec(3),
                  gspec((GDN_HEADS, nc, CHUNK)), gspec((GDN_HEADS, nc, CHUNK)),
                  _const_spec(o_norm.shape)],
        out_specs=pl.BlockSpec((t, width), lambda bi: (bi, 0)),
        out_shape=jax.ShapeDtypeStruct((n, width), BF),
        scratch_shapes=[pltpu.VMEM((nc * (GDN_HEAD_DIM + 2 * CHUNK), width), BF),
                        pltpu.VMEM((nc * GDN_HEAD_DIM, width), F32),
                        pltpu.VMEM((t, width), F32),
                        pltpu.VMEM((t, GDN_HEADS * CHUNK), BF),
                        pltpu.VMEM((2 * CHUNK, width), F32),
                        pltpu.VMEM((GDN_HEADS, nc, CHUNK), F32),
                        pltpu.VMEM((GDN_HEADS, CHUNK, nc), F32),
                        pltpu.VMEM((GDN_HEADS, CHUNK, nc), F32),
                        pltpu.VMEM((GDN_HEADS, GDN_HEAD_DIM, GDN_HEAD_DIM), F32)],
        compiler_params=_cparams(("parallel",)),
        name="gated_deltanet",
    )(a_log, dt_bias, zb, zb, zb, zb, ga_row, gb_row, o_norm)


def _mlp_kernel(a1_ref, a2_ref, x_ref, wo1_ref, wo2_ref, g_ref, w1_ref, w2_ref, o_ref, acc_ref,
                *, fc):
    mix = (jnp.dot(a1_ref[...], wo1_ref[...], preferred_element_type=F32)
           + jnp.dot(a2_ref[...], wo2_ref[...], preferred_element_type=F32))
    x1 = x_ref[...] + mix
    h = _rms(x1, g_ref[...]).astype(BF)
    acc_ref[...] = x1
    for c in range(w1_ref.shape[1] // fc):
        a = jnp.maximum(jnp.dot(h, w1_ref[:, c * fc:(c + 1) * fc],
                                preferred_element_type=F32), 0.0)
        acc_ref[...] += jnp.dot((a * a).astype(BF), w2_ref[c * fc:(c + 1) * fc, :],
                                preferred_element_type=F32)
    o_ref[...] = acc_ref[...]


def _mlp(a1, a2, a2_block, x2, wo1, wo2, gain, w1, w2, tm):
    n, d = x2.shape
    half = wo1.shape[0]
    return pl.pallas_call(
        functools.partial(_mlp_kernel, fc=1024),
        grid=(n // tm,),
        in_specs=[pl.BlockSpec((tm, half), lambda i: (i, 0)),
                  pl.BlockSpec((tm, half), lambda i: (i, a2_block)),
                  pl.BlockSpec((tm, d), lambda i: (i, 0)),
                  _const_spec(wo1.shape), _const_spec(wo2.shape), _const_spec(gain.shape),
                  _const_spec(w1.shape), _const_spec(w2.shape)],
        out_specs=pl.BlockSpec((tm, d), lambda i: (i, 0)),
        out_shape=jax.ShapeDtypeStruct((n, d), F32),
        scratch_shapes=[pltpu.VMEM((tm, d), F32)],
        compiler_params=_cparams(("parallel",)),
        name="out_proj_mlp",
    )(a1, a2, x2, wo1, wo2, gain, w1, w2)


def _in1_kernel(x_ref, g_ref, wb_ref, wf_ref, zb_ref, zf_ref):
    h = _rms(x_ref[...], g_ref[...]).astype(BF)
    gw = 512
    for c in range(wb_ref.shape[1] // gw):
        zb_ref[:, c * gw:(c + 1) * gw] = jnp.dot(
            h, wb_ref[:, c * gw:(c + 1) * gw], preferred_element_type=F32).astype(BF)
    for c in range(wf_ref.shape[1] // gw):
        zf_ref[:, c * gw:(c + 1) * gw] = jnp.dot(
            h, wf_ref[:, c * gw:(c + 1) * gw], preferred_element_type=F32)


def _in1(x2, gain, wb, wf, tm):
    n, d = x2.shape
    return pl.pallas_call(
        _in1_kernel,
        grid=(n // tm,),
        in_specs=[pl.BlockSpec((tm, d), lambda i: (i, 0)),
                  _const_spec(gain.shape), _const_spec(wb.shape), _const_spec(wf.shape)],
        out_specs=[pl.BlockSpec((tm, wb.shape[1]), lambda i: (i, 0)),
                   pl.BlockSpec((tm, wf.shape[1]), lambda i: (i, 0))],
        out_shape=[jax.ShapeDtypeStruct((n, wb.shape[1]), BF),
                   jax.ShapeDtypeStruct((n, wf.shape[1]), F32)],
        compiler_params=_cparams(("parallel",)),
        name="l1_in_proj",
    )(x2, gain, wb, wf)


HG_GROUP = 4
HG_HEADS_PER_STEP = 2


HG_LEVELS = (32, 16, 8, 4, 2, 1)


def _hgrn_term_mask(terms):
    t = lax.broadcasted_iota(jnp.int32, (CHUNK, len(terms) * CHUNK), 0)
    col = lax.broadcasted_iota(jnp.int32, (CHUNK, len(terms) * CHUNK), 1)
    s = col % CHUNK
    mask = None
    for j, m in enumerate(terms):
        own = (t == s) if m == 0 else (
            (t % (2 * m) >= m) & (s % (2 * m) < m) & (t // (2 * m) == s // (2 * m)))
        own = own & (col // CHUNK == j)
        mask = own if mask is None else mask | own
    return mask


def _hgrn_level_operands(q, k, b, ref_row):
    d = q.shape[1]
    nb = CHUNK // SUBLANES
    blk = lambda x, i: x[SUBLANES * i:SUBLANES * (i + 1)]
    sub = lax.broadcasted_iota(jnp.int32, (SUBLANES, d), 0)
    xs = []
    for m in HG_LEVELS:
        parts = []
        for i in range(nb):
            bi, qi, ki = blk(b, i), blk(q, i), blk(k, i)
            row0 = SUBLANES * i
            if m >= SUBLANES:
                ref = row0 // (2 * m) * (2 * m) + m
                if row0 % (2 * m) >= m:
                    parts.append(qi * jnp.exp2(bi - ref_row(ref)))
                else:
                    parts.append(ki * jnp.exp2(ref_row(ref) - bi))
                continue
            upper = sub % (2 * m) >= m
            if m == 4:
                bref = ref_row(row0 + 4)
            elif m == 2:
                bref = jnp.where(sub < 4, ref_row(row0 + 2), ref_row(row0 + 6))
            else:
                bref = jnp.where(upper, bi, pltpu.roll(bi, SUBLANES - 1, 0))
            dlt = bi - bref
            parts.append(jnp.where(upper, qi, ki) * jnp.exp2(jnp.where(upper, dlt, -dlt)))
        xs.append(jnp.concatenate(parts, axis=0).astype(BF))
        yield
    return xs


def _hgrn_kernel(zq_ref, zi_ref, zg_ref, zf_ref, lbl_ref, on_ref, o_ref,
                 oi_s, qd_s, kv_s, dec_s, h1_s, *, group):
    t = zq_ref.shape[0]
    d = HGRN_HEAD_DIM
    heads = range(zq_ref.shape[1] // d)
    hs = lambda hh: slice(hh * d, (hh + 1) * d)
    gr = group * CHUNK
    ng = t // gr
    ri = lax.broadcasted_iota(jnp.int32, (gr, gr), 0)
    ci = lax.broadcasted_iota(jnp.int32, (gr, gr), 1)
    tril = ((ri >= ci) & (ri // CHUNK == ci // CHUNK)).astype(BF)
    term_mask = {m: _hgrn_term_mask((m,)) for m in (0,) + HG_LEVELS}
    sls =[slice(c * CHUNK, (c + 1) * CHUNK) for c in range(group)]

    l0 = lbl_ref[0:1, :]
    l1 = lbl_ref[1:2, :]
    mx = jnp.maximum(l0, l1)
    e0 = jnp.exp(l0 - mx)
    e1 = jnp.exp(l1 - mx)
    s0 = e0 / (e0 + e1)
    s1 = e1 / (e0 + e1)
    lb = (s0 + s1) - s0

    def group_rows(g):
        return pl.ds(g * gr, gr)

    def prelude(hh, g):
        rows = group_rows(g)
        qs, ks, ls = [], [], []
        for sl in sls:
            qs.append(_silu(zq_ref[rows, hs(hh)][sl].astype(F32)))
            f = lb[:, hs(hh)] + (1.0 - lb[:, hs(hh)]) * _sigmoid(zf_ref[rows, hs(hh)][sl])
            ks.append(1.0 - f)
            ls.append(jnp.log2(f))
            yield
        q, k = jnp.concatenate(qs, axis=0), jnp.concatenate(ks, axis=0)
        b = _mm_exact_lhs(tril, jnp.concatenate(ls, axis=0))
        h1_s[hh, g, 0] = b
        h1_s[hh, g, 1] = k
        h1_s[hh, g, 2] = q
        qd_s[hh, rows, :] = (q * jnp.exp2(b)).astype(BF)
        yield

    def operands(hh, g):
        b, k, q = h1_s[hh, g, 0], h1_s[hh, g, 1], h1_s[hh, g, 2]
        ops = []
        for c, sl in enumerate(sls):
            ref_row = lambda r, c=c: h1_s[hh, g, 0, c * CHUNK + r:c * CHUNK + r + 1, :]
            xs = yield from _hgrn_level_operands(q[sl], k[sl], b[sl], ref_row)
            b_last = b[sl][CHUNK - 1:CHUNK, :]
            kdec = (k[sl] * jnp.exp2(b_last - b[sl])).astype(BF)
            dec_s[hh, g * group + c] = jnp.exp2(b_last)
            ops.append((xs, q[sl].astype(BF), k[sl].astype(BF), kdec))
            yield
        return ops

    def products(hh, g, ops):
        rows = group_rows(g)
        v = zi_ref[rows, hs(hh)]
        dmat, kv, oi = [], [], []
        for (xs, qb, kb, kdec), sl in zip(ops, sls):
            dm = jnp.where(term_mask[0], _mm_nt(qb, kb), 0.0)
            yield
            for m, x in zip(HG_LEVELS, xs):
                dm = jnp.where(term_mask[m], _mm_nt(x, x), dm)
                yield
            kv.append(_mm_tn(v[sl], kdec))
            dmat.append(dm)
            yield
        for dm, sl in zip(dmat, sls):
            oi.append(_mm(dm, v[sl]))
            yield
        oi_s[hh, rows, :] = jnp.concatenate(oi, axis=0)
        kv_s[hh, pl.ds(g * group, group)] = jnp.stack(kv)

    def advance(*stages):
        results = [None] * len(stages)
        active = list(enumerate(stages))
        while active:
            for item in list(active):
                j, (gen, turns) = item
                try:
                    for _ in range(turns):
                        next(gen)
                except StopIteration as done:
                    results[j] = done.value
                    active.remove(item)
        return results

    for hh in heads:
        advance((prelude(hh, 0), 1), (prelude(hh, 1), 1))
        ops = advance((operands(hh, 0), 1))[0]
        for g in range(1, ng):
            stages = [(products(hh, g - 1, ops), 1), (operands(hh, g), 1)]
            if g + 1 < ng:
                stages.append((prelude(hh, g + 1), 1))
            ops = advance(*stages)[1]
        advance((products(hh, ng - 1, ops), 1))

    per_step = 4 * group
    sr = per_step * CHUNK

    def state_step(g, sts):
        r0 = pl.multiple_of(g * sr, sr)
        rows = pl.ds(r0, sr)
        new_sts, normed = [], []
        for hh, st in zip(heads, sts):
            states = []
            for c in range(per_step):
                n = g * per_step + c
                states.append(st)
                st = st * dec_s[hh, n] + kv_s[hh, n]
            new_sts.append(st)
            outs = [oi_s[hh, pl.ds(r0 + c * CHUNK, CHUNK), :]
                    + _mm_nt(qd_s[hh, pl.ds(r0 + c * CHUNK, CHUNK), :], states[c])
                    for c in range(per_step)]
            normed.append(_rms(jnp.concatenate(outs, axis=0), on_ref[...]))
        o = jnp.concatenate(normed, axis=1)
        o_ref[rows, :] = (o * _silu(zg_ref[rows, :].astype(F32))).astype(o_ref.dtype)
        return tuple(new_sts)

    lax.fori_loop(0, t // sr, state_step, tuple(jnp.zeros((d, d), F32) for _ in heads))


def _hgrn(zb, zf, lb_logits, o_norm, b, t):
    n = zb.shape[0]
    hd = HGRN_HEAD_DIM
    nh = HGRN_HEADS
    nc = t // CHUNK
    gr = HG_GROUP * CHUNK
    hps = HG_HEADS_PER_STEP
    ns = nh // hps
    wide = hps * hd
    return pl.pallas_call(
        functools.partial(_hgrn_kernel, group=HG_GROUP),
        grid=(b, ns),
        in_specs=[pl.BlockSpec((t, wide), lambda bi, h: (bi, h)),
                  pl.BlockSpec((t, wide), lambda bi, h: (bi, ns + h)),
                  pl.BlockSpec((t, wide), lambda bi, h: (bi, 2 * ns + h)),
                  pl.BlockSpec((t, wide), lambda bi, h: (bi, h)),
                  pl.BlockSpec((lb_logits.shape[0], wide), lambda bi, h: (0, h)),
                  _const_spec(o_norm.shape)],
        out_specs=pl.BlockSpec((t, wide), lambda bi, h: (bi, h)),
        out_shape=jax.ShapeDtypeStruct((n, nh * hd), BF),
        scratch_shapes=[pltpu.VMEM((hps, t, hd), F32), pltpu.VMEM((hps, t, hd), BF),
                        pltpu.VMEM((hps, nc, hd, hd), F32), pltpu.VMEM((hps, nc, 1, hd), F32),
                        pltpu.VMEM((hps, t // gr, 3, gr, hd), F32)],
        compiler_params=_cparams(("parallel", "parallel")),
        name="hgrn2",
    )(zb, zb, zb, zf, lb_logits, o_norm)


def kernel(x, l0_mix_norm, l0_w_in, l0_fox_q_norm, l0_fox_k_norm, l0_fox_f_bias, l0_gdn_conv,
           l0_gdn_A_log, l0_gdn_dt_bias, l0_gdn_o_norm, l0_w_out, l0_ffn_norm, l0_w_ff1,
           l0_w_ff2, l1_mix_norm, l1_w_in, l1_hgrn_o_norm, l1_w_out, l1_ffn_norm, l1_w_ff1,
           l1_w_ff2, hgrn_lb_logits):
    b, t, d = x.shape
    n = b * t
    fw = FOX_HEADS * FOX_HEAD_DIM
    gw = GDN_HEADS * GDN_HEAD_DIM
    assert hgrn_lb_logits.shape[0] == 2 and d == HGRN_HEADS * HGRN_HEAD_DIM
    assert t % (2 * HG_GROUP * CHUNK) == 0
    tm = 512 if t % 512 == 0 else 256
    tq = 256
    nc = t // CHUNK
    row = lambda p: p.reshape(1, -1).astype(F32)
    x2 = x.reshape(n, d)

    o_ff = 3 * fw
    o_gq = o_ff + FOX_HEADS
    o_gb = o_gq + 3 * gw
    o_gg = o_gb + 2 * GDN_HEADS
    wb0 = jnp.concatenate([l0_w_in[:, :o_ff], l0_w_in[:, o_gq:o_gb], l0_w_in[:, o_gg:]],
                          axis=1).astype(BF)
    n_small = FOX_HEADS + 2 * GDN_HEADS
    ws0 = jnp.concatenate([l0_w_in[:, o_ff:o_gq], l0_w_in[:, o_gb:o_gg],
                           jnp.zeros((d, LANES - n_small), F32)], axis=1).astype(BF)
    hid = jnp.arange(fw) // FOX_HEAD_DIM
    pm = jnp.where(hid[:, None] == hid[None, :], 1.0 / FOX_HEAD_DIM, 0.0).astype(BF)
    qkg = jnp.stack([jnp.tile(l0_fox_q_norm, FOX_HEADS) * (FOX_HEAD_DIM ** -0.5 * LOG2E),
                     jnp.tile(l0_fox_k_norm, FOX_HEADS)]).astype(F32)

    tm0 = 1024 if t % 1024 == 0 else tm
    zb0, zs0, zst0 = _in0(x2, row(l0_mix_norm), wb0, ws0, pm, qkg, l0_gdn_conv.astype(F32),
                          tm0, t)

    f_bias = jnp.pad(l0_fox_f_bias.astype(F32), (0, LANES - FOX_HEADS)).reshape(1, LANES)
    qbias, kbias = _fox_cum(zs0, f_bias, b, t)
    fox_o = _fox_attention(zb0, qbias, kbias, b, t, tq)

    def gate_rows(lo):
        g = zst0[lo:lo + GDN_HEADS].reshape(GDN_HEADS, b, nc, CHUNK)
        return g.transpose(1, 0, 2, 3)
    gb_row = gate_rows(FOX_HEADS)
    ga_row = gate_rows(FOX_HEADS + GDN_HEADS)
    gdn_o = _gdn(zb0, ga_row, gb_row,
                 l0_gdn_A_log.reshape(GDN_HEADS, 1, 1).astype(F32),
                 l0_gdn_dt_bias.reshape(GDN_HEADS, 1, 1).astype(F32), row(l0_gdn_o_norm), b, t)

    wo0 = l0_w_out.astype(BF)
    x1 = _mlp(fox_o, gdn_o, 0, x2, wo0[:fw], wo0[fw:], row(l0_ffn_norm),
              l0_w_ff1.astype(BF), l0_w_ff2.astype(BF), tm)

    wb1 = jnp.concatenate([l1_w_in[:, :d], l1_w_in[:, 2 * d:]], axis=1).astype(BF)
    wf1 = l1_w_in[:, d:2 * d].astype(BF)
    zb1, zf1 = _in1(x1, row(l1_mix_norm), wb1, wf1, tm0)
    ho = _hgrn(zb1, zf1, hgrn_lb_logits.astype(F32), row(l1_hgrn_o_norm), b, t)
    wo1 = l1_w_out.astype(BF)
    half = d // 2
    x3 = _mlp(ho, ho, 1, x1, wo1[:half], wo1[half:], row(l1_ffn_norm),
              l1_w_ff1.astype(BF), l1_w_ff2.astype(BF), tm)
    return x3.reshape(b, t, d)
```

```python
import functools

import jax
import jax.numpy as jnp
from jax import lax
from jax.experimental import pallas as pl
from jax.experimental.pallas import tpu as pltpu

BF = jnp.bfloat16
F32 = jnp.float32
NORM_EPS = 1e-6
LOG2E = 1.4426950408889634
CHUNK = 64
FOX_HEADS, FOX_HEAD_DIM = 8, 64
GDN_HEADS, GDN_HEAD_DIM = 4, 128
HGRN_HEADS, HGRN_HEAD_DIM = 8, 128
GDN_CONV = 4
IN0_ROWS = 128
LANES = 128
SUBLANES = 8
VMEM_LIMIT = 56 * 1024 * 1024


def _cparams(sem):
    return pltpu.CompilerParams(dimension_semantics=sem, vmem_limit_bytes=VMEM_LIMIT)


def _const_spec(shape):
    nd = len(shape)
    return pl.BlockSpec(shape, lambda *_: (0,) * nd, pipeline_mode=pl.Buffered(1))


def _mm(a, b):
    return jnp.dot(a.astype(BF), b.astype(BF), preferred_element_type=F32)


def _mm_nt(a, b):
    return lax.dot_general(a.astype(BF), b.astype(BF), (((1,), (1,)), ((), ())),
                           preferred_element_type=F32)


def _mm_tn(a, b):
    return lax.dot_general(a.astype(BF), b.astype(BF), (((0,), (0,)), ((), ())),
                           preferred_element_type=F32)


def _split3(x):
    hi = x.astype(BF)
    r1 = x - hi.astype(F32)
    mid = r1.astype(BF)
    lo = (r1 - mid.astype(F32)).astype(BF)
    return hi, mid, lo


def _mm_exact_lhs(m_bf, x):
    n = x.shape[1]
    if n % LANES:
        hi, mid, lo = _split3(x)
        d = lambda p: jnp.dot(m_bf, p, preferred_element_type=F32)
        return d(hi) + d(mid) + d(lo)
    y = jnp.dot(m_bf, jnp.concatenate(_split3(x), axis=1), preferred_element_type=F32)
    return y[:, :n] + y[:, n:2 * n] + y[:, 2 * n:]


def _mm_exact_rhs(x, m_bf):
    hi, mid, lo = _split3(x)
    d = lambda p: jnp.dot(p, m_bf, preferred_element_type=F32)
    return d(hi) + d(mid) + d(lo)


def _sigmoid(x):
    return 1.0 / (1.0 + jnp.exp(-x))


def _silu(x):
    return x * _sigmoid(x)


def _softplus(x):
    return jnp.maximum(x, 0.0) + jnp.log1p(jnp.exp(-jnp.abs(x)))


def _rms(x, gain):
    return x * lax.rsqrt(jnp.mean(x * x, axis=-1, keepdims=True) + NORM_EPS) * gain


def _in0_kernel(x_ref, g_ref, wb_ref, ws_ref, pm_ref, qkg_ref, cw_ref, zb_ref, zs_ref, zst_ref,
                hist_ref, *, tiles_per_seq):
    tm = x_ref.shape[0]
    hd = GDN_HEAD_DIM
    hrows = hist_ref.shape[1]

    @pl.when(pl.program_id(0) % tiles_per_seq == 0)
    def _():
        hist_ref[...] = jnp.zeros_like(hist_ref)

    h = _rms(x_ref[...], g_ref[...]).astype(BF)
    gw = 512
    rb = IN0_ROWS

    def project(c, r):
        return jnp.dot(h[r * rb:(r + 1) * rb], wb_ref[:, c * gw:(c + 1) * gw],
                       preferred_element_type=F32)

    def finish(c, r, z, above):
        if c < 2:
            ms = jnp.dot((z * z).astype(BF), pm_ref[...], preferred_element_type=F32)
            z = z * lax.rsqrt(ms + NORM_EPS) * qkg_ref[c:c + 1, :]
        elif 3 <= c < 6:
            j = c - 3
            w = cw_ref[:, j * gw:(j + 1) * gw]
            ext = jnp.concatenate([above, z], axis=0)
            y = z * w[GDN_CONV - 1:GDN_CONV]
            for s in range(1, GDN_CONV):
                y = y + ext[hrows - s:hrows - s + rb] * w[GDN_CONV - 1 - s:GDN_CONV - s]
            z = _silu(y)
            if j < 2:
                scale = 1.0 if j else hd ** -0.5
                z = jnp.concatenate(
                    [zh * (lax.rsqrt(jnp.sum(zh * zh, axis=-1, keepdims=True) + NORM_EPS) * scale)
                     for zh in (z[:, k * hd:(k + 1) * hd] for k in range(gw // hd))], axis=1)
        zb_ref[r * rb:(r + 1) * rb, c * gw:(c + 1) * gw] = z.astype(BF)

    pending = None
    for c in range(wb_ref.shape[1] // gw):
        conv = 3 <= c < 6
        above = hist_ref[c - 3] if conv else None
        for r in range(tm // rb):
            z = project(c, r)
            if pending is not None:
                finish(*pending)
            pending = (c, r, z, above)
            above = z[rb - hrows:, :] if conv else None
        if conv:
            hist_ref[c - 3] = above
    finish(*pending)
    zs = jnp.dot(h, ws_ref[...], preferred_element_type=F32)
    zs_ref[...] = zs
    zst_ref[...] = zs.T


def _in0(x2, gain, wb, ws, pm, qkg, conv_w, tm, t):
    n, d = x2.shape
    nb = wb.shape[1]
    return pl.pallas_call(
        functools.partial(_in0_kernel, tiles_per_seq=t // tm),
        grid=(n // tm,),
        in_specs=[pl.BlockSpec((tm, d), lambda i: (i, 0)),
                  _const_spec((1, d)), _const_spec(wb.shape), _const_spec(ws.shape),
                  _const_spec(pm.shape), _const_spec(qkg.shape), _const_spec(conv_w.shape)],
        out_specs=[pl.BlockSpec((tm, nb), lambda i: (i, 0)),
                   pl.BlockSpec((tm, LANES), lambda i: (i, 0)),
                   pl.BlockSpec((LANES, tm), lambda i: (0, i))],
        out_shape=[jax.ShapeDtypeStruct((n, nb), BF), jax.ShapeDtypeStruct((n, LANES), F32),
                   jax.ShapeDtypeStruct((LANES, n), F32)],
        scratch_shapes=[pltpu.VMEM((3, SUBLANES, 512), F32)],
        compiler_params=_cparams(("arbitrary",)),
        name="l0_in_proj",
    )(x2, gain, wb, ws, pm, qkg, conv_w)


FOX_BIAS_LANES = 8


def _fox_cum_kernel(zs_ref, bias_ref, qb_ref, kb_ref, *, blk):
    t = zs_ref.shape[0]
    r = lax.broadcasted_iota(jnp.int32, (blk, blk), 0)
    c = lax.broadcasted_iota(jnp.int32, (blk, blk), 1)
    tril = (r >= c).astype(BF)
    pr = lax.broadcasted_iota(jnp.int32, (3 * LANES, 2 * LANES), 0)
    pc = lax.broadcasted_iota(jnp.int32, (3 * LANES, 2 * LANES), 1)
    place = ((pr % LANES < FOX_HEADS)
             & (pc % LANES == (pr % LANES) * FOX_BIAS_LANES + pr // LANES + 3 * (pc // LANES))
             ).astype(BF)
    lane = lax.broadcasted_iota(jnp.int32, (1, LANES), 1)
    slot = lane % FOX_BIAS_LANES
    used = lane < FOX_HEADS * FOX_BIAS_LANES
    ones_q = jnp.where(used & (slot >= 3) & (slot < 6), 1.0, 0.0)
    ones_k = jnp.where(used & (slot < 3), 1.0, 0.0)
    blocks = [slice(i * blk, (i + 1) * blk) for i in range(t // blk)]
    local = []
    for rows in blocks:
        z = zs_ref[rows, :] + bias_ref[...]
        ls = jnp.minimum(z, 0.0) - jnp.log1p(jnp.exp(-jnp.abs(z)))
        local.append(_mm_exact_lhs(tril, ls))
    carry = jnp.zeros((1, LANES), F32)
    for rows, loc in zip(blocks, local):
        cb = loc + carry
        carry = cb[blk - 1:blk, :]
        feat = jnp.dot(jnp.concatenate(_split3(cb * LOG2E), axis=1), place,
                       preferred_element_type=F32)
        qb_ref[rows, :] = (feat[:, :LANES] + ones_q).astype(BF)
        kb_ref[rows, :] = (ones_k - feat[:, LANES:]).astype(BF)


def _fox_cum(zs, bias_row, b, t):
    n = zs.shape[0]
    spec = pl.BlockSpec((t, LANES), lambda bi: (bi, 0))
    return pl.pallas_call(
        functools.partial(_fox_cum_kernel, blk=256),
        grid=(b,),
        in_specs=[spec, _const_spec(bias_row.shape)],
        out_specs=[spec, spec],
        out_shape=[jax.ShapeDtypeStruct((n, LANES), BF)] * 2,
        compiler_params=_cparams(("parallel",)),
        name="fox_cum_gate",
    )(zs, bias_row)


def _fox_kernel(q_ref, k_ref, v_ref, qb_ref, kb_ref, o_ref, *, tq):
    pair = pl.program_id(1)
    t = q_ref.shape[0]
    lane = lax.broadcasted_iota(jnp.int32, (1, LANES), 1)
    zero = jnp.zeros((tq, LANES), BF)
    first = lane < FOX_HEAD_DIM
    bias_head = lane // FOX_BIAS_LANES
    ones = jnp.ones((tq, LANES), BF)
    on_diag = (lax.broadcasted_iota(jnp.int32, (2 * tq, tq), 1)
               <= lax.broadcasted_iota(jnp.int32, (2 * tq, tq), 0) % tq)

    def scores(i):
        rows_i = slice(i * tq, (i + 1) * tq)
        kv = (i + 1) * tq
        q2, qb = q_ref[rows_i, :], qb_ref[rows_i, :]
        rows = []
        for h in range(2):
            qh = jnp.where(first if h == 0 else jnp.logical_not(first), q2, zero)
            qbh = jnp.where(bias_head == 2 * pair + h, qb, zero)
            rows.append(jnp.concatenate([qh, qbh], axis=1))
        ka = jnp.concatenate([k_ref[:kv, :], kb_ref[:kv, :]], axis=1)
        s = lax.dot_general(jnp.concatenate(rows, axis=0), ka, (((1,), (1,)), ((), ())),
                            preferred_element_type=F32)
        diag = jnp.where(on_diag, s[:, kv - tq:], -jnp.inf)
        return diag if i == 0 else jnp.concatenate([s[:, :kv - tq], diag], axis=1)

    def finish(i, s):
        kv = (i + 1) * tq
        va = jnp.concatenate([v_ref[:kv, :], jnp.concatenate([ones] * (i + 1), axis=0)], axis=1)
        halves = []
        for sh in (s[:tq], s[tq:]):
            p = jnp.exp2(sh - jnp.max(sh, axis=-1, keepdims=True)).astype(BF)
            acc = jnp.dot(p, va, preferred_element_type=F32)
            halves.append(acc[:, :LANES] / acc[:, LANES:])
        o_ref[i * tq:(i + 1) * tq, :] = jnp.where(first, halves[0], halves[1]).astype(o_ref.dtype)

    s = scores(0)
    for i in range(t // tq):
        s_next = scores(i + 1) if (i + 1) * tq < t else None
        finish(i, s)
        s = s_next


def _fox_attention(zb, qbias, kbias, b, t, tq):
    n = zb.shape[0]
    hp = FOX_HEADS // 2
    return pl.pallas_call(
        functools.partial(_fox_kernel, tq=tq),
        grid=(b, hp),
        in_specs=[pl.BlockSpec((t, LANES), lambda bi, p: (bi, p)),
                  pl.BlockSpec((t, LANES), lambda bi, p: (bi, hp + p)),
                  pl.BlockSpec((t, LANES), lambda bi, p: (bi, 2 * hp + p)),
                  pl.BlockSpec((t, LANES), lambda bi, p: (bi, 0)),
                  pl.BlockSpec((t, LANES), lambda bi, p: (bi, 0))],
        out_specs=pl.BlockSpec((t, LANES), lambda bi, p: (bi, p)),
        out_shape=jax.ShapeDtypeStruct((n, FOX_HEADS * FOX_HEAD_DIM), BF),
        compiler_params=_cparams(("parallel", "parallel")),
        name="fox_attention",
    )(zb, zb, zb, qbias, kbias)


def _neumann_inverse(mats, eye):
    ps = [eye - a for a in mats]
    aps = list(mats)
    k = 2
    while k < CHUNK:
        aps = [_mm(ap, ap) for ap in aps]
        yield
        ps = [p + _mm(p, ap) for p, ap in zip(ps, aps)]
        yield
        k *= 2
    return ps


def _gdn_kernel(alog_ref, dtb_ref, q_ref, k_ref, v_ref, gg_ref, gar_ref, gbr_ref,
                on_ref, o_ref, lhs_s, ku_s, u_s, qk_s, pend_s, grow_s, gcol_s, bcol_s, s_s):
    t = q_ref.shape[0]
    nc = t // CHUNK
    hd = GDN_HEAD_DIM
    lrows = hd + 2 * CHUNK

    ri = lax.broadcasted_iota(jnp.int32, (CHUNK, CHUNK), 0)
    ci = lax.broadcasted_iota(jnp.int32, (CHUNK, CHUNK), 1)
    lower = ri >= ci
    strict = ri > ci
    triu = (ri <= ci).astype(BF)
    eye = (ri == ci).astype(F32)
    eye_bf = eye.astype(BF)

    def transposed(x):
        return sum(lax.dot_general(eye_bf, p, (((1,), (1,)), ((), ())),
                                   preferred_element_type=F32) for p in _split3(x))

    for h in range(GDN_HEADS):
        a = -jnp.exp(alog_ref[h])
        g_row = _mm_exact_rhs(a * _softplus(gar_ref[0, h] + dtb_ref[h]), triu)
        grow_s[h] = g_row
        gcol_s[h] = transposed(g_row)
        bcol_s[h] = transposed(_sigmoid(gbr_ref[0, h]))
    s_s[...] = jnp.zeros_like(s_s)

    lane_nc = lax.broadcasted_iota(jnp.int32, (CHUNK, nc), 1)
    heads = range(GDN_HEADS)
    cat = lambda parts: jnp.concatenate(parts, axis=1)

    per_iter = 4

    hsl = lambda h: slice(h * hd, (h + 1) * hd)

    def chunk_rows(n, size):
        return pl.ds(n * size if isinstance(n, int) else pl.multiple_of(n * size, size), size)

    def local_work(it):
        rows = chunk_rows(it, per_iter * CHUNK)
        chains = [(c, h) for c in range(per_iter) for h in heads]
        csl = lambda c: slice(c * CHUNK, (c + 1) * CHUNK)
        q_all, k_all, v_all = (r[rows, :].astype(F32) for r in (q_ref, k_ref, v_ref))
        q = [q_all[csl(c), hsl(h)] for c, h in chains]
        k = [k_all[csl(c), hsl(h)] for c, h in chains]
        v = [v_all[csl(c), hsl(h)] for c, h in chains]
        sel = [lane_nc == it * per_iter + c for c in range(per_iter)]
        g_c = [jnp.sum(jnp.where(sel[c], gcol_s[h], 0.0), axis=1, keepdims=True)
               for c, h in chains]
        b_c = [jnp.sum(jnp.where(sel[c], bcol_s[h], 0.0), axis=1, keepdims=True)
               for c, h in chains]
        g_r = [grow_s[h, pl.ds(it * per_iter + c, 1), :] for c, h in chains]
        yield
        decay = [jnp.exp(jnp.where(lower, gc - gr, -jnp.inf)) for gc, gr in zip(g_c, g_r)]
        kb = [x * bc for x, bc in zip(k, b_c)]
        kk = [_mm_nt(x, y) for x, y in zip(kb, k)]
        yield
        qk = [_mm_nt(x, y) for x, y in zip(q, k)]
        yield
        t_inv = yield from _neumann_inverse(
            [jnp.where(strict, x * d, 0.0) for x, d in zip(kk, decay)], eye)
        eg = [jnp.exp(gc) for gc in g_c]
        uw = [_mm(ti, jnp.concatenate([x * bc, y * e], axis=1))
              for ti, x, bc, y, e in zip(t_inv, v, b_c, kb, eg)]
        yield
        kd = [x * jnp.exp(gr[:, CHUNK - 1:CHUNK] - gc) for x, gr, gc in zip(k, g_r, g_c)]
        kuw = [_mm_tn(x, y) for x, y in zip(kd, uw)]
        yield
        grid = lambda parts: jnp.concatenate(
            [cat(parts[c * GDN_HEADS:(c + 1) * GDN_HEADS]) for c in range(per_iter)], axis=0)
        lhs = [jnp.concatenate([m[:, hd:], x[:, hd:], y * e], axis=0)
               for m, x, y, e in zip(kuw, uw, q, eg)]
        lhs_s[chunk_rows(it, per_iter * lrows), :] = grid(lhs).astype(BF)
        ku_s[chunk_rows(it, per_iter * hd), :] = grid([m[:, :hd] for m in kuw])
        u_s[rows, :] = grid([x[:, :hd] for x in uw])
        qk_s[rows, :] = grid([x * d for x, d in zip(qk, decay)]).astype(BF)

    def advance(n):
        s_prev = [s_s[h] for h in heads]
        prod = [_mm(lhs_s[chunk_rows(n, lrows), hsl(h)], s_prev[h]) for h in heads]
        g_last = [grow_s[h, pl.ds(n, 1), :][:, CHUNK - 1:CHUNK] for h in heads]
        s_s[...] = jnp.stack([s_prev[h] * jnp.exp(g_last[h]) - prod[h][:hd]
                              + ku_s[chunk_rows(n, hd), hsl(h)] for h in heads])
        pend_s[...] = cat([prod[h][hd:] for h in heads])

    def emit(n):
        rows = chunk_rows(n, CHUNK)
        pend = pend_s[...]
        v_new = [u_s[rows, hsl(h)] - pend[:CHUNK, hsl(h)] for h in heads]
        qv = [_mm(qk_s[rows, h * CHUNK:(h + 1) * CHUNK], v_new[h]) for h in heads]
        outs = [_rms(pend[CHUNK:, hsl(h)] + qv[h], on_ref[...]) for h in heads]
        gate = _silu(gg_ref[rows, :].astype(F32))
        o_ref[rows, :] = (cat(outs) * gate).astype(o_ref.dtype)

    def state_work(first, has_previous):
        for c in range(per_iter):
            if has_previous or c:
                emit(first + c - 1)
                yield
            advance(first + c)
            yield

    def interleave(*gens):
        active = list(enumerate(gens))
        while active:
            for item in list(active):
                try:
                    for _ in range(2 if item[0] else 1):
                        next(item[1])
                except StopIteration:
                    active.remove(item)

    n_iter = nc // per_iter
    interleave(local_work(0))
    if n_iter > 1:
        interleave(local_work(1), state_work(0, False))

    def merged_step(it, carry):
        interleave(local_work(it), state_work((it - 1) * per_iter, True))
        return carry

    lax.fori_loop(2, n_iter, merged_step, 0)
    interleave(state_work((n_iter - 1) * per_iter, n_iter > 1))
    emit(nc - 1)


def _gdn(zb, ga_row, gb_row, a_log, dt_bias, o_norm, b, t):
    n = zb.shape[0]
    nc = t // CHUNK
    width = GDN_HEADS * GDN_HEAD_DIM
    fox_blocks = 3 * FOX_HEADS * FOX_HEAD_DIM // width
    colspec = lambda j: pl.BlockSpec((t, width), lambda bi: (bi, fox_blocks + j))
    gspec = lambda shp: pl.BlockSpec((1,) + shp, lambda bi: (bi, 0, 0, 0))
    return pl.pallas_call(
        _gdn_kernel,
        grid=(b,),
        in_specs=[_const_spec(a_log.shape), _const_spec(dt_bias.shape),
                  colspec(0), colspec(1), colspec(2), colspec(3),
                  gspec((GDN_HEADS, nc, CHUNK)), gspec((GDN_HEADS, nc, CHUNK)),
                  _const_spec(o_norm.shape)],
        out_specs=pl.BlockSpec((t, width), lambda bi: (bi, 0)),
        out_shape=jax.ShapeDtypeStruct((n, width), BF),
        scratch_shapes=[pltpu.VMEM((nc * (GDN_HEAD_DIM + 2 * CHUNK), width), BF),
                        pltpu.VMEM((nc * GDN_HEAD_DIM, width), F32),
                        pltpu.VMEM((t, width), F32),
                        pltpu.VMEM((t, GDN_HEADS * CHUNK), BF),
                        pltpu.VMEM((2 * CHUNK, width), F32),
                        pltpu.VMEM((GDN_HEADS, nc, CHUNK), F32),
                        pltpu.VMEM((GDN_HEADS, CHUNK, nc), F32),
                        pltpu.VMEM((GDN_HEADS, CHUNK, nc), F32),
                        pltpu.VMEM((GDN_HEADS, GDN_HEAD_DIM, GDN_HEAD_DIM), F32)],
        compiler_params=_cparams(("parallel",)),
        name="gated_deltanet",
    )(a_log, dt_bias, zb, zb, zb, zb, ga_row, gb_row, o_norm)


def _mlp_kernel(a1_ref, a2_ref, x_ref, wo1_ref, wo2_ref, g_ref, w1_ref, w2_ref, o_ref, acc_ref,
                *, fc):
    mix = (jnp.dot(a1_ref[...], wo1_ref[...], preferred_element_type=F32)
           + jnp.dot(a2_ref[...], wo2_ref[...], preferred_element_type=F32))
    x1 = x_ref[...] + mix
    h = _rms(x1, g_ref[...]).astype(BF)
    acc_ref[...] = x1
    for c in range(w1_ref.shape[1] // fc):
        a = jnp.maximum(jnp.dot(h, w1_ref[:, c * fc:(c + 1) * fc],
                                preferred_element_type=F32), 0.0)
        acc_ref[...] += jnp.dot((a * a).astype(BF), w2_ref[c * fc:(c + 1) * fc, :],
                                preferred_element_type=F32)
    o_ref[...] = acc_ref[...]


def _mlp(a1, a2, a2_block, x2, wo1, wo2, gain, w1, w2, tm):
    n, d = x2.shape
    half = wo1.shape[0]
    return pl.pallas_call(
        functools.partial(_mlp_kernel, fc=1024),
        grid=(n // tm,),
        in_specs=[pl.BlockSpec((tm, half), lambda i: (i, 0)),
                  pl.BlockSpec((tm, half), lambda i: (i, a2_block)),
                  pl.BlockSpec((tm, d), lambda i: (i, 0)),
                  _const_spec(wo1.shape), _const_spec(wo2.shape), _const_spec(gain.shape),
                  _const_spec(w1.shape), _const_spec(w2.shape)],
        out_specs=pl.BlockSpec((tm, d), lambda i: (i, 0)),
        out_shape=jax.ShapeDtypeStruct((n, d), F32),
        scratch_shapes=[pltpu.VMEM((tm, d), F32)],
        compiler_params=_cparams(("parallel",)),
        name="out_proj_mlp",
    )(a1, a2, x2, wo1, wo2, gain, w1, w2)


def _in1_kernel(x_ref, g_ref, wb_ref, wf_ref, zb_ref, zf_ref):
    h = _rms(x_ref[...], g_ref[...]).astype(BF)
    gw = 512
    for c in range(wb_ref.shape[1] // gw):
        zb_ref[:, c * gw:(c + 1) * gw] = jnp.dot(
            h, wb_ref[:, c * gw:(c + 1) * gw], preferred_element_type=F32).astype(BF)
    for c in range(wf_ref.shape[1] // gw):
        zf_ref[:, c * gw:(c + 1) * gw] = jnp.dot(
            h, wf_ref[:, c * gw:(c + 1) * gw], preferred_element_type=F32)


def _in1(x2, gain, wb, wf, tm):
    n, d = x2.shape
    return pl.pallas_call(
        _in1_kernel,
        grid=(n // tm,),
        in_specs=[pl.BlockSpec((tm, d), lambda i: (i, 0)),
                  _const_spec(gain.shape), _const_spec(wb.shape), _const_spec(wf.shape)],
        out_specs=[pl.BlockSpec((tm, wb.shape[1]), lambda i: (i, 0)),
                   pl.BlockSpec((tm, wf.shape[1]), lambda i: (i, 0))],
        out_shape=[jax.ShapeDtypeStruct((n, wb.shape[1]), BF),
                   jax.ShapeDtypeStruct((n, wf.shape[1]), F32)],
        compiler_params=_cparams(("parallel",)),
        name="l1_in_proj",
    )(x2, gain, wb, wf)


HG_GROUP = 4
HG_HEADS_PER_STEP = 2


HG_LEVELS = (32, 16, 8, 4, 2, 1)


def _hgrn_term_mask(terms):
    t = lax.broadcasted_iota(jnp.int32, (CHUNK, len(terms) * CHUNK), 0)
    col = lax.broadcasted_iota(jnp.int32, (CHUNK, len(terms) * CHUNK), 1)
    s = col % CHUNK
    mask = None
    for j, m in enumerate(terms):
        own = (t == s) if m == 0 else (
            (t % (2 * m) >= m) & (s % (2 * m) < m) & (t // (2 * m) == s // (2 * m)))
        own = own & (col // CHUNK == j)
        mask = own if mask is None else mask | own
    return mask


def _hgrn_level_operands(q, k, b, ref_row):
    d = q.shape[1]
    nb = CHUNK // SUBLANES
    blk = lambda x, i: x[SUBLANES * i:SUBLANES * (i + 1)]
    sub = lax.broadcasted_iota(jnp.int32, (SUBLANES, d), 0)
    xs = []
    for m in HG_LEVELS:
        parts = []
        for i in range(nb):
            bi, qi, ki = blk(b, i), blk(q, i), blk(k, i)
            row0 = SUBLANES * i
            if m >= SUBLANES:
                ref = row0 // (2 * m) * (2 * m) + m
                if row0 % (2 * m) >= m:
                    parts.append(qi * jnp.exp2(bi - ref_row(ref)))
                else:
                    parts.append(ki * jnp.exp2(ref_row(ref) - bi))
                continue
            upper = sub % (2 * m) >= m
            if m == 4:
                bref = ref_row(row0 + 4)
            elif m == 2:
                bref = jnp.where(sub < 4, ref_row(row0 + 2), ref_row(row0 + 6))
            else:
                bref = jnp.where(upper, bi, pltpu.roll(bi, SUBLANES - 1, 0))
            dlt = bi - bref
            parts.append(jnp.where(upper, qi, ki) * jnp.exp2(jnp.where(upper, dlt, -dlt)))
        xs.append(jnp.concatenate(parts, axis=0).astype(BF))
        yield
    return xs


def _hgrn_kernel(zq_ref, zi_ref, zg_ref, zf_ref, lbl_ref, on_ref, o_ref,
                 oi_s, qd_s, kv_s, dec_s, h1_s, *, group):
    t = zq_ref.shape[0]
    d = HGRN_HEAD_DIM
    heads = range(zq_ref.shape[1] // d)
    hs = lambda hh: slice(hh * d, (hh + 1) * d)
    gr = group * CHUNK
    ng = t // gr
    ri = lax.broadcasted_iota(jnp.int32, (gr, gr), 0)
    ci = lax.broadcasted_iota(jnp.int32, (gr, gr), 1)
    tril = ((ri >= ci) & (ri // CHUNK == ci // CHUNK)).astype(BF)
    term_mask = {m: _hgrn_term_mask((m,)) for m in (0,) + HG_LEVELS}
    sls =[slice(c * CHUNK, (c + 1) * CHUNK) for c in range(group)]

    l0 = lbl_ref[0:1, :]
    l1 = lbl_ref[1:2, :]
    mx = jnp.maximum(l0, l1)
    e0 = jnp.exp(l0 - mx)
    e1 = jnp.exp(l1 - mx)
    s0 = e0 / (e0 + e1)
    s1 = e1 / (e0 + e1)
    lb = (s0 + s1) - s0

    def group_rows(g):
        return pl.ds(g * gr, gr)

    def prelude(hh, g):
        rows = group_rows(g)
        qs, ks, ls = [], [], []
        for sl in sls:
            qs.append(_silu(zq_ref[rows, hs(hh)][sl].astype(F32)))
            f = lb[:, hs(hh)] + (1.0 - lb[:, hs(hh)]) * _sigmoid(zf_ref[rows, hs(hh)][sl])
            ks.append(1.0 - f)
            ls.append(jnp.log2(f))
            yield
        q, k = jnp.concatenate(qs, axis=0), jnp.concatenate(ks, axis=0)
        b = _mm_exact_lhs(tril, jnp.concatenate(ls, axis=0))
        h1_s[hh, g, 0] = b
        h1_s[hh, g, 1] = k
        h1_s[hh, g, 2] = q
        qd_s[hh, rows, :] = (q * jnp.exp2(b)).astype(BF)
        yield

    def operands(hh, g):
        b, k, q = h1_s[hh, g, 0], h1_s[hh, g, 1], h1_s[hh, g, 2]
        ops = []
        for c, sl in enumerate(sls):
            ref_row = lambda r, c=c: h1_s[hh, g, 0, c * CHUNK + r:c * CHUNK + r + 1, :]
            xs = yield from _hgrn_level_operands(q[sl], k[sl], b[sl], ref_row)
            b_last = b[sl][CHUNK - 1:CHUNK, :]
            kdec = (k[sl] * jnp.exp2(b_last - b[sl])).astype(BF)
            dec_s[hh, g * group + c] = jnp.exp2(b_last)
            ops.append((xs, q[sl].astype(BF), k[sl].astype(BF), kdec))
            yield
        return ops

    def products(hh, g, ops):
        rows = group_rows(g)
        v = zi_ref[rows, hs(hh)]
        dmat, kv, oi = [], [], []
        for (xs, qb, kb, kdec), sl in zip(ops, sls):
            dm = jnp.where(term_mask[0], _mm_nt(qb, kb), 0.0)
            yield
            for m, x in zip(HG_LEVELS, xs):
                dm = jnp.where(term_mask[m], _mm_nt(x, x), dm)
                yield
            kv.append(_mm_tn(v[sl], kdec))
            dmat.append(dm)
            yield
        for dm, sl in zip(dmat, sls):
            oi.append(_mm(dm, v[sl]))
            yield
        oi_s[hh, rows, :] = jnp.concatenate(oi, axis=0)
        kv_s[hh, pl.ds(g * group, group)] = jnp.stack(kv)

    def advance(*stages):
        results = [None] * len(stages)
        active = list(enumerate(stages))
        while active:
            for item in list(active):
                j, (gen, turns) = item
                try:
                    for _ in range(turns):
                        next(gen)
                except StopIteration as done:
                    results[j] = done.value
                    active.remove(item)
        return results

    for hh in heads:
        advance((prelude(hh, 0), 1), (prelude(hh, 1), 1))
        ops = advance((operands(hh, 0), 1))[0]
        for g in range(1, ng):
            stages = [(products(hh, g - 1, ops), 1), (operands(hh, g), 1)]
            if g + 1 < ng:
                stages.append((prelude(hh, g + 1), 1))
            ops = advance(*stages)[1]
        advance((products(hh, ng - 1, ops), 1))

    per_step = 4 * group
    sr = per_step * CHUNK

    def state_step(g, sts):
        r0 = pl.multiple_of(g * sr, sr)
        rows = pl.ds(r0, sr)
        new_sts, normed = [], []
        for hh, st in zip(heads, sts):
            states = []
            for c in range(per_step):
                n = g * per_step + c
                states.append(st)
                st = st * dec_s[hh, n] + kv_s[hh, n]
            new_sts.append(st)
            outs = [oi_s[hh, pl.ds(r0 + c * CHUNK, CHUNK), :]
                    + _mm_nt(qd_s[hh, pl.ds(r0 + c * CHUNK, CHUNK), :], states[c])
                    for c in range(per_step)]
            normed.append(_rms(jnp.concatenate(outs, axis=0), on_ref[...]))
        o = jnp.concatenate(normed, axis=1)
        o_ref[rows, :] = (o * _silu(zg_ref[rows, :].astype(F32))).astype(o_ref.dtype)
        return tuple(new_sts)

    lax.fori_loop(0, t // sr, state_step, tuple(jnp.zeros((d, d), F32) for _ in heads))


def _hgrn(zb, zf, lb_logits, o_norm, b, t):
    n = zb.shape[0]
    hd = HGRN_HEAD_DIM
    nh = HGRN_HEADS
    nc = t // CHUNK
    gr = HG_GROUP * CHUNK
    hps = HG_HEADS_PER_STEP
    ns = nh // hps
    wide = hps * hd
    return pl.pallas_call(
        functools.partial(_hgrn_kernel, group=HG_GROUP),
        grid=(b, ns),
        in_specs=[pl.BlockSpec((t, wide), lambda bi, h: (bi, h)),
                  pl.BlockSpec((t, wide), lambda bi, h: (bi, ns + h)),
                  pl.BlockSpec((t, wide), lambda bi, h: (bi, 2 * ns + h)),
                  pl.BlockSpec((t, wide), lambda bi, h: (bi, h)),
                  pl.BlockSpec((lb_logits.shape[0], wide), lambda bi, h: (0, h)),
                  _const_spec(o_norm.shape)],
        out_specs=pl.BlockSpec((t, wide), lambda bi, h: (bi, h)),
        out_shape=jax.ShapeDtypeStruct((n, nh * hd), BF),
        scratch_shapes=[pltpu.VMEM((hps, t, hd), F32), pltpu.VMEM((hps, t, hd), BF),
                        pltpu.VMEM((hps, nc, hd, hd), F32), pltpu.VMEM((hps, nc, 1, hd), F32),
                        pltpu.VMEM((hps, t // gr, 3, gr, hd), F32)],
        compiler_params=_cparams(("parallel", "parallel")),
        name="hgrn2",
    )(zb, zb, zb, zf, lb_logits, o_norm)


def kernel(x, l0_mix_norm, l0_w_in, l0_fox_q_norm, l0_fox_k_norm, l0_fox_f_bias, l0_gdn_conv,
           l0_gdn_A_log, l0_gdn_dt_bias, l0_gdn_o_norm, l0_w_out, l0_ffn_norm, l0_w_ff1,
           l0_w_ff2, l1_mix_norm, l1_w_in, l1_hgrn_o_norm, l1_w_out, l1_ffn_norm, l1_w_ff1,
           l1_w_ff2, hgrn_lb_logits):
    b, t, d = x.shape
    n = b * t
    fw = FOX_HEADS * FOX_HEAD_DIM
    gw = GDN_HEADS * GDN_HEAD_DIM
    assert hgrn_lb_logits.shape[0] == 2 and d == HGRN_HEADS * HGRN_HEAD_DIM
    assert t % (2 * HG_GROUP * CHUNK) == 0
    tm = 512 if t % 512 == 0 else 256
    tq = 256
    nc = t // CHUNK
    row = lambda p: p.reshape(1, -1).astype(F32)
    x2 = x.reshape(n, d)

    o_ff = 3 * fw
    o_gq = o_ff + FOX_HEADS
    o_gb = o_gq + 3 * gw
    o_gg = o_gb + 2 * GDN_HEADS
    wb0 = jnp.concatenate([l0_w_in[:, :o_ff], l0_w_in[:, o_gq:o_gb], l0_w_in[:, o_gg:]],
                          axis=1).astype(BF)
    n_small = FOX_HEADS + 2 * GDN_HEADS
    ws0 = jnp.concatenate([l0_w_in[:, o_ff:o_gq], l0_w_in[:, o_gb:o_gg],
                           jnp.zeros((d, LANES - n_small), F32)], axis=1).astype(BF)
    hid = jnp.arange(fw) // FOX_HEAD_DIM
    pm = jnp.where(hid[:, None] == hid[None, :], 1.0 / FOX_HEAD_DIM, 0.0).astype(BF)
    qkg = jnp.stack([jnp.tile(l0_fox_q_norm, FOX_HEADS) * (FOX_HEAD_DIM ** -0.5 * LOG2E),
                     jnp.tile(l0_fox_k_norm, FOX_HEADS)]).astype(F32)

    tm0 = 1024 if t % 1024 == 0 else tm
    zb0, zs0, zst0 = _in0(x2, row(l0_mix_norm), wb0, ws0, pm, qkg, l0_gdn_conv.astype(F32),
                          tm0, t)

    f_bias = jnp.pad(l0_fox_f_bias.astype(F32), (0, LANES - FOX_HEADS)).reshape(1, LANES)
    qbias, kbias = _fox_cum(zs0, f_bias, b, t)
    fox_o = _fox_attention(zb0, qbias, kbias, b, t, tq)

    def gate_rows(lo):
        g = zst0[lo:lo + GDN_HEADS].reshape(GDN_HEADS, b, nc, CHUNK)
        return g.transpose(1, 0, 2, 3)
    gb_row = gate_rows(FOX_HEADS)
    ga_row = gate_rows(FOX_HEADS + GDN_HEADS)
    gdn_o = _gdn(zb0, ga_row, gb_row,
                 l0_gdn_A_log.reshape(GDN_HEADS, 1, 1).astype(F32),
                 l0_gdn_dt_bias.reshape(GDN_HEADS, 1, 1).astype(F32), row(l0_gdn_o_norm), b, t)

    wo0 = l0_w_out.astype(BF)
    x1 = _mlp(fox_o, gdn_o, 0, x2, wo0[:fw], wo0[fw:], row(l0_ffn_norm),
              l0_w_ff1.astype(BF), l0_w_ff2.astype(BF), tm)

    wb1 = jnp.concatenate([l1_w_in[:, :d], l1_w_in[:, 2 * d:]], axis=1).astype(BF)
    wf1 = l1_w_in[:, d:2 * d].astype(BF)
    zb1, zf1 = _in1(x1, row(l1_mix_norm), wb1, wf1, tm0)
    ho = _hgrn(zb1, zf1, hgrn_lb_logits.astype(F32), row(l1_hgrn_o_norm), b, t)
    wo1 = l1_w_out.astype(BF)
    half = d // 2
    x3 = _mlp(ho, ho, 1, x1, wo1[:half], wo1[half:], row(l1_ffn_norm),
              l1_w_ff1.astype(BF), l1_w_ff2.astype(BF), tm)
    return x3.reshape(b, t, d)
```
